```python
import math
import jax, jax.numpy as jnp
from jax import lax
import numpy as np

D_MODEL = 2048
BATCH = 2
SEQ = 8192
DEPTH = 1

D_MIX = D_MODEL
D_A = D_MIX // 2
D_B = D_MIX - D_A
N_HEADS_A = 8
HD_A = D_A // N_HEADS_A
N_GROUPS_B = 8
CHUNK = 128
CONV_W = 3
EMB_DIM = 33
N_BANDS = (EMB_DIM - 1) // 2
FILTER_ORDER = 64
DECAY_TARGET = 1e-2
FAST_DECAY_PCT = 0.3
SLOW_DECAY_PCT = 1.5
D_FF = 5504
D_IN = 2 * D_A + 3 * D_B
EPS = 1e-6

kernel_name = "hybrid_sgu_hyena_convglu_encoder"


def rms_norm(x, g):
    xf = x.astype(jnp.float32)
    y = xf * lax.rsqrt(jnp.mean(xf * xf, axis=-1, keepdims=True) + EPS)
    return (y * g.astype(jnp.float32)).astype(x.dtype)


def dwconv3(x, w, b):
    xp = jnp.pad(x, ((0, 0), (1, 1), (0, 0)))
    return xp[:, :-2] * w[0] + xp[:, 1:-1] * w[1] + xp[:, 2:] * w[2] + b


def positional_features(L):
    t = jnp.linspace(0.0, 1.0, L, dtype=jnp.float32)[:, None]
    w = (2.0 * math.pi / L) * jnp.arange(L, dtype=jnp.float32)[:, None]
    f = jnp.linspace(1e-4, N_BANDS - 1, N_BANDS, dtype=jnp.float32)[None]
    z = jnp.concatenate([t, jnp.cos(f * w), -jnp.sin(f * w)], axis=-1)
    return z, t


def hyena_filter(L, w1, b1, w2, b2, w3, b3, freq, wout):
    f32 = jnp.float32
    z, t = positional_features(L)
    fr = freq.astype(f32)
    h = jnp.sin(fr * (z @ w1.astype(f32) + b1.astype(f32)))
    h = jnp.sin(fr * (h @ w2.astype(f32) + b2.astype(f32)))
    h = jnp.sin(fr * (h @ w3.astype(f32) + b3.astype(f32)))
    h = h @ wout.astype(f32)
    max_decay = math.log(DECAY_TARGET) / FAST_DECAY_PCT
    min_decay = math.log(DECAY_TARGET) / SLOW_DECAY_PCT
    deltas = jnp.abs(jnp.linspace(min_decay, max_decay, D_B, dtype=f32))
    window = jnp.exp(-t * jnp.concatenate([deltas, deltas])[None])
    h = h * window
    h_fwd, h_bwd = h[:, :D_B], h[:, D_B:]
    k = jnp.concatenate([h_fwd, jnp.zeros((1, D_B), f32), h_bwd[:0:-1]], axis=0)
    l1 = jnp.sum(jnp.abs(k), axis=0, keepdims=True)
    return k / (l1 + EPS)


def bidir_long_conv(v, k):
    L = v.shape[1]
    V = jnp.fft.rfft(v.astype(jnp.float32), n=2 * L, axis=1)
    K = jnp.fft.rfft(k, n=2 * L, axis=0)
    return jnp.fft.irfft(V * K[None], n=2 * L, axis=1)[:, :L]


def setup_inputs(seed: int = 0) -> dict:
    key = jax.random.key(seed)
    ks = jax.random.split(key, 32)

    def nrm(k, shape, scale):
        return jax.random.normal(k, shape, jnp.float32) * scale

    def gain(k, shape):
        return 1.0 + 0.02 * jax.random.normal(k, shape, jnp.float32)

    Dp = DEPTH
    return {
        "x": nrm(ks[0], (BATCH, SEQ, D_MODEL), 1.0),
        "norm1_g": gain(ks[1], (Dp, D_MODEL)),
        "w_in": nrm(ks[2], (Dp, D_MODEL, D_IN), D_MODEL ** -0.5),
        "sgu_norm_g": gain(ks[3], (Dp, D_A)),
        "sgu_w": nrm(ks[4], (Dp, N_HEADS_A, CHUNK, CHUNK), CHUNK ** -0.5),
        "sgu_b": gain(ks[5], (Dp, N_HEADS_A, CHUNK)),
        "hy_conv_w": nrm(ks[6], (Dp, CONV_W, 3 * D_B), CONV_W ** -0.5),
        "hy_conv_b": nrm(ks[7], (Dp, 3 * D_B), 0.02),
        "hy_f_w1": nrm(ks[8], (Dp, EMB_DIM, FILTER_ORDER), EMB_DIM ** -0.5),
        "hy_f_b1": nrm(ks[9], (Dp, FILTER_ORDER), 0.1),
        "hy_f_w2": nrm(ks[10], (Dp, FILTER_ORDER, FILTER_ORDER), FILTER_ORDER ** -0.5),
        "hy_f_b2": nrm(ks[11], (Dp, FILTER_ORDER), 0.1),
        "hy_f_w3": nrm(ks[12], (Dp, FILTER_ORDER, FILTER_ORDER), FILTER_ORDER ** -0.5),
        "hy_f_b3": nrm(ks[13], (Dp, FILTER_ORDER), 0.1),
        "hy_f_freq": gain(ks[14], (Dp, FILTER_ORDER)),
        "hy_f_wout": nrm(ks[15], (Dp, FILTER_ORDER, 2 * D_B), FILTER_ORDER ** -0.5),
        "hy_d_skip": nrm(ks[16], (Dp, D_B), 1.0),
        "outnorm_a_g": gain(ks[17], (Dp, D_A)),
        "outnorm_b_g": gain(ks[18], (Dp, D_B)),
        "w_out": nrm(ks[19], (Dp, D_MIX, D_MODEL), D_MIX ** -0.5),
        "norm2_g": gain(ks[20], (Dp, D_MODEL)),
        "ffn_w_up": nrm(ks[21], (Dp, D_MODEL, 2 * D_FF), D_MODEL ** -0.5),
        "ffn_dw_w": nrm(ks[22], (Dp, CONV_W, D_FF), CONV_W ** -0.5),
        "ffn_dw_b": nrm(ks[23], (Dp, D_FF), 0.02),
        "ffn_w_down": nrm(ks[24], (Dp, D_FF, D_MODEL), D_FF ** -0.5),
        "final_g": gain(ks[25], (D_MODEL,)),
    }


def reference(x, norm1_g, w_in, sgu_norm_g, sgu_w, sgu_b, hy_conv_w, hy_conv_b,
              hy_f_w1, hy_f_b1, hy_f_w2, hy_f_b2, hy_f_w3, hy_f_b3, hy_f_freq,
              hy_f_wout, hy_d_skip, outnorm_a_g, outnorm_b_g, w_out, norm2_g,
              ffn_w_up, ffn_dw_w, ffn_dw_b, ffn_w_down, final_g):
    B, L, _ = x.shape
    n_chunks = L // CHUNK
    for l in range(DEPTH):
        h = rms_norm(x, norm1_g[l])
        p = h @ w_in[l]
        a_u = p[..., :D_A]
        a_v = p[..., D_A:2 * D_A]
        b_in = p[..., 2 * D_A:]

        zu = jax.nn.gelu(a_u, approximate=False)
        zv = rms_norm(jax.nn.gelu(a_v, approximate=False), sgu_norm_g[l])
        zv = zv.reshape(B, n_chunks, CHUNK, N_HEADS_A, HD_A)
        s = jnp.einsum('hij,bcjhd->bcihd', sgu_w[l], zv)
        s = s + sgu_b[l].T[None, None, :, :, None]
        y_a = zu * s.reshape(B, L, D_A)

        b_c = dwconv3(b_in, hy_conv_w[l], hy_conv_b[l])
        bx0 = b_c[..., :D_B]
        bx1 = b_c[..., D_B:2 * D_B]
        bv = b_c[..., 2 * D_B:]
        k = hyena_filter(L, hy_f_w1[l], hy_f_b1[l], hy_f_w2[l], hy_f_b2[l],
                         hy_f_w3[l], hy_f_b3[l], hy_f_freq[l], hy_f_wout[l])
        u = bv * bx1
        yc = bidir_long_conv(u, k).astype(x.dtype) + u * hy_d_skip[l]
        y_b = bx0 * yc

        merged = jnp.concatenate([rms_norm(y_a, outnorm_a_g[l]),
                                  rms_norm(y_b, outnorm_b_g[l])], axis=-1)
        x = x + merged @ w_out[l]

        h = rms_norm(x, norm2_g[l])
        gv = h @ ffn_w_up[l]
        g = dwconv3(gv[..., :D_FF], ffn_dw_w[l], ffn_dw_b[l])
        val = gv[..., D_FF:]
        x = x + (jax.nn.gelu(g, approximate=False) * val) @ ffn_w_down[l]
    return rms_norm(x, final_g)
```

```python
import functools
import math

import numpy as np
import jax
import jax.numpy as jnp
from jax import lax
from jax.experimental import pallas as pl
from jax.experimental.pallas import tpu as pltpu

EPS = 1e-6
CHUNK = 128
N_HEADS_A = 8
DECAY_TARGET = 1e-2
FAST_DECAY_PCT = 0.3
SLOW_DECAY_PCT = 1.5
N_BANDS = 16

DFT_R = 128
DFT_G = 16
HALO = 16
VMEM_LIMIT = 56 * 1024 * 1024

F32 = jnp.float32
BF16 = jnp.bfloat16


def _gelu(x):
    return 0.5 * x * (1.0 + lax.erf(x * np.float32(math.sqrt(0.5))))


def _rms(x, g):
    return x * lax.rsqrt(jnp.mean(x * x, axis=-1, keepdims=True) + EPS) * g


def _params(*sem):
    return pltpu.CompilerParams(dimension_semantics=sem, vmem_limit_bytes=VMEM_LIMIT)


def _in_proj_a_kernel(x_ref, g1_ref, w_ref, gs_ref, ws_ref, bias_ref, go_ref, o_ref, ya_ref):
    tm = x_ref.shape[0]
    d_a = o_ref.shape[1]
    hd = d_a // N_HEADS_A
    h = _rms(x_ref[...], g1_ref[...]).astype(BF16)
    p = jnp.dot(h, w_ref[...], preferred_element_type=F32)
    gv = _gelu(p[:, d_a:])
    zv = _rms(gv, gs_ref[...]).astype(BF16)
    for c in range(tm // CHUNK):
        rows = slice(c * CHUNK, (c + 1) * CHUNK)
        for hh in range(N_HEADS_A):
            cols = slice(hh * hd, (hh + 1) * hd)
            s = jnp.dot(ws_ref[hh], zv[rows, cols], preferred_element_type=F32)
            ya_ref[rows, cols] = _gelu(p[rows, cols]) * (s + bias_ref[:, cols])
    o_ref[...] = _rms(ya_ref[...], go_ref[...]).astype(o_ref.dtype)


def _in_proj_a(x2d, g1, w_a, gs, ws, bias_full, go, *, tm):
    m, d = x2d.shape
    d_a = w_a.shape[1] // 2
    return pl.pallas_call(
        _in_proj_a_kernel,
        grid=(m // tm,),
        in_specs=[
            pl.BlockSpec((tm, d), lambda i: (i, 0)),
            pl.BlockSpec((1, d), lambda i: (0, 0)),
            pl.BlockSpec((d, 2 * d_a), lambda i: (0, 0)),
            pl.BlockSpec((1, d_a), lambda i: (0, 0)),
            pl.BlockSpec((N_HEADS_A, CHUNK, CHUNK), lambda i: (0, 0, 0)),
            pl.BlockSpec((CHUNK, d_a), lambda i: (0, 0)),
            pl.BlockSpec((1, d_a), lambda i: (0, 0)),
        ],
        out_specs=pl.BlockSpec((tm, d_a), lambda i: (i, 0)),
        out_shape=jax.ShapeDtypeStruct((m, d_a), BF16),
        scratch_shapes=[pltpu.VMEM((tm, d_a), F32)],
        compiler_params=_params("parallel"),
        name="in_proj_a",
    )(x2d, g1, w_a, gs, ws, bias_full, go)


def _fill_ext(hext_ref, prev, main, nxt, i, seq_tiles):
    tm = main.shape[0]
    has_prev = (i % seq_tiles) != 0
    has_next = ((i + 1) % seq_tiles) != 0
    hext_ref[0:HALO, :] = jnp.where(has_prev, prev, jnp.zeros_like(prev))
    hext_ref[HALO:HALO + tm, :] = main
    hext_ref[HALO + tm:, :] = jnp.where(has_next, nxt, jnp.zeros_like(nxt))


def _conv3(gext, w_ref, b_ref, tm):
    n = gext.shape[0]
    prev = pltpu.roll(gext, 1, 0)[HALO:HALO + tm]
    nxt = pltpu.roll(gext, n - 1, 0)[HALO:HALO + tm]
    cur = gext[HALO:HALO + tm]
    return prev * w_ref[0:1, :] + cur * w_ref[1:2, :] + nxt * w_ref[2:3, :] + b_ref[...]


def _in_proj_b_kernel(xp_ref, x_ref, xn_ref, g1_ref, w0_ref, w1_ref, w2_ref,
                      cw0_ref, cw1_ref, cw2_ref, cb0_ref, cb1_ref, cb2_ref,
                      x0_ref, u_ref, hext_ref, *, seq_tiles):
    tm = x_ref.shape[0]
    i = pl.program_id(0)

    @pl.when(pl.program_id(1) == 0)
    def _():
        g1 = g1_ref[...]
        _fill_ext(hext_ref,
                  _rms(xp_ref[...], g1).astype(BF16),
                  _rms(x_ref[...], g1).astype(BF16),
                  _rms(xn_ref[...], g1).astype(BF16), i, seq_tiles)

    hext = hext_ref[...]

    def branch(w_ref, cw_ref, cb_ref):
        gext = jnp.dot(hext, w_ref[...], preferred_element_type=F32)
        return _conv3(gext, cw_ref, cb_ref, tm)

    x0_ref[...] = branch(w0_ref, cw0_ref, cb0_ref)
    x1 = branch(w1_ref, cw1_ref, cb1_ref)
    v = branch(w2_ref, cw2_ref, cb2_ref)
    u_ref[...] = v * x1


def _halo_specs(tm, d, m):
    blocks_per_tile = tm // HALO
    last = m // HALO - 1
    return [
        pl.BlockSpec((HALO, d), lambda i, j: (jnp.maximum(i * blocks_per_tile - 1, 0), 0)),
        pl.BlockSpec((tm, d), lambda i, j: (i, 0)),
        pl.BlockSpec((HALO, d), lambda i, j: (jnp.minimum((i + 1) * blocks_per_tile, last), 0)),
    ]


def _in_proj_b(x2d, g1, w_in, conv_w, conv_b, *, col0, d_b, seq_len, tm, tn):
    m, d = x2d.shape
    nj = d_b // tn
    w_specs = [pl.BlockSpec((d, tn), functools.partial(
        lambda i, j, off: (0, off + j), off=(col0 + g * d_b) // tn)) for g in range(3)]
    cw_specs = [pl.BlockSpec((3, tn), functools.partial(
        lambda i, j, off: (0, off + j), off=(g * d_b) // tn)) for g in range(3)]
    cb_specs = [pl.BlockSpec((1, tn), functools.partial(
        lambda i, j, off: (0, off + j), off=(g * d_b) // tn)) for g in range(3)]
    out_spec = pl.BlockSpec((tm, tn), lambda i, j: (i, j))
    return pl.pallas_call(
        functools.partial(_in_proj_b_kernel, seq_tiles=seq_len // tm),
        grid=(m // tm, nj),
        in_specs=_halo_specs(tm, d, m) + [pl.BlockSpec((1, d), lambda i, j: (0, 0))]
        + w_specs + cw_specs + cb_specs,
        out_specs=[out_spec, out_spec],
        out_shape=[jax.ShapeDtypeStruct((m, d_b), F32)] * 2,
        scratch_shapes=[pltpu.VMEM((tm + 2 * HALO, d), BF16)],
        compiler_params=_params("parallel", "arbitrary"),
        name="in_proj_b",
    )(x2d, x2d, x2d, g1, w_in, w_in, w_in, conv_w, conv_w, conv_w, conv_b, conv_b, conv_b)


def _filter_mlp_kernel(z_ref, w1_ref, b1_ref, w2_ref, b2_ref, w3_ref, b3_ref, fr_ref,
                       wo_ref, dl_ref, hw_ref, l1_ref):
    hi = lax.Precision.HIGHEST
    z = z_ref[...]
    fr = fr_ref[...]
    h = jnp.sin(fr * (jnp.dot(z, w1_ref[...], precision=hi, preferred_element_type=F32) + b1_ref[...]))
    h = jnp.sin(fr * (jnp.dot(h, w2_ref[...], precision=hi, preferred_element_type=F32) + b2_ref[...]))
    h = jnp.sin(fr * (jnp.dot(h, w3_ref[...], precision=hi, preferred_element_type=F32) + b3_ref[...]))
    h = jnp.dot(h, wo_ref[...], precision=hi, preferred_element_type=F32)
    hw = h * jnp.exp(-z[:, 0:1] * dl_ref[...])
    tl, c2 = hw.shape
    row = lax.broadcasted_iota(jnp.int32, (tl, c2), 0) + pl.program_id(0) * tl
    col = lax.broadcasted_iota(jnp.int32, (tl, c2), 1)
    hw = jnp.where((row == 0) & (col >= c2 // 2), 0.0, hw)
    hw_ref[...] = hw

    @pl.when(pl.program_id(0) == 0)
    def _():
        l1_ref[...] = jnp.zeros_like(l1_ref)

    l1_ref[...] += jnp.sum(jnp.abs(hw), axis=0, keepdims=True)


def _filter_mlp(z, w1, b1, w2, b2, w3, b3, fr, wout, deltas2, *, tl):
    seq_len, e = z.shape
    o = w2.shape[0]
    c2 = wout.shape[1]
    full = lambda shape: pl.BlockSpec(shape, lambda i: (0, 0))
    return pl.pallas_call(
        _filter_mlp_kernel,
        grid=(seq_len // tl,),
        in_specs=[pl.BlockSpec((tl, e), lambda i: (i, 0)), full((e, o)), full((1, o)),
                  full((o, o)), full((1, o)), full((o, o)), full((1, o)), full((1, o)),
                  full((o, c2)), full((1, c2))],
        out_specs=[pl.BlockSpec((tl, c2), lambda i: (i, 0)), full((1, c2))],
        out_shape=[jax.ShapeDtypeStruct((seq_len, c2), F32), jax.ShapeDtypeStruct((1, c2), F32)],
        compiler_params=_params("arbitrary"),
        name="filter_mlp",
    )(z, w1, b1, w2, b2, w3, b3, fr, wout, deltas2)


def _dft1_kernel(w_ref, x_ref, o_ref, a_ref):
    g = x_ref.shape[2]
    xs = jnp.swapaxes(x_ref[0], 0, 1).astype(BF16)
    w = w_ref[...]
    for j in range(g):
        a_ref[j] = jnp.dot(w, xs[j], preferred_element_type=F32)
    o_ref[0] = jnp.swapaxes(a_ref[...], 0, 1).astype(o_ref.dtype)


def _dft1(w, x4, *, g, ct, name):
    nb, hq, r, c = x4.shape
    mo = w.shape[0]
    return pl.pallas_call(
        _dft1_kernel,
        grid=(nb, r // g, c // ct),
        in_specs=[pl.BlockSpec((mo, hq), lambda b, i, j: (0, 0)),
                  pl.BlockSpec((1, hq, g, ct), lambda b, i, j: (b, 0, i, j))],
        out_specs=pl.BlockSpec((1, mo, g, ct), lambda b, i, j: (b, 0, i, j)),
        out_shape=jax.ShapeDtypeStruct((nb, mo, r, c), BF16),
        scratch_shapes=[pltpu.VMEM((g, mo, ct), F32)],
        compiler_params=_params("parallel", "parallel", "parallel"),
        name=name,
    )(w, x4)


def _filter_spec_kernel(af_ref, ab_ref, mf_ref, l1f_ref, l1b_ref, k_ref):
    kb = mf_ref.shape[0]
    r = mf_ref.shape[1] // 2
    ct = k_ref.shape[2]
    scale = 1.0 / (l1f_ref[...] + l1b_ref[...] + EPS)

    def body(kk, carry):
        mfk = mf_ref[kk]
        xf = jnp.dot(mfk, af_ref[0, :, kk].reshape(2 * r, ct), preferred_element_type=F32)
        xb = jnp.dot(mfk, ab_ref[0, :, kk].reshape(2 * r, ct), preferred_element_type=F32)
        k_ref[kk, 0:r, :] = (xf[:r] + xb[:r]) * scale
        k_ref[kk, r:, :] = (xf[r:] - xb[r:]) * scale
        return carry

    lax.fori_loop(0, kb, body, 0)


def _filter_spec(ak5, mf, l1, *, kb, ct):
    _, _, hq, r, c2 = ak5.shape
    c = c2 // 2
    nct = c // ct
    return pl.pallas_call(
        _filter_spec_kernel,
        grid=(hq // kb, nct),
        in_specs=[pl.BlockSpec((1, 2, kb, r, ct), lambda i, j: (0, 0, i, 0, j)),
                  pl.BlockSpec((1, 2, kb, r, ct), lambda i, j: (0, 0, i, 0, j + nct)),
                  pl.BlockSpec((kb, 2 * r, 2 * r), lambda i, j: (i, 0, 0)),
                  pl.BlockSpec((1, ct), lambda i, j: (0, j)),
                  pl.BlockSpec((1, ct), lambda i, j: (0, j + nct))],
        out_specs=pl.BlockSpec((kb, 2 * r, ct), lambda i, j: (i, 0, j)),
        out_shape=jax.ShapeDtypeStruct((hq, 2 * r, c), F32),
        compiler_params=_params("parallel", "parallel"),
        name="filter_spec",
    )(ak5, ak5, mf, l1, l1)


def _spec_conv_kernel(a_ref, mf_ref, mi_ref, k_ref, dr_ref, di_ref, d_scr):
    kb = mf_ref.shape[0]
    r = mf_ref.shape[1] // 2
    ct = k_ref.shape[2]

    def body(kk, carry):
        a = a_ref[0, :, kk].reshape(2 * r, ct)
        x = jnp.dot(mf_ref[kk], a, preferred_element_type=F32)
        xr, xi = x[:r], x[r:]
        kr, ki = k_ref[kk, 0:r, :], k_ref[kk, r:, :]
        y = jnp.concatenate([xr * kr - xi * ki, xr * ki + xi * kr], axis=0).astype(BF16)
        dd = jnp.dot(mi_ref[kk], y, preferred_element_type=F32)
        d_scr[0, kk] = dd[:r]
        d_scr[1, kk] = dd[r:]
        return carry

    lax.fori_loop(0, kb, body, 0)
    dr_ref[0] = jnp.swapaxes(d_scr[0], 0, 1).astype(dr_ref.dtype)
    di_ref[0] = jnp.swapaxes(d_scr[1], 0, 1).astype(di_ref.dtype)


def _spec_conv(a5, mf, mi, khat, *, kb, ct):
    nb, _, hq, r, c = a5.shape
    mspec = pl.BlockSpec((kb, 2 * r, 2 * r), lambda i, j, b: (i, 0, 0))
    ospec = pl.BlockSpec((1, r, kb, ct), lambda i, j, b: (b, 0, i, j))
    oshape = jax.ShapeDtypeStruct((nb, r, hq, c), BF16)
    return pl.pallas_call(
        _spec_conv_kernel,
        grid=(hq // kb, c // ct, nb),
        in_specs=[pl.BlockSpec((1, 2, kb, r, ct), lambda i, j, b: (b, 0, i, 0, j)), mspec, mspec,
                  pl.BlockSpec((kb, 2 * r, ct), lambda i, j, b: (i, 0, j))],
        out_specs=[ospec, ospec],
        out_shape=[oshape, oshape],
        scratch_shapes=[pltpu.VMEM((2, kb, r, ct), F32)],
        compiler_params=_params("parallel", "parallel", "arbitrary"),
        name="spec_conv",
    )(a5, mf, mi, khat)


def _conv_out_kernel(wr_ref, wi_ref, dr_ref, di_ref, u_ref, x0_ref, ds_ref, o_ref, y_scr):
    g = dr_ref.shape[1]
    wr, wi = wr_ref[...], wi_ref[...]
    for j in range(g):
        y_scr[j] = (jnp.dot(wr, dr_ref[0, j], preferred_element_type=F32)
                    + jnp.dot(wi, di_ref[0, j], preferred_element_type=F32))
    yconv = jnp.swapaxes(y_scr[...], 0, 1)
    o_ref[0] = x0_ref[0] * (yconv + u_ref[0] * ds_ref[...])


def _conv_out(wr, wi, dr, di, u4, x04, ds, *, g, ct):
    nb, r, hq, c = dr.shape
    dspec = pl.BlockSpec((1, g, hq, ct), lambda b, i, j: (b, i, 0, j))
    nspec = pl.BlockSpec((1, hq, g, ct), lambda b, i, j: (b, 0, i, j))
    wspec = pl.BlockSpec((hq, hq), lambda b, i, j: (0, 0))
    return pl.pallas_call(
        _conv_out_kernel,
        grid=(nb, r // g, c // ct),
        in_specs=[wspec, wspec, dspec, dspec, nspec, nspec,
                  pl.BlockSpec((1, ct), lambda b, i, j: (0, j))],
        out_specs=nspec,
        out_shape=jax.ShapeDtypeStruct((nb, hq, r, c), F32),
        scratch_shapes=[pltpu.VMEM((g, hq, ct), F32)],
        compiler_params=_params("parallel", "parallel", "parallel"),
        name="conv_out",
    )(wr, wi, dr, di, u4, x04, ds)


def _out_proj_kernel(ma_ref, yb_ref, x_ref, gb_ref, wa_ref, wb_ref, g2_ref, x1_ref, h2_ref):
    mb = _rms(yb_ref[...], gb_ref[...]).astype(BF16)
    x1 = (x_ref[...]
          + jnp.dot(ma_ref[...], wa_ref[...], preferred_element_type=F32)
          + jnp.dot(mb, wb_ref[...], preferred_element_type=F32))
    x1_ref[...] = x1
    h2_ref[...] = _rms(x1, g2_ref[...]).astype(h2_ref.dtype)


def _out_proj(ma, yb, x2d, gb, w_out, g2, *, tm):
    m, d = x2d.shape
    d_a = ma.shape[1]
    d_b = yb.shape[1]
    row = lambda w: pl.BlockSpec((tm, w), lambda i: (i, 0))
    return pl.pallas_call(
        _out_proj_kernel,
        grid=(m // tm,),
        in_specs=[row(d_a), row(d_b), row(d), pl.BlockSpec((1, d_b), lambda i: (0, 0)),
                  pl.BlockSpec((d_a, d), lambda i: (0, 0)),
                  pl.BlockSpec((d_b, d), lambda i: (d_a // d_b, 0)),
                  pl.BlockSpec((1, d), lambda i: (0, 0))],
        out_specs=[row(d), row(d)],
        out_shape=[jax.ShapeDtypeStruct((m, d), F32), jax.ShapeDtypeStruct((m, d), BF16)],
        compiler_params=_params("parallel"),
        name="out_proj",
    )(ma, yb, x2d, gb, w_out, w_out, g2)


def _ffn_kernel(hp_ref, h_ref, hn_ref, x1_ref, wg_ref, wv_ref, cw_ref, cb_ref, wd_ref, gf_ref,
                o_ref, hext_ref, *, seq_tiles, final_norm):
    tm = h_ref.shape[0]
    i = pl.program_id(0)
    f = pl.program_id(1)

    @pl.when(f == 0)
    def _():
        _fill_ext(hext_ref, hp_ref[...], h_ref[...], hn_ref[...], i, seq_tiles)
        o_ref[...] = x1_ref[...]

    gext = jnp.dot(hext_ref[...], wg_ref[...], preferred_element_type=F32)
    g = _conv3(gext, cw_ref, cb_ref, tm)
    val = jnp.dot(h_ref[...], wv_ref[...], preferred_element_type=F32)
    act = (_gelu(g) * val).astype(BF16)
    o_ref[...] += jnp.dot(act, wd_ref[...], preferred_element_type=F32)

    if final_norm:
        @pl.when(f == pl.num_programs(1) - 1)
        def _():
            o_ref[...] = _rms(o_ref[...], gf_ref[...])


def _ffn(h2, x1, wg, wv, cw, cb, wd, gf, *, seq_len, tm, tf, final_norm):
    m, d = x1.shape
    ffp = wg.shape[1]
    return pl.pallas_call(
        functools.partial(_ffn_kernel, seq_tiles=seq_len // tm, final_norm=final_norm),
        grid=(m // tm, ffp // tf),
        in_specs=_halo_specs(tm, d, m) + [
            pl.BlockSpec((tm, d), lambda i, j: (i, 0)),
            pl.BlockSpec((d, tf), lambda i, j: (0, j)),
            pl.BlockSpec((d, tf), lambda i, j: (0, j)),
            pl.BlockSpec((3, tf), lambda i, j: (0, j)),
            pl.BlockSpec((1, tf), lambda i, j: (0, j)),
            pl.BlockSpec((tf, d), lambda i, j: (j, 0)),
            pl.BlockSpec((1, d), lambda i, j: (0, 0)),
        ],
        out_specs=pl.BlockSpec((tm, d), lambda i, j: (i, 0)),
        out_shape=jax.ShapeDtypeStruct((m, d), F32),
        scratch_shapes=[pltpu.VMEM((tm + 2 * HALO, d), BF16)],
        compiler_params=_params("parallel", "arbitrary"),
        name="ffn",
    )(h2, h2, h2, x1, wg, wv, cw, cb, wd, gf)


@functools.lru_cache(maxsize=None)
def _dft_constants(seq_len):
    n = 2 * seq_len
    r = DFT_R
    q = n // r
    hq = q // 2
    k1 = np.arange(hq, dtype=np.float64)[:, None]
    n1 = np.arange(hq, dtype=np.float64)[None, :]
    th1 = 2.0 * np.pi * n1 * (k1 + 0.5) / q
    f1r, f1i = np.cos(th1), -np.sin(th1)
    f1s = np.concatenate([f1r, f1i], axis=0)
    f1rt, f1it = (2.0 / n) * f1r.T, (2.0 / n) * f1i.T
    k2 = np.arange(r, dtype=np.float64)[None, :, None]
    n2 = np.arange(r, dtype=np.float64)[None, None, :]
    kk = np.arange(hq, dtype=np.float64)[:, None, None]
    th2 = 2.0 * np.pi * (n2 * k2 / r + n2 * (kk + 0.5) / n)
    gr, gi = np.cos(th2), -np.sin(th2)
    mf = np.concatenate([np.concatenate([gr, -gi], axis=2),
                         np.concatenate([gi, gr], axis=2)], axis=1)
    grt, git = gr.transpose(0, 2, 1), gi.transpose(0, 2, 1)
    mi = np.concatenate([np.concatenate([grt, git], axis=2),
                         np.concatenate([-git, grt], axis=2)], axis=1)
    return f1s, f1rt, f1it, mf, mi


def _positional_features(seq_len, e_pad):
    t = jnp.linspace(0.0, 1.0, seq_len, dtype=F32)[:, None]
    w = (2.0 * math.pi / seq_len) * jnp.arange(seq_len, dtype=F32)[:, None]
    f = jnp.linspace(1e-4, N_BANDS - 1, N_BANDS, dtype=F32)[None]
    z = jnp.concatenate([t, jnp.cos(f * w), -jnp.sin(f * w)], axis=-1)
    return jnp.pad(z, ((0, 0), (0, e_pad - z.shape[1])))


def kernel(x, norm1_g, w_in, sgu_norm_g, sgu_w, sgu_b, hy_conv_w, hy_conv_b, hy_f_w1, hy_f_b1,
           hy_f_w2, hy_f_b2, hy_f_w3, hy_f_b3, hy_f_freq, hy_f_wout, hy_d_skip, outnorm_a_g,
           outnorm_b_g, w_out, norm2_g, ffn_w_up, ffn_dw_w, ffn_dw_b, ffn_w_down, final_g):
    nb, seq_len, d = x.shape
    depth = w_in.shape[0]
    d_a = sgu_norm_g.shape[1]
    d_b = hy_d_skip.shape[1]
    d_ff = ffn_dw_b.shape[1]
    m = nb * seq_len
    r = DFT_R
    hq = seq_len // r
    g = DFT_G
    kb = min(DFT_G, hq)
    tm = 512
    tf = 512
    d_ffp = -(-d_ff // tf) * tf

    f1s, f1rt, f1it, mf, mi = (jnp.asarray(a, BF16) for a in _dft_constants(seq_len))
    e_pad = hy_f_w2.shape[1]
    z = _positional_features(seq_len, e_pad)
    max_decay = math.log(DECAY_TARGET) / FAST_DECAY_PCT
    min_decay = math.log(DECAY_TARGET) / SLOW_DECAY_PCT
    deltas = jnp.abs(jnp.linspace(min_decay, max_decay, d_b, dtype=F32))
    deltas2 = jnp.concatenate([deltas, deltas])[None]

    x2d = x.reshape(m, d)
    row = lambda v: v.reshape(1, -1)
    for l in range(depth):
        w_in_l = w_in[l].astype(BF16)
        g1 = row(norm1_g[l])

        bias_full = jnp.repeat(sgu_b[l].T, d_a // N_HEADS_A, axis=1)
        ma = _in_proj_a(x2d, g1, w_in_l[:, :2 * d_a], row(sgu_norm_g[l]), sgu_w[l].astype(BF16),
                        bias_full, row(outnorm_a_g[l]), tm=256)

        x0, u = _in_proj_b(x2d, g1, w_in_l, hy_conv_w[l], row(hy_conv_b[l]),
                           col0=2 * d_a, d_b=d_b, seq_len=seq_len, tm=tm, tn=512)
        u4 = u.reshape(nb, hq, r, d_b)
        x04 = x0.reshape(nb, hq, r, d_b)

        w1p = jnp.pad(hy_f_w1[l], ((0, e_pad - hy_f_w1.shape[1]), (0, 0)))
        hw, l1 = _filter_mlp(z, w1p, row(hy_f_b1[l]), hy_f_w2[l], row(hy_f_b2[l]), hy_f_w3[l],
                             row(hy_f_b3[l]), row(hy_f_freq[l]), hy_f_wout[l], deltas2, tl=512)
        ak = _dft1(f1s, hw.reshape(1, hq, r, 2 * d_b), g=g, ct=512, name="filter_dft1")
        khat = _filter_spec(ak.reshape(1, 2, hq, r, 2 * d_b), mf, l1, kb=kb, ct=256)

        a = _dft1(f1s, u4, g=g, ct=512, name="u_dft1")
        dr, di = _spec_conv(a.reshape(nb, 2, hq, r, d_b), mf, mi, khat, kb=kb, ct=256)
        yb = _conv_out(f1rt, f1it, dr, di, u4, x04, row(hy_d_skip[l]), g=g, ct=512)

        x1, h2 = _out_proj(ma, yb.reshape(m, d_b), x2d, row(outnorm_b_g[l]),
                           w_out[l].astype(BF16), row(norm2_g[l]), tm=tm)

        pad_c = ((0, 0), (0, d_ffp - d_ff))
        wg = jnp.pad(ffn_w_up[l][:, :d_ff], pad_c).astype(BF16)
        wv = jnp.pad(ffn_w_up[l][:, d_ff:], pad_c).astype(BF16)
        wd = jnp.pad(ffn_w_down[l], ((0, d_ffp - d_ff), (0, 0))).astype(BF16)
        x2d = _ffn(h2, x1, wg, wv, jnp.pad(ffn_dw_w[l], pad_c), jnp.pad(row(ffn_dw_b[l]), pad_c),
                   wd, row(final_g), seq_len=seq_len, tm=tm, tf=tf, final_norm=(l == depth - 1))
    return x2d.reshape(nb, seq_len, d)
```

```python
import functools
import math

import numpy as np
import jax
import jax.numpy as jnp
from jax import lax
from jax.experimental import pallas as pl
from jax.experimental.pallas import tpu as pltpu

EPS = 1e-6
CHUNK = 128
N_HEADS_A = 8
DECAY_TARGET = 1e-2
FAST_DECAY_PCT = 0.3
SLOW_DECAY_PCT = 1.5
N_BANDS = 16

LANE = 128
DFT_R = 128
DFT_G = 16
HALO = 16
VMEM_LIMIT = 56 * 1024 * 1024

F32 = jnp.float32
BF16 = jnp.bfloat16


def _gelu(x):
    return 0.5 * x * (1.0 + lax.erf(x * np.float32(math.sqrt(0.5))))


def _rms(x, g):
    return x * lax.rsqrt(jnp.mean(x * x, axis=-1, keepdims=True) + EPS) * g


def _params(*sem):
    return pltpu.CompilerParams(dimension_semantics=sem, vmem_limit_bytes=VMEM_LIMIT)


def _in_proj_a_kernel(x_ref, g1_ref, w_ref, gs_ref, ws_ref, bias_ref, go_ref, o_ref, ya_ref):
    tm = x_ref.shape[0]
    d_a = o_ref.shape[1]
    hd = d_a // N_HEADS_A
    h = _rms(x_ref[...], g1_ref[...]).astype(BF16)
    p = jnp.dot(h, w_ref[...], preferred_element_type=F32)
    gv = _gelu(p[:, d_a:])
    zv = _rms(gv, gs_ref[...]).astype(BF16)
    for c in range(tm // CHUNK):
        rows = slice(c * CHUNK, (c + 1) * CHUNK)
        for hh in range(N_HEADS_A):
            cols = slice(hh * hd, (hh + 1) * hd)
            s = jnp.dot(ws_ref[hh], zv[rows, cols], preferred_element_type=F32)
            ya_ref[rows, cols] = _gelu(p[rows, cols]) * (s + bias_ref[:, cols])
    o_ref[...] = _rms(ya_ref[...], go_ref[...]).astype(o_ref.dtype)


def _in_proj_a(x2d, g1, w_a, gs, ws, bias_full, go, *, tm):
    m, d = x2d.shape
    d_a = gs.shape[1]
    return pl.pallas_call(
        _in_proj_a_kernel,
        grid=(m // tm,),
        in_specs=[
            pl.BlockSpec((tm, d), lambda i: (i, 0)),
            pl.BlockSpec((1, d), lambda i: (0, 0)),
            pl.BlockSpec((d, 2 * d_a), lambda i: (0, 0)),
            pl.BlockSpec((1, d_a), lambda i: (0, 0)),
            pl.BlockSpec((N_HEADS_A, CHUNK, CHUNK), lambda i: (0, 0, 0)),
            pl.BlockSpec((CHUNK, d_a), lambda i: (0, 0)),
            pl.BlockSpec((1, d_a), lambda i: (0, 0)),
        ],
        out_specs=pl.BlockSpec((tm, d_a), lambda i: (i, 0)),
        out_shape=jax.ShapeDtypeStruct((m, d_a), BF16),
        scratch_shapes=[pltpu.VMEM((tm, d_a), F32)],
        compiler_params=_params("parallel"),
        name="in_proj_a",
    )(x2d, g1, w_a, gs, ws, bias_full, go)


def _fill_ext(hext_ref, prev, main, nxt, i, seq_tiles):
    tm = main.shape[0]
    has_prev = (i % seq_tiles) != 0
    has_next = ((i + 1) % seq_tiles) != 0
    hext_ref[0:HALO, :] = jnp.where(has_prev, prev, jnp.zeros_like(prev))
    hext_ref[HALO:HALO + tm, :] = main
    hext_ref[HALO + tm:, :] = jnp.where(has_next, nxt, jnp.zeros_like(nxt))


def _conv3(gext, w_ref, b_ref, tm):
    n = gext.shape[0]
    prev = pltpu.roll(gext, 1, 0)[HALO:HALO + tm]
    nxt = pltpu.roll(gext, n - 1, 0)[HALO:HALO + tm]
    cur = gext[HALO:HALO + tm]
    return prev * w_ref[0:1, :] + cur * w_ref[1:2, :] + nxt * w_ref[2:3, :] + b_ref[...]


def _in_proj_b_kernel(xp_ref, x_ref, xn_ref, g1_ref, w0_ref, w1_ref, w2_ref,
                      cw0_ref, cw1_ref, cw2_ref, cb0_ref, cb1_ref, cb2_ref,
                      x0_ref, u_ref, hext_ref, *, seq_tiles):
    tm = x_ref.shape[0]
    i = pl.program_id(0)

    @pl.when(pl.program_id(1) == 0)
    def _():
        g1 = g1_ref[...]
        _fill_ext(hext_ref,
                  _rms(xp_ref[...], g1).astype(BF16),
                  _rms(x_ref[...], g1).astype(BF16),
                  _rms(xn_ref[...], g1).astype(BF16), i, seq_tiles)

    hext = hext_ref[...]

    def branch(w_ref, cw_ref, cb_ref):
        gext = jnp.dot(hext, w_ref[...], preferred_element_type=F32)
        return _conv3(gext, cw_ref, cb_ref, tm)

    x0_ref[...] = branch(w0_ref, cw0_ref, cb0_ref)
    x1 = branch(w1_ref, cw1_ref, cb1_ref)
    v = branch(w2_ref, cw2_ref, cb2_ref)
    u_ref[...] = v * x1


def _halo_specs(tm, d, m):
    blocks_per_tile = tm // HALO
    last = m // HALO - 1
    return [
        pl.BlockSpec((HALO, d), lambda i, j: (jnp.maximum(i * blocks_per_tile - 1, 0), 0)),
        pl.BlockSpec((tm, d), lambda i, j: (i, 0)),
        pl.BlockSpec((HALO, d), lambda i, j: (jnp.minimum((i + 1) * blocks_per_tile, last), 0)),
    ]


def _in_proj_b(x2d, g1, w_in, conv_w, conv_b, *, col0, d_b, seq_len, tm, tn):
    m, d = x2d.shape
    nj = d_b // tn
    w_specs = [pl.BlockSpec((d, tn), functools.partial(
        lambda i, j, off: (0, off + j), off=(col0 + g * d_b) // tn)) for g in range(3)]
    cw_specs = [pl.BlockSpec((3, tn), functools.partial(
        lambda i, j, off: (0, off + j), off=(g * d_b) // tn)) for g in range(3)]
    cb_specs = [pl.BlockSpec((1, tn), functools.partial(
        lambda i, j, off: (0, off + j), off=(g * d_b) // tn)) for g in range(3)]
    out_spec = pl.BlockSpec((tm, tn), lambda i, j: (i, j))
    return pl.pallas_call(
        functools.partial(_in_proj_b_kernel, seq_tiles=seq_len // tm),
        grid=(m // tm, nj),
        in_specs=_halo_specs(tm, d, m) + [pl.BlockSpec((1, d), lambda i, j: (0, 0))]
        + w_specs + cw_specs + cb_specs,
        out_specs=[out_spec, out_spec],
        out_shape=[jax.ShapeDtypeStruct((m, d_b), F32)] * 2,
        scratch_shapes=[pltpu.VMEM((tm + 2 * HALO, d), BF16)],
        compiler_params=_params("parallel", "arbitrary"),
        name="in_proj_b",
    )(x2d, x2d, x2d, g1, w_in, w_in, w_in, conv_w, conv_w, conv_w, conv_b, conv_b, conv_b)


def _filter_mlp_kernel(z_ref, w1_ref, b1_ref, w2_ref, b2_ref, w3_ref, b3_ref, fr_ref,
                       wo_ref, dl_ref, hw_ref, l1_ref):
    hi = lax.Precision.HIGHEST
    z = z_ref[...]
    fr = fr_ref[...]
    h = jnp.sin(fr * (jnp.dot(z, w1_ref[...], precision=hi, preferred_element_type=F32) + b1_ref[...]))
    h = jnp.sin(fr * (jnp.dot(h, w2_ref[...], precision=hi, preferred_element_type=F32) + b2_ref[...]))
    h = jnp.sin(fr * (jnp.dot(h, w3_ref[...], precision=hi, preferred_element_type=F32) + b3_ref[...]))
    h = jnp.dot(h, wo_ref[...], precision=hi, preferred_element_type=F32)
    hw = h * jnp.exp(-z[:, 0:1] * dl_ref[...])
    tl, c2 = hw.shape
    row = lax.broadcasted_iota(jnp.int32, (tl, c2), 0) + pl.program_id(0) * tl
    col = lax.broadcasted_iota(jnp.int32, (tl, c2), 1)
    hw = jnp.where((row == 0) & (col >= c2 // 2), 0.0, hw)
    hw_ref[...] = hw

    @pl.when(pl.program_id(0) == 0)
    def _():
        l1_ref[...] = jnp.zeros_like(l1_ref)

    l1_ref[...] += jnp.sum(jnp.abs(hw), axis=0, keepdims=True)


def _filter_mlp(z, w1, b1, w2, b2, w3, b3, fr, wout, deltas2, *, tl):
    seq_len, e = z.shape
    o = w2.shape[0]
    c2 = wout.shape[1]
    full = lambda shape: pl.BlockSpec(shape, lambda i: (0, 0))
    return pl.pallas_call(
        _filter_mlp_kernel,
        grid=(seq_len // tl,),
        in_specs=[pl.BlockSpec((tl, e), lambda i: (i, 0)), full((e, o)), full((1, o)),
                  full((o, o)), full((1, o)), full((o, o)), full((1, o)), full((1, o)),
                  full((o, c2)), full((1, c2))],
        out_specs=[pl.BlockSpec((tl, c2), lambda i: (i, 0)), full((1, c2))],
        out_shape=[jax.ShapeDtypeStruct((seq_len, c2), F32), jax.ShapeDtypeStruct((1, c2), F32)],
        compiler_params=_params("arbitrary"),
        name="filter_mlp",
    )(z, w1, b1, w2, b2, w3, b3, fr, wout, deltas2)


def _dft1_kernel(w_ref, x_ref, o_ref, a_ref):
    g = x_ref.shape[2]
    xs = jnp.swapaxes(x_ref[0], 0, 1).astype(BF16)
    w = w_ref[...]
    for j in range(g):
        a_ref[j] = jnp.dot(w, xs[j], preferred_element_type=F32)
    o_ref[0] = jnp.swapaxes(a_ref[...], 0, 1).astype(o_ref.dtype)


def _dft1(w, x4, *, g, ct, name):
    nb, hq, r, c = x4.shape
    mo = w.shape[0]
    return pl.pallas_call(
        _dft1_kernel,
        grid=(nb, r // g, c // ct),
        in_specs=[pl.BlockSpec((mo, hq), lambda b, i, j: (0, 0)),
                  pl.BlockSpec((1, hq, g, ct), lambda b, i, j: (b, 0, i, j))],
        out_specs=pl.BlockSpec((1, mo, g, ct), lambda b, i, j: (b, 0, i, j)),
        out_shape=jax.ShapeDtypeStruct((nb, mo, r, c), BF16),
        scratch_shapes=[pltpu.VMEM((g, mo, ct), F32)],
        compiler_params=_params("parallel", "parallel", "parallel"),
        name=name,
    )(w, x4)


def _filter_spec_kernel(af_ref, ab_ref, mf_ref, l1f_ref, l1b_ref, k_ref):
    kb = mf_ref.shape[0]
    r = mf_ref.shape[1] // 2
    ct = k_ref.shape[2]
    scale = 1.0 / (l1f_ref[...] + l1b_ref[...] + EPS)

    for kk in range(kb):
        mfk = mf_ref[kk]
        xf = jnp.dot(mfk, af_ref[0, :, kk].reshape(2 * r, ct), preferred_element_type=F32)
        xb = jnp.dot(mfk, ab_ref[0, :, kk].reshape(2 * r, ct), preferred_element_type=F32)
        k_ref[kk, 0:r, :] = (xf[:r] + xb[:r]) * scale
        k_ref[kk, r:, :] = (xf[r:] - xb[r:]) * scale


def _filter_spec(ak5, mf, l1, *, kb, ct):
    _, _, hq, r, c2 = ak5.shape
    c = c2 // 2
    nct = c // ct
    return pl.pallas_call(
        _filter_spec_kernel,
        grid=(hq // kb, nct),
        in_specs=[pl.BlockSpec((1, 2, kb, r, ct), lambda i, j: (0, 0, i, 0, j)),
                  pl.BlockSpec((1, 2, kb, r, ct), lambda i, j: (0, 0, i, 0, j + nct)),
                  pl.BlockSpec((kb, 2 * r, 2 * r), lambda i, j: (i, 0, 0)),
                  pl.BlockSpec((1, ct), lambda i, j: (0, j)),
                  pl.BlockSpec((1, ct), lambda i, j: (0, j + nct))],
        out_specs=pl.BlockSpec((kb, 2 * r, ct), lambda i, j: (i, 0, j)),
        out_shape=jax.ShapeDtypeStruct((hq, 2 * r, c), F32),
        compiler_params=_params("parallel", "parallel"),
        name="filter_spec",
    )(ak5, ak5, mf, l1, l1)


def _spec_conv_kernel(a_ref, mf_ref, mi_ref, k_ref, dr_ref, di_ref, d_scr):
    kb = mf_ref.shape[0]
    r = mf_ref.shape[1] // 2
    ct = k_ref.shape[2]

    for kk in range(kb):
        a = a_ref[0, :, kk].reshape(2 * r, ct)
        x = jnp.dot(mf_ref[kk], a, preferred_element_type=F32)
        xr, xi = x[:r], x[r:]
        kr, ki = k_ref[kk, 0:r, :], k_ref[kk, r:, :]
        y = jnp.concatenate([xr * kr - xi * ki, xr * ki + xi * kr], axis=0).astype(BF16)
        dd = jnp.dot(mi_ref[kk], y, preferred_element_type=F32)
        d_scr[0, kk] = dd[:r]
        d_scr[1, kk] = dd[r:]
    dr_ref[0] = jnp.swapaxes(d_scr[0], 0, 1).astype(dr_ref.dtype)
    di_ref[0] = jnp.swapaxes(d_scr[1], 0, 1).astype(di_ref.dtype)


def _spec_conv(a5, mf, mi, khat, *, kb, ct):
    nb, _, hq, r, c = a5.shape
    mspec = pl.BlockSpec((kb, 2 * r, 2 * r), lambda i, j, b: (i, 0, 0))
    ospec = pl.BlockSpec((1, r, kb, ct), lambda i, j, b: (b, 0, i, j))
    oshape = jax.ShapeDtypeStruct((nb, r, hq, c), BF16)
    return pl.pallas_call(
        _spec_conv_kernel,
        grid=(hq // kb, c // ct, nb),
        in_specs=[pl.BlockSpec((1, 2, kb, r, ct), lambda i, j, b: (b, 0, i, 0, j)), mspec, mspec,
                  pl.BlockSpec((kb, 2 * r, ct), lambda i, j, b: (i, 0, j))],
        out_specs=[ospec, ospec],
        out_shape=[oshape, oshape],
        scratch_shapes=[pltpu.VMEM((2, kb, r, ct), F32)],
        compiler_params=_params("parallel", "parallel", "arbitrary"),
        name="spec_conv",
    )(a5, mf, mi, khat)


def _conv_out_kernel(wr_ref, wi_ref, dr_ref, di_ref, u_ref, x0_ref, ds_ref, o_ref, y_scr):
    g = dr_ref.shape[1]
    wr, wi = wr_ref[...], wi_ref[...]
    for j in range(g):
        y_scr[j] = (jnp.dot(wr, dr_ref[0, j], preferred_element_type=F32)
                    + jnp.dot(wi, di_ref[0, j], preferred_element_type=F32))
    yconv = jnp.swapaxes(y_scr[...], 0, 1)
    o_ref[0] = x0_ref[0] * (yconv + u_ref[0] * ds_ref[...])


def _conv_out(wr, wi, dr, di, u4, x04, ds, *, g, ct):
    nb, r, hq, c = dr.shape
    dspec = pl.BlockSpec((1, g, hq, ct), lambda b, i, j: (b, i, 0, j))
    nspec = pl.BlockSpec((1, hq, g, ct), lambda b, i, j: (b, 0, i, j))
    wspec = pl.BlockSpec((hq, hq), lambda b, i, j: (0, 0))
    return pl.pallas_call(
        _conv_out_kernel,
        grid=(nb, r // g, c // ct),
        in_specs=[wspec, wspec, dspec, dspec, nspec, nspec,
                  pl.BlockSpec((1, ct), lambda b, i, j: (0, j))],
        out_specs=nspec,
        out_shape=jax.ShapeDtypeStruct((nb, hq, r, c), F32),
        scratch_shapes=[pltpu.VMEM((g, hq, ct), F32)],
        compiler_params=_params("parallel", "parallel", "parallel"),
        name="conv_out",
    )(wr, wi, dr, di, u4, x04, ds)


def _out_proj_kernel(ma_ref, yb_ref, x_ref, gb_ref, wa_ref, wb_ref, g2_ref, x1_ref, h2_ref):
    mb = _rms(yb_ref[...], gb_ref[...]).astype(BF16)
    x1 = (x_ref[...]
          + jnp.dot(ma_ref[...], wa_ref[...], preferred_element_type=F32)
          + jnp.dot(mb, wb_ref[...], preferred_element_type=F32))
    x1_ref[...] = x1
    h2_ref[...] = _rms(x1, g2_ref[...]).astype(h2_ref.dtype)


def _out_proj(ma, yb, x2d, gb, w_out, g2, *, tm):
    m, d = x2d.shape
    d_a = ma.shape[1]
    d_b = yb.shape[1]
    row = lambda w: pl.BlockSpec((tm, w), lambda i: (i, 0))
    return pl.pallas_call(
        _out_proj_kernel,
        grid=(m // tm,),
        in_specs=[row(d_a), row(d_b), row(d), pl.BlockSpec((1, d_b), lambda i: (0, 0)),
                  pl.BlockSpec((d_a, d), lambda i: (0, 0)),
                  pl.BlockSpec((d_b, d), lambda i: (d_a // d_b, 0)),
                  pl.BlockSpec((1, d), lambda i: (0, 0))],
        out_specs=[row(d), row(d)],
        out_shape=[jax.ShapeDtypeStruct((m, d), F32), jax.ShapeDtypeStruct((m, d), BF16)],
        compiler_params=_params("parallel"),
        name="out_proj",
    )(ma, yb, x2d, gb, w_out, w_out, g2)


def _ffn_kernel(hp_ref, h_ref, hn_ref, x1_ref, wg_ref, wv_ref, cw_ref, cb_ref, wd_ref, gf_ref,
                o_ref, hext_ref, *, seq_tiles, final_norm, d_ff):
    tm = h_ref.shape[0]
    tf = wg_ref.shape[1]
    i = pl.program_id(0)
    f = pl.program_id(1)

    @pl.when(f == 0)
    def _():
        _fill_ext(hext_ref, hp_ref[...], h_ref[...], hn_ref[...], i, seq_tiles)
        o_ref[...] = x1_ref[...]

    gext = jnp.dot(hext_ref[...], wg_ref[...], preferred_element_type=F32)
    g = _conv3(gext, cw_ref, cb_ref, tm)
    val = jnp.dot(h_ref[...], wv_ref[...], preferred_element_type=F32)
    act = _gelu(g) * val
    dup = f * tf - _ragged_start(f, tf, d_ff)
    col = lax.broadcasted_iota(jnp.int32, act.shape, 1)
    act = jnp.where(col >= dup, act, 0.0).astype(BF16)
    o_ref[...] += jnp.dot(act, wd_ref[...], preferred_element_type=F32)

    if final_norm:
        @pl.when(f == pl.num_programs(1) - 1)
        def _():
            o_ref[...] = _rms(o_ref[...], gf_ref[...])


def _ragged_start(j, tf, d_ff, base=0):
    return (base // LANE + jnp.minimum(j * (tf // LANE), (d_ff - tf) // LANE)) * LANE


def _ffn(h2, x1, w_up, cw, cb, wd, gf, *, seq_len, tm, tf, final_norm):
    m, d = x1.shape
    d_ff = wd.shape[0]
    assert d_ff % LANE == 0 and tf % LANE == 0 and d_ff >= tf
    el = pl.Element
    start = functools.partial(_ragged_start, tf=tf, d_ff=d_ff)
    return pl.pallas_call(
        functools.partial(_ffn_kernel, seq_tiles=seq_len // tm, final_norm=final_norm, d_ff=d_ff),
        grid=(m // tm, pl.cdiv(d_ff, tf)),
        in_specs=_halo_specs(tm, d, m) + [
            pl.BlockSpec((tm, d), lambda i, j: (i, 0)),
            pl.BlockSpec((el(d), el(tf)), lambda i, j: (0, start(j))),
            pl.BlockSpec((el(d), el(tf)), lambda i, j: (0, start(j, base=d_ff))),
            pl.BlockSpec((el(3), el(tf)), lambda i, j: (0, start(j))),
            pl.BlockSpec((el(1), el(tf)), lambda i, j: (0, start(j))),
            pl.BlockSpec((el(tf), el(d)), lambda i, j: (start(j), 0)),
            pl.BlockSpec((1, d), lambda i, j: (0, 0)),
        ],
        out_specs=pl.BlockSpec((tm, d), lambda i, j: (i, 0)),
        out_shape=jax.ShapeDtypeStruct((m, d), F32),
        scratch_shapes=[pltpu.VMEM((tm + 2 * HALO, d), BF16)],
        compiler_params=_params("parallel", "arbitrary"),
        name="ffn",
    )(h2, h2, h2, x1, w_up, w_up, cw, cb, wd, gf)


@functools.lru_cache(maxsize=None)
def _dft_constants(seq_len):
    n = 2 * seq_len
    r = DFT_R
    q = n // r
    hq = q // 2
    k1 = np.arange(hq, dtype=np.float64)[:, None]
    n1 = np.arange(hq, dtype=np.float64)[None, :]
    th1 = 2.0 * np.pi * n1 * (k1 + 0.5) / q
    f1r, f1i = np.cos(th1), -np.sin(th1)
    f1s = np.concatenate([f1r, f1i], axis=0)
    f1rt, f1it = (2.0 / n) * f1r.T, (2.0 / n) * f1i.T
    k2 = np.arange(r, dtype=np.float64)[None, :, None]
    n2 = np.arange(r, dtype=np.float64)[None, None, :]
    kk = np.arange(hq, dtype=np.float64)[:, None, None]
    th2 = 2.0 * np.pi * (n2 * k2 / r + n2 * (kk + 0.5) / n)
    gr, gi = np.cos(th2), -np.sin(th2)
    mf = np.concatenate([np.concatenate([gr, -gi], axis=2),
                         np.concatenate([gi, gr], axis=2)], axis=1)
    grt, git = gr.transpose(0, 2, 1), gi.transpose(0, 2, 1)
    mi = np.concatenate([np.concatenate([grt, git], axis=2),
                         np.concatenate([-git, grt], axis=2)], axis=1)
    return f1s, f1rt, f1it, mf, mi


def _positional_features(seq_len, e_pad):
    t = jnp.linspace(0.0, 1.0, seq_len, dtype=F32)[:, None]
    w = (2.0 * math.pi / seq_len) * jnp.arange(seq_len, dtype=F32)[:, None]
    f = jnp.linspace(1e-4, N_BANDS - 1, N_BANDS, dtype=F32)[None]
    z = jnp.concatenate([t, jnp.cos(f * w), -jnp.sin(f * w)], axis=-1)
    return jnp.pad(z, ((0, 0), (0, e_pad - z.shape[1])))


def kernel(x, norm1_g, w_in, sgu_norm_g, sgu_w, sgu_b, hy_conv_w, hy_conv_b, hy_f_w1, hy_f_b1,
           hy_f_w2, hy_f_b2, hy_f_w3, hy_f_b3, hy_f_freq, hy_f_wout, hy_d_skip, outnorm_a_g,
           outnorm_b_g, w_out, norm2_g, ffn_w_up, ffn_dw_w, ffn_dw_b, ffn_w_down, final_g):
    nb, seq_len, d = x.shape
    depth = w_in.shape[0]
    d_a = sgu_norm_g.shape[1]
    d_b = hy_d_skip.shape[1]
    d_ff = ffn_dw_b.shape[1]
    m = nb * seq_len
    r = DFT_R
    hq = seq_len // r
    g = DFT_G
    kb = min(DFT_G, hq)
    tm = 512
    tf = 512

    f1s, f1rt, f1it, mf, mi = (jnp.asarray(a, F32).astype(BF16) for a in _dft_constants(seq_len))
    e_pad = hy_f_w2.shape[1]
    z = _positional_features(seq_len, e_pad)
    max_decay = math.log(DECAY_TARGET) / FAST_DECAY_PCT
    min_decay = math.log(DECAY_TARGET) / SLOW_DECAY_PCT
    deltas = jnp.abs(jnp.linspace(min_decay, max_decay, d_b, dtype=F32))
    deltas2 = jnp.concatenate([deltas, deltas])[None]

    x2d = x.reshape(m, d)
    row = lambda v: v.reshape(1, -1)
    for l in range(depth):
        w_in_l = w_in[l].astype(BF16)
        g1 = row(norm1_g[l])

        bias_full = jnp.repeat(sgu_b[l].T, d_a // N_HEADS_A, axis=1)
        ma = _in_proj_a(x2d, g1, w_in_l, row(sgu_norm_g[l]), sgu_w[l].astype(BF16),
                        bias_full, row(outnorm_a_g[l]), tm=256)

        x0, u = _in_proj_b(x2d, g1, w_in_l, hy_conv_w[l], row(hy_conv_b[l]),
                           col0=2 * d_a, d_b=d_b, seq_len=seq_len, tm=tm, tn=512)
        u4 = u.reshape(nb, hq, r, d_b)
        x04 = x0.reshape(nb, hq, r, d_b)

        w1p = jnp.pad(hy_f_w1[l], ((0, e_pad - hy_f_w1.shape[1]), (0, 0)))
        hw, l1 = _filter_mlp(z, w1p, row(hy_f_b1[l]), hy_f_w2[l], row(hy_f_b2[l]), hy_f_w3[l],
                             row(hy_f_b3[l]), row(hy_f_freq[l]), hy_f_wout[l], deltas2, tl=512)
        ak = _dft1(f1s, hw.reshape(1, hq, r, 2 * d_b), g=g, ct=512, name="filter_dft1")
        khat = _filter_spec(ak.reshape(1, 2, hq, r, 2 * d_b), mf, l1, kb=kb, ct=256)

        a = _dft1(f1s, u4, g=g, ct=512, name="u_dft1")
        dr, di = _spec_conv(a.reshape(nb, 2, hq, r, d_b), mf, mi, khat, kb=kb, ct=256)
        yb = _conv_out(f1rt, f1it, dr, di, u4, x04, row(hy_d_skip[l]), g=g, ct=512)

        x1, h2 = _out_proj(ma, yb.reshape(m, d_b), x2d, row(outnorm_b_g[l]),
                           w_out[l].astype(BF16), row(norm2_g[l]), tm=tm)

        x2d = _ffn(h2, x1, ffn_w_up[l].astype(BF16), ffn_dw_w[l], row(ffn_dw_b[l]),
                   ffn_w_down[l].astype(BF16), row(final_g), seq_len=seq_len, tm=tm, tf=tf,
                   final_norm=(l == depth - 1))
    return x2d.reshape(nb, seq_len, d)
```

```python
import functools
import math

import numpy as np
import jax
import jax.numpy as jnp
from jax import lax
from jax.experimental import pallas as pl
from jax.experimental.pallas import tpu as pltpu

EPS = 1e-6
CHUNK = 128
N_HEADS_A = 8
DECAY_TARGET = 1e-2
FAST_DECAY_PCT = 0.3
SLOW_DECAY_PCT = 1.5
N_BANDS = 16

LANE = 128
DFT_R = 128
DFT_G = 16
HALO = 16
VMEM_LIMIT = 56 * 1024 * 1024

F32 = jnp.float32
BF16 = jnp.bfloat16


def _gelu(x):
    return 0.5 * x * (1.0 + lax.erf(x * np.float32(math.sqrt(0.5))))


def _rms(x, g):
    return x * lax.rsqrt(jnp.mean(x * x, axis=-1, keepdims=True) + EPS) * g


def _params(*sem):
    return pltpu.CompilerParams(dimension_semantics=sem, vmem_limit_bytes=VMEM_LIMIT)


def _in_proj_a_kernel(x_ref, g1_ref, w_ref, gs_ref, ws_ref, bias_ref, go_ref, o_ref, ya_ref):
    tm = x_ref.shape[0]
    d_a = o_ref.shape[1]
    hd = d_a // N_HEADS_A
    h = _rms(x_ref[...], g1_ref[...]).astype(BF16)
    p = jnp.dot(h, w_ref[...], preferred_element_type=F32)
    gv = _gelu(p[:, d_a:])
    zv = _rms(gv, gs_ref[...]).astype(BF16)
    for c in range(tm // CHUNK):
        rows = slice(c * CHUNK, (c + 1) * CHUNK)
        for hh in range(N_HEADS_A):
            cols = slice(hh * hd, (hh + 1) * hd)
            s = jnp.dot(ws_ref[hh], zv[rows, cols], preferred_element_type=F32)
            ya_ref[rows, cols] = _gelu(p[rows, cols]) * (s + bias_ref[:, cols])
    o_ref[...] = _rms(ya_ref[...], go_ref[...]).astype(o_ref.dtype)


def _in_proj_a(x2d, g1, w_a, gs, ws, bias_full, go, *, tm):
    m, d = x2d.shape
    d_a = gs.shape[1]
    return pl.pallas_call(
        _in_proj_a_kernel,
        grid=(m // tm,),
        in_specs=[
            pl.BlockSpec((tm, d), lambda i: (i, 0)),
            pl.BlockSpec((1, d), lambda i: (0, 0)),
            pl.BlockSpec((d, 2 * d_a), lambda i: (0, 0)),
            pl.BlockSpec((1, d_a), lambda i: (0, 0)),
            pl.BlockSpec((N_HEADS_A, CHUNK, CHUNK), lambda i: (0, 0, 0)),
            pl.BlockSpec((CHUNK, d_a), lambda i: (0, 0)),
            pl.BlockSpec((1, d_a), lambda i: (0, 0)),
        ],
        out_specs=pl.BlockSpec((tm, d_a), lambda i: (i, 0)),
        out_shape=jax.ShapeDtypeStruct((m, d_a), BF16),
        scratch_shapes=[pltpu.VMEM((tm, d_a), F32)],
        compiler_params=_params("parallel"),
        name="in_proj_a",
    )(x2d, g1, w_a, gs, ws, bias_full, go)


def _fill_ext(hext_ref, prev, main, nxt, i, seq_tiles):
    tm = main.shape[0]
    has_prev = (i % seq_tiles) != 0
    has_next = ((i + 1) % seq_tiles) != 0
    hext_ref[0:HALO, :] = jnp.where(has_prev, prev, jnp.zeros_like(prev))
    hext_ref[HALO:HALO + tm, :] = main
    hext_ref[HALO + tm:, :] = jnp.where(has_next, nxt, jnp.zeros_like(nxt))


def _conv3(gext, w_ref, b_ref, tm):
    n = gext.shape[0]
    prev = pltpu.roll(gext, 1, 0)[HALO:HALO + tm]
    nxt = pltpu.roll(gext, n - 1, 0)[HALO:HALO + tm]
    cur = gext[HALO:HALO + tm]
    return prev * w_ref[0:1, :] + cur * w_ref[1:2, :] + nxt * w_ref[2:3, :] + b_ref[...]


def _in_proj_b_kernel(xp_ref, x_ref, xn_ref, g1_ref, w0_ref, w1_ref, w2_ref,
                      cw0_ref, cw1_ref, cw2_ref, cb0_ref, cb1_ref, cb2_ref,
                      x0_ref, u_ref, hext_ref, *, seq_tiles):
    tm = x_ref.shape[0]
    i = pl.program_id(0)

    @pl.when(pl.program_id(1) == 0)
    def _():
        g1 = g1_ref[...]
        _fill_ext(hext_ref,
                  _rms(xp_ref[...], g1).astype(BF16),
                  _rms(x_ref[...], g1).astype(BF16),
                  _rms(xn_ref[...], g1).astype(BF16), i, seq_tiles)

    hext = hext_ref[...]

    def branch(w_ref, cw_ref, cb_ref):
        gext = jnp.dot(hext, w_ref[...], preferred_element_type=F32)
        return _conv3(gext, cw_ref, cb_ref, tm)

    x0_ref[...] = branch(w0_ref, cw0_ref, cb0_ref)
    x1 = branch(w1_ref, cw1_ref, cb1_ref)
    v = branch(w2_ref, cw2_ref, cb2_ref)
    u_ref[...] = v * x1


def _halo_specs(tm, d, m):
    blocks_per_tile = tm // HALO
    last = m // HALO - 1
    return [
        pl.BlockSpec((HALO, d), lambda i, j: (jnp.maximum(i * blocks_per_tile - 1, 0), 0)),
        pl.BlockSpec((tm, d), lambda i, j: (i, 0)),
        pl.BlockSpec((HALO, d), lambda i, j: (jnp.minimum((i + 1) * blocks_per_tile, last), 0)),
    ]


def _in_proj_b(x2d, g1, w_in, conv_w, conv_b, *, col0, d_b, seq_len, tm, tn):
    m, d = x2d.shape
    nj = d_b // tn
    w_specs = [pl.BlockSpec((d, tn), functools.partial(
        lambda i, j, off: (0, off + j), off=(col0 + g * d_b) // tn)) for g in range(3)]
    cw_specs = [pl.BlockSpec((3, tn), functools.partial(
        lambda i, j, off: (0, off + j), off=(g * d_b) // tn)) for g in range(3)]
    cb_specs = [pl.BlockSpec((1, tn), functools.partial(
        lambda i, j, off: (0, off + j), off=(g * d_b) // tn)) for g in range(3)]
    out_spec = pl.BlockSpec((tm, tn), lambda i, j: (i, j))
    return pl.pallas_call(
        functools.partial(_in_proj_b_kernel, seq_tiles=seq_len // tm),
        grid=(m // tm, nj),
        in_specs=_halo_specs(tm, d, m) + [pl.BlockSpec((1, d), lambda i, j: (0, 0))]
        + w_specs + cw_specs + cb_specs,
        out_specs=[out_spec, out_spec],
        out_shape=[jax.ShapeDtypeStruct((m, d_b), F32)] * 2,
        scratch_shapes=[pltpu.VMEM((tm + 2 * HALO, d), BF16)],
        compiler_params=_params("parallel", "arbitrary"),
        name="in_proj_b",
    )(x2d, x2d, x2d, g1, w_in, w_in, w_in, conv_w, conv_w, conv_w, conv_b, conv_b, conv_b)


def _filter_mlp_kernel(z_ref, w1_ref, b1_ref, w2_ref, b2_ref, w3_ref, b3_ref, fr_ref,
                       wo_ref, dl_ref, hw_ref, l1_ref):
    hi = lax.Precision.HIGHEST
    z = z_ref[...]
    fr = fr_ref[...]
    h = jnp.sin(fr * (jnp.dot(z, w1_ref[...], precision=hi, preferred_element_type=F32) + b1_ref[...]))
    h = jnp.sin(fr * (jnp.dot(h, w2_ref[...], precision=hi, preferred_element_type=F32) + b2_ref[...]))
    h = jnp.sin(fr * (jnp.dot(h, w3_ref[...], precision=hi, preferred_element_type=F32) + b3_ref[...]))
    h = jnp.dot(h, wo_ref[...], precision=hi, preferred_element_type=F32)
    hw = h * jnp.exp(-z[:, 0:1] * dl_ref[...])
    tl, c2 = hw.shape
    row = lax.broadcasted_iota(jnp.int32, (tl, c2), 0) + pl.program_id(0) * tl
    col = lax.broadcasted_iota(jnp.int32, (tl, c2), 1)
    hw = jnp.where((row == 0) & (col >= c2 // 2), 0.0, hw)
    hw_ref[...] = hw

    @pl.when(pl.program_id(0) == 0)
    def _():
        l1_ref[...] = jnp.zeros_like(l1_ref)

    l1_ref[...] += jnp.sum(jnp.abs(hw), axis=0, keepdims=True)


def _filter_mlp(z, w1, b1, w2, b2, w3, b3, fr, wout, deltas2, *, tl):
    seq_len, e = z.shape
    o = w2.shape[0]
    c2 = wout.shape[1]
    full = lambda shape: pl.BlockSpec(shape, lambda i: (0, 0))
    return pl.pallas_call(
        _filter_mlp_kernel,
        grid=(seq_len // tl,),
        in_specs=[pl.BlockSpec((tl, e), lambda i: (i, 0)), full((e, o)), full((1, o)),
                  full((o, o)), full((1, o)), full((o, o)), full((1, o)), full((1, o)),
                  full((o, c2)), full((1, c2))],
        out_specs=[pl.BlockSpec((tl, c2), lambda i: (i, 0)), full((1, c2))],
        out_shape=[jax.ShapeDtypeStruct((seq_len, c2), F32), jax.ShapeDtypeStruct((1, c2), F32)],
        compiler_params=_params("arbitrary"),
        name="filter_mlp",
    )(z, w1, b1, w2, b2, w3, b3, fr, wout, deltas2)


def _dft1_kernel(w_ref, x_ref, o_ref, a_ref):
    g = x_ref.shape[2]
    xs = jnp.swapaxes(x_ref[0], 0, 1).astype(BF16)
    w = w_ref[...]
    for j in range(g):
        a_ref[j] = jnp.dot(w, xs[j], preferred_element_type=F32)
    o_ref[0] = jnp.swapaxes(a_ref[...], 0, 1).astype(o_ref.dtype)


def _dft1(w, x4, *, g, ct, name):
    nb, hq, r, c = x4.shape
    mo = w.shape[0]
    return pl.pallas_call(
        _dft1_kernel,
        grid=(nb, r // g, c // ct),
        in_specs=[pl.BlockSpec((mo, hq), lambda b, i, j: (0, 0)),
                  pl.BlockSpec((1, hq, g, ct), lambda b, i, j: (b, 0, i, j))],
        out_specs=pl.BlockSpec((1, mo, g, ct), lambda b, i, j: (b, 0, i, j)),
        out_shape=jax.ShapeDtypeStruct((nb, mo, r, c), BF16),
        scratch_shapes=[pltpu.VMEM((g, mo, ct), F32)],
        compiler_params=_params("parallel", "parallel", "parallel"),
        name=name,
    )(w, x4)


def _filter_spec_kernel(af_ref, ab_ref, mf_ref, l1f_ref, l1b_ref, k_ref):
    kb = mf_ref.shape[0]
    r = mf_ref.shape[1] // 2
    ct = k_ref.shape[2]
    scale = 1.0 / (l1f_ref[...] + l1b_ref[...] + EPS)

    for kk in range(kb):
        mfk = mf_ref[kk]
        xf = jnp.dot(mfk, af_ref[0, :, kk].reshape(2 * r, ct), preferred_element_type=F32)
        xb = jnp.dot(mfk, ab_ref[0, :, kk].reshape(2 * r, ct), preferred_element_type=F32)
        k_ref[kk, 0:r, :] = (xf[:r] + xb[:r]) * scale
        k_ref[kk, r:, :] = (xf[r:] - xb[r:]) * scale


def _filter_spec(ak5, mf, l1, *, kb, ct):
    _, _, hq, r, c2 = ak5.shape
    c = c2 // 2
    nct = c // ct
    return pl.pallas_call(
        _filter_spec_kernel,
        grid=(hq // kb, nct),
        in_specs=[pl.BlockSpec((1, 2, kb, r, ct), lambda i, j: (0, 0, i, 0, j)),
                  pl.BlockSpec((1, 2, kb, r, ct), lambda i, j: (0, 0, i, 0, j + nct)),
                  pl.BlockSpec((kb, 2 * r, 2 * r), lambda i, j: (i, 0, 0)),
                  pl.BlockSpec((1, ct), lambda i, j: (0, j)),
                  pl.BlockSpec((1, ct), lambda i, j: (0, j + nct))],
        out_specs=pl.BlockSpec((kb, 2 * r, ct), lambda i, j: (i, 0, j)),
        out_shape=jax.ShapeDtypeStruct((hq, 2 * r, c), F32),
        compiler_params=_params("parallel", "parallel"),
        name="filter_spec",
    )(ak5, ak5, mf, l1, l1)


def _spec_conv_kernel(a_ref, mf_ref, mi_ref, k_ref, dr_ref, di_ref, d_scr):
    kb = mf_ref.shape[0]
    r = mf_ref.shape[1] // 2
    ct = k_ref.shape[2]

    for kk in range(kb):
        a = a_ref[0, :, kk].reshape(2 * r, ct)
        x = jnp.dot(mf_ref[kk], a, preferred_element_type=F32)
        xr, xi = x[:r], x[r:]
        kr, ki = k_ref[kk, 0:r, :], k_ref[kk, r:, :]
        y = jnp.concatenate([xr * kr - xi * ki, xr * ki + xi * kr], axis=0).astype(BF16)
        dd = jnp.dot(mi_ref[kk], y, preferred_element_type=F32)
        d_scr[0, kk] = dd[:r]
        d_scr[1, kk] = dd[r:]
    dr_ref[0] = jnp.swapaxes(d_scr[0], 0, 1).astype(dr_ref.dtype)
    di_ref[0] = jnp.swapaxes(d_scr[1], 0, 1).astype(di_ref.dtype)


def _spec_conv(a5, mf, mi, khat, *, kb, ct):
    nb, _, hq, r, c = a5.shape
    mspec = pl.BlockSpec((kb, 2 * r, 2 * r), lambda i, j, b: (i, 0, 0))
    ospec = pl.BlockSpec((1, r, kb, ct), lambda i, j, b: (b, 0, i, j))
    oshape = jax.ShapeDtypeStruct((nb, r, hq, c), BF16)
    return pl.pallas_call(
        _spec_conv_kernel,
        grid=(hq // kb, c // ct, nb),
        in_specs=[pl.BlockSpec((1, 2, kb, r, ct), lambda i, j, b: (b, 0, i, 0, j)), mspec, mspec,
                  pl.BlockSpec((kb, 2 * r, ct), lambda i, j, b: (i, 0, j))],
        out_specs=[ospec, ospec],
        out_shape=[oshape, oshape],
        scratch_shapes=[pltpu.VMEM((2, kb, r, ct), F32)],
        compiler_params=_params("parallel", "parallel", "arbitrary"),
        name="spec_conv",
    )(a5, mf, mi, khat)


def _conv_out_kernel(wr_ref, wi_ref, dr_ref, di_ref, u_ref, x0_ref, ds_ref, o_ref, y_scr):
    g = dr_ref.shape[1]
    wr, wi = wr_ref[...], wi_ref[...]
    for j in range(g):
        y_scr[j] = (jnp.dot(wr, dr_ref[0, j], preferred_element_type=F32)
                    + jnp.dot(wi, di_ref[0, j], preferred_element_type=F32))
    yconv = jnp.swapaxes(y_scr[...], 0, 1)
    o_ref[0] = x0_ref[0] * (yconv + u_ref[0] * ds_ref[...])


def _conv_out(wr, wi, dr, di, u4, x04, ds, *, g, ct):
    nb, r, hq, c = dr.shape
    dspec = pl.BlockSpec((1, g, hq, ct), lambda b, i, j: (b, i, 0, j))
    nspec = pl.BlockSpec((1, hq, g, ct), lambda b, i, j: (b, 0, i, j))
    wspec = pl.BlockSpec((hq, hq), lambda b, i, j: (0, 0))
    return pl.pallas_call(
        _conv_out_kernel,
        grid=(nb, r // g, c // ct),
        in_specs=[wspec, wspec, dspec, dspec, nspec, nspec,
                  pl.BlockSpec((1, ct), lambda b, i, j: (0, j))],
        out_specs=nspec,
        out_shape=jax.ShapeDtypeStruct((nb, hq, r, c), F32),
        scratch_shapes=[pltpu.VMEM((g, hq, ct), F32)],
        compiler_params=_params("parallel", "parallel", "parallel"),
        name="conv_out",
    )(wr, wi, dr, di, u4, x04, ds)


def _out_proj_kernel(ma_ref, yb_ref, x_ref, gb_ref, wa_ref, wb_ref, g2_ref, x1_ref, h2_ref):
    mb = _rms(yb_ref[...], gb_ref[...]).astype(BF16)
    x1 = (x_ref[...]
          + jnp.dot(ma_ref[...], wa_ref[...], preferred_element_type=F32)
          + jnp.dot(mb, wb_ref[...], preferred_element_type=F32))
    x1_ref[...] = x1
    h2_ref[...] = _rms(x1, g2_ref[...]).astype(h2_ref.dtype)


def _out_proj(ma, yb, x2d, gb, w_out, g2, *, tm):
    m, d = x2d.shape
    d_a = ma.shape[1]
    d_b = yb.shape[1]
    row = lambda w: pl.BlockSpec((tm, w), lambda i: (i, 0))
    return pl.pallas_call(
        _out_proj_kernel,
        grid=(m // tm,),
        in_specs=[row(d_a), row(d_b), row(d), pl.BlockSpec((1, d_b), lambda i: (0, 0)),
                  pl.BlockSpec((d_a, d), lambda i: (0, 0)),
                  pl.BlockSpec((d_b, d), lambda i: (d_a // d_b, 0)),
                  pl.BlockSpec((1, d), lambda i: (0, 0))],
        out_specs=[row(d), row(d)],
        out_shape=[jax.ShapeDtypeStruct((m, d), F32), jax.ShapeDtypeStruct((m, d), BF16)],
        compiler_params=_params("parallel"),
        name="out_proj",
    )(ma, yb, x2d, gb, w_out, w_out, g2)


def _ffn_kernel(hp_ref, h_ref, hn_ref, x1_hbm, wg_ref, wv_ref, cw_ref, cb_ref, wd_ref, gf_ref,
                o_ref, hext_ref, x1_sem, *, seq_tiles, final_norm, d_ff):
    tm = h_ref.shape[0]
    tf = wg_ref.shape[1]
    i = pl.program_id(0)
    f = pl.program_id(1)
    x1_copy = pltpu.make_async_copy(x1_hbm.at[pl.ds(i * tm, tm), :], o_ref, x1_sem)

    @pl.when(f == 0)
    def _():
        x1_copy.start()
        _fill_ext(hext_ref, hp_ref[...], h_ref[...], hn_ref[...], i, seq_tiles)

    gext = jnp.dot(hext_ref[...], wg_ref[...], preferred_element_type=F32)
    g = _conv3(gext, cw_ref, cb_ref, tm)
    val = jnp.dot(h_ref[...], wv_ref[...], preferred_element_type=F32)
    act = _gelu(g) * val
    dup = f * tf - _ragged_start(f, tf, d_ff)
    col = lax.broadcasted_iota(jnp.int32, act.shape, 1)
    act = jnp.where(col >= dup, act, 0.0).astype(BF16)
    down = jnp.dot(act, wd_ref[...], preferred_element_type=F32)

    @pl.when(f == 0)
    def _():
        x1_copy.wait()

    o_ref[...] += down

    if final_norm:
        @pl.when(f == pl.num_programs(1) - 1)
        def _():
            o_ref[...] = _rms(o_ref[...], gf_ref[...])


def _ragged_start(j, tf, d_ff, base=0):
    return (base // LANE + jnp.minimum(j * (tf // LANE), (d_ff - tf) // LANE)) * LANE


def _ffn(h2, x1, w_up, cw, cb, wd, gf, *, seq_len, tm, tf, final_norm):
    m, d = x1.shape
    d_ff = wd.shape[0]
    assert d_ff % LANE == 0 and tf % LANE == 0 and d_ff >= tf
    el = pl.Element
    start = functools.partial(_ragged_start, tf=tf, d_ff=d_ff)
    return pl.pallas_call(
        functools.partial(_ffn_kernel, seq_tiles=seq_len // tm, final_norm=final_norm, d_ff=d_ff),
        grid=(m // tm, pl.cdiv(d_ff, tf)),
        in_specs=_halo_specs(tm, d, m) + [
            pl.BlockSpec(memory_space=pl.ANY),
            pl.BlockSpec((el(d), el(tf)), lambda i, j: (0, start(j))),
            pl.BlockSpec((el(d), el(tf)), lambda i, j: (0, start(j, base=d_ff))),
            pl.BlockSpec((el(3), el(tf)), lambda i, j: (0, start(j))),
            pl.BlockSpec((el(1), el(tf)), lambda i, j: (0, start(j))),
            pl.BlockSpec((el(tf), el(d)), lambda i, j: (start(j), 0)),
            pl.BlockSpec((1, d), lambda i, j: (0, 0)),
        ],
        out_specs=pl.BlockSpec((tm, d), lambda i, j: (i, 0)),
        out_shape=jax.ShapeDtypeStruct((m, d), F32),
        scratch_shapes=[pltpu.VMEM((tm + 2 * HALO, d), BF16), pltpu.SemaphoreType.DMA(())],
        compiler_params=_params("parallel", "arbitrary"),
        name="ffn",
    )(h2, h2, h2, x1, w_up, w_up, cw, cb, wd, gf)


@functools.lru_cache(maxsize=None)
def _dft_constants(seq_len):
    n = 2 * seq_len
    r = DFT_R
    q = n // r
    hq = q // 2
    k1 = np.arange(hq, dtype=np.float64)[:, None]
    n1 = np.arange(hq, dtype=np.float64)[None, :]
    th1 = 2.0 * np.pi * n1 * (k1 + 0.5) / q
    f1r, f1i = np.cos(th1), -np.sin(th1)
    f1s = np.concatenate([f1r, f1i], axis=0)
    f1rt, f1it = (2.0 / n) * f1r.T, (2.0 / n) * f1i.T
    k2 = np.arange(r, dtype=np.float64)[None, :, None]
    n2 = np.arange(r, dtype=np.float64)[None, None, :]
    kk = np.arange(hq, dtype=np.float64)[:, None, None]
    th2 = 2.0 * np.pi * (n2 * k2 / r + n2 * (kk + 0.5) / n)
    gr, gi = np.cos(th2), -np.sin(th2)
    mf = np.concatenate([np.concatenate([gr, -gi], axis=2),
                         np.concatenate([gi, gr], axis=2)], axis=1)
    grt, git = gr.transpose(0, 2, 1), gi.transpose(0, 2, 1)
    mi = np.concatenate([np.concatenate([grt, git], axis=2),
                         np.concatenate([-git, grt], axis=2)], axis=1)
    return f1s, f1rt, f1it, mf, mi


def _positional_features(seq_len, e_pad):
    t = jnp.linspace(0.0, 1.0, seq_len, dtype=F32)[:, None]
    w = (2.0 * math.pi / seq_len) * jnp.arange(seq_len, dtype=F32)[:, None]
    f = jnp.linspace(1e-4, N_BANDS - 1, N_BANDS, dtype=F32)[None]
    z = jnp.concatenate([t, jnp.cos(f * w), -jnp.sin(f * w)], axis=-1)
    return jnp.pad(z, ((0, 0), (0, e_pad - z.shape[1])))


def kernel(x, norm1_g, w_in, sgu_norm_g, sgu_w, sgu_b, hy_conv_w, hy_conv_b, hy_f_w1, hy_f_b1,
           hy_f_w2, hy_f_b2, hy_f_w3, hy_f_b3, hy_f_freq, hy_f_wout, hy_d_skip, outnorm_a_g,
           outnorm_b_g, w_out, norm2_g, ffn_w_up, ffn_dw_w, ffn_dw_b, ffn_w_down, final_g):
    nb, seq_len, d = x.shape
    depth = w_in.shape[0]
    d_a = sgu_norm_g.shape[1]
    d_b = hy_d_skip.shape[1]
    d_ff = ffn_dw_b.shape[1]
    m = nb * seq_len
    r = DFT_R
    hq = seq_len // r
    g = DFT_G
    kb = min(DFT_G, hq)
    tm = 512
    tf = 512

    f1s, f1rt, f1it, mf, mi = (jnp.asarray(a, F32).astype(BF16) for a in _dft_constants(seq_len))
    e_pad = hy_f_w2.shape[1]
    z = _positional_features(seq_len, e_pad)
    max_decay = math.log(DECAY_TARGET) / FAST_DECAY_PCT
    min_decay = math.log(DECAY_TARGET) / SLOW_DECAY_PCT
    deltas = jnp.abs(jnp.linspace(min_decay, max_decay, d_b, dtype=F32))
    deltas2 = jnp.concatenate([deltas, deltas])[None]

    x2d = x.reshape(m, d)
    row = lambda v: v.reshape(1, -1)
    for l in range(depth):
        w_in_l = w_in[l].astype(BF16)
        g1 = row(norm1_g[l])

        bias_full = jnp.repeat(sgu_b[l].T, d_a // N_HEADS_A, axis=1)
        ma = _in_proj_a(x2d, g1, w_in_l, row(sgu_norm_g[l]), sgu_w[l].astype(BF16),
                        bias_full, row(outnorm_a_g[l]), tm=512)

        x0, u = _in_proj_b(x2d, g1, w_in_l, hy_conv_w[l], row(hy_conv_b[l]),
                           col0=2 * d_a, d_b=d_b, seq_len=seq_len, tm=1024, tn=512)
        u4 = u.reshape(nb, hq, r, d_b)
        x04 = x0.reshape(nb, hq, r, d_b)

        w1p = jnp.pad(hy_f_w1[l], ((0, e_pad - hy_f_w1.shape[1]), (0, 0)))
        hw, l1 = _filter_mlp(z, w1p, row(hy_f_b1[l]), hy_f_w2[l], row(hy_f_b2[l]), hy_f_w3[l],
                             row(hy_f_b3[l]), row(hy_f_freq[l]), hy_f_wout[l], deltas2, tl=512)
        ak = _dft1(f1s, hw.reshape(1, hq, r, 2 * d_b), g=g, ct=512, name="filter_dft1")
        khat = _filter_spec(ak.reshape(1, 2, hq, r, 2 * d_b), mf, l1, kb=kb, ct=256)

        a = _dft1(f1s, u4, g=g, ct=512, name="u_dft1")
        dr, di = _spec_conv(a.reshape(nb, 2, hq, r, d_b), mf, mi, khat, kb=kb, ct=256)
        yb = _conv_out(f1rt, f1it, dr, di, u4, x04, row(hy_d_skip[l]), g=g, ct=512)

        x1, h2 = _out_proj(ma, yb.reshape(m, d_b), x2d, row(outnorm_b_g[l]),
                           w_out[l].astype(BF16), row(norm2_g[l]), tm=tm)

        x2d = _ffn(h2, x1, ffn_w_up[l].astype(BF16), ffn_dw_w[l], row(ffn_dw_b[l]),
                   ffn_w_down[l].astype(BF16), row(final_g), seq_len=seq_len, tm=1024, tf=tf,
                   final_norm=(l == depth - 1))
    return x2d.reshape(nb, seq_len, d)
```

```python
import functools
import math

import numpy as np
import jax
import jax.numpy as jnp
from jax import lax
from jax.experimental import pallas as pl
from jax.experimental.pallas import tpu as pltpu

EPS = 1e-6
CHUNK = 128
N_HEADS_A = 8
DECAY_TARGET = 1e-2
FAST_DECAY_PCT = 0.3
SLOW_DECAY_PCT = 1.5
N_BANDS = 16

LANE = 128
DFT_R = 128
DFT_G = 16
HALO = 16
VMEM_LIMIT = 56 * 1024 * 1024

F32 = jnp.float32
BF16 = jnp.bfloat16


def _gelu(x):
    return 0.5 * x * (1.0 + lax.erf(x * np.float32(math.sqrt(0.5))))


def _rms(x, g):
    return x * lax.rsqrt(jnp.mean(x * x, axis=-1, keepdims=True) + EPS) * g


def _params(*sem):
    return pltpu.CompilerParams(dimension_semantics=sem, vmem_limit_bytes=VMEM_LIMIT)


def _in_proj_a_kernel(x_ref, g1_ref, w_ref, gs_ref, ws_ref, bias_ref, go_ref, o_ref, ya_ref):
    tm = x_ref.shape[0]
    d_a = o_ref.shape[1]
    hd = d_a // N_HEADS_A
    h = _rms(x_ref[...], g1_ref[...]).astype(BF16)
    p = jnp.dot(h, w_ref[...], preferred_element_type=F32)
    gv = _gelu(p[:, d_a:])
    zv = _rms(gv, gs_ref[...]).astype(BF16)
    for c in range(tm // CHUNK):
        rows = slice(c * CHUNK, (c + 1) * CHUNK)
        for hh in range(N_HEADS_A):
            cols = slice(hh * hd, (hh + 1) * hd)
            s = jnp.dot(ws_ref[hh], zv[rows, cols], preferred_element_type=F32)
            ya_ref[rows, cols] = _gelu(p[rows, cols]) * (s + bias_ref[:, cols])
    o_ref[...] = _rms(ya_ref[...], go_ref[...]).astype(o_ref.dtype)


def _in_proj_a(x2d, g1, w_a, gs, ws, bias_full, go, *, tm):
    m, d = x2d.shape
    d_a = gs.shape[1]
    return pl.pallas_call(
        _in_proj_a_kernel,
        grid=(m // tm,),
        in_specs=[
            pl.BlockSpec((tm, d), lambda i: (i, 0)),
            pl.BlockSpec((1, d), lambda i: (0, 0)),
            pl.BlockSpec((d, 2 * d_a), lambda i: (0, 0)),
            pl.BlockSpec((1, d_a), lambda i: (0, 0)),
            pl.BlockSpec((N_HEADS_A, CHUNK, CHUNK), lambda i: (0, 0, 0)),
            pl.BlockSpec((CHUNK, d_a), lambda i: (0, 0)),
            pl.BlockSpec((1, d_a), lambda i: (0, 0)),
        ],
        out_specs=pl.BlockSpec((tm, d_a), lambda i: (i, 0)),
        out_shape=jax.ShapeDtypeStruct((m, d_a), BF16),
        scratch_shapes=[pltpu.VMEM((tm, d_a), F32)],
        compiler_params=_params("parallel"),
        name="in_proj_a",
    )(x2d, g1, w_a, gs, ws, bias_full, go)


def _fill_ext(hext_ref, prev, main, nxt, i, seq_tiles):
    tm = main.shape[0]
    has_prev = (i % seq_tiles) != 0
    has_next = ((i + 1) % seq_tiles) != 0
    hext_ref[0:HALO, :] = jnp.where(has_prev, prev, jnp.zeros_like(prev))
    hext_ref[HALO:HALO + tm, :] = main
    hext_ref[HALO + tm:, :] = jnp.where(has_next, nxt, jnp.zeros_like(nxt))


def _conv3(gext, w_ref, b_ref, tm):
    n = gext.shape[0]
    prev = pltpu.roll(gext, 1, 0)[HALO:HALO + tm]
    nxt = pltpu.roll(gext, n - 1, 0)[HALO:HALO + tm]
    cur = gext[HALO:HALO + tm]
    return prev * w_ref[0:1, :] + cur * w_ref[1:2, :] + nxt * w_ref[2:3, :] + b_ref[...]


def _in_proj_b_kernel(xp_ref, x_ref, xn_ref, g1_ref, w0_ref, w1_ref, w2_ref,
                      cw0_ref, cw1_ref, cw2_ref, cb0_ref, cb1_ref, cb2_ref,
                      x0_ref, u_ref, hext_ref, *, seq_tiles):
    tm = x_ref.shape[0]
    i = pl.program_id(0)

    @pl.when(pl.program_id(1) == 0)
    def _():
        g1 = g1_ref[...]
        _fill_ext(hext_ref,
                  _rms(xp_ref[...], g1).astype(BF16),
                  _rms(x_ref[...], g1).astype(BF16),
                  _rms(xn_ref[...], g1).astype(BF16), i, seq_tiles)

    hext = hext_ref[...]

    def branch(w_ref, cw_ref, cb_ref):
        gext = jnp.dot(hext, w_ref[...], preferred_element_type=F32)
        return _conv3(gext, cw_ref, cb_ref, tm)

    x0_ref[...] = branch(w0_ref, cw0_ref, cb0_ref)
    x1 = branch(w1_ref, cw1_ref, cb1_ref)
    v = branch(w2_ref, cw2_ref, cb2_ref)
    u_ref[...] = v * x1


def _halo_specs(tm, d, m):
    blocks_per_tile = tm // HALO
    last = m // HALO - 1
    return [
        pl.BlockSpec((HALO, d), lambda i, j: (jnp.maximum(i * blocks_per_tile - 1, 0), 0)),
        pl.BlockSpec((tm, d), lambda i, j: (i, 0)),
        pl.BlockSpec((HALO, d), lambda i, j: (jnp.minimum((i + 1) * blocks_per_tile, last), 0)),
    ]


def _in_proj_b(x2d, g1, w_in, conv_w, conv_b, *, col0, d_b, seq_len, tm, tn):
    m, d = x2d.shape
    nj = d_b // tn
    w_specs = [pl.BlockSpec((d, tn), functools.partial(
        lambda i, j, off: (0, off + j), off=(col0 + g * d_b) // tn)) for g in range(3)]
    cw_specs = [pl.BlockSpec((3, tn), functools.partial(
        lambda i, j, off: (0, off + j), off=(g * d_b) // tn)) for g in range(3)]
    cb_specs = [pl.BlockSpec((1, tn), functools.partial(
        lambda i, j, off: (0, off + j), off=(g * d_b) // tn)) for g in range(3)]
    out_spec = pl.BlockSpec((tm, tn), lambda i, j: (i, j))
    return pl.pallas_call(
        functools.partial(_in_proj_b_kernel, seq_tiles=seq_len // tm),
        grid=(m // tm, nj),
        in_specs=_halo_specs(tm, d, m) + [pl.BlockSpec((1, d), lambda i, j: (0, 0))]
        + w_specs + cw_specs + cb_specs,
        out_specs=[out_spec, out_spec],
        out_shape=[jax.ShapeDtypeStruct((m, d_b), F32)] * 2,
        scratch_shapes=[pltpu.VMEM((tm + 2 * HALO, d), BF16)],
        compiler_params=_params("parallel", "arbitrary"),
        name="in_proj_b",
    )(x2d, x2d, x2d, g1, w_in, w_in, w_in, conv_w, conv_w, conv_w, conv_b, conv_b, conv_b)


def _filter_mlp_kernel(z_ref, w1_ref, b1_ref, w2_ref, b2_ref, w3_ref, b3_ref, fr_ref,
                       wo_ref, dl_ref, hw_ref, l1_ref):
    hi = lax.Precision.HIGHEST
    z = z_ref[...]
    fr = fr_ref[...]
    h = jnp.sin(fr * (jnp.dot(z, w1_ref[...], precision=hi, preferred_element_type=F32) + b1_ref[...]))
    h = jnp.sin(fr * (jnp.dot(h, w2_ref[...], precision=hi, preferred_element_type=F32) + b2_ref[...]))
    h = jnp.sin(fr * (jnp.dot(h, w3_ref[...], precision=hi, preferred_element_type=F32) + b3_ref[...]))
    h = jnp.dot(h, wo_ref[...], precision=hi, preferred_element_type=F32)
    hw = h * jnp.exp(-z[:, 0:1] * dl_ref[...])
    tl, c2 = hw.shape
    row = lax.broadcasted_iota(jnp.int32, (tl, c2), 0) + pl.program_id(0) * tl
    col = lax.broadcasted_iota(jnp.int32, (tl, c2), 1)
    hw = jnp.where((row == 0) & (col >= c2 // 2), 0.0, hw)
    hw_ref[...] = hw

    @pl.when(pl.program_id(0) == 0)
    def _():
        l1_ref[...] = jnp.zeros_like(l1_ref)

    l1_ref[...] += jnp.sum(jnp.abs(hw), axis=0, keepdims=True)


def _filter_mlp(z, w1, b1, w2, b2, w3, b3, fr, wout, deltas2, *, tl):
    seq_len, e = z.shape
    o = w2.shape[0]
    c2 = wout.shape[1]
    full = lambda shape: pl.BlockSpec(shape, lambda i: (0, 0))
    return pl.pallas_call(
        _filter_mlp_kernel,
        grid=(seq_len // tl,),
        in_specs=[pl.BlockSpec((tl, e), lambda i: (i, 0)), full((e, o)), full((1, o)),
                  full((o, o)), full((1, o)), full((o, o)), full((1, o)), full((1, o)),
                  full((o, c2)), full((1, c2))],
        out_specs=[pl.BlockSpec((tl, c2), lambda i: (i, 0)), full((1, c2))],
        out_shape=[jax.ShapeDtypeStruct((seq_len, c2), F32), jax.ShapeDtypeStruct((1, c2), F32)],
        compiler_params=_params("arbitrary"),
        name="filter_mlp",
    )(z, w1, b1, w2, b2, w3, b3, fr, wout, deltas2)


def _dft1_kernel(w_ref, x_ref, o_ref, a_ref):
    g = x_ref.shape[2]
    xs = jnp.swapaxes(x_ref[0], 0, 1).astype(BF16)
    w = w_ref[...]
    for j in range(g):
        a_ref[j] = jnp.dot(w, xs[j], preferred_element_type=F32)
    o_ref[0] = jnp.swapaxes(a_ref[...], 0, 1).astype(o_ref.dtype)


def _dft1(w, x4, *, g, ct, name):
    nb, hq, r, c = x4.shape
    mo = w.shape[0]
    return pl.pallas_call(
        _dft1_kernel,
        grid=(nb, r // g, c // ct),
        in_specs=[pl.BlockSpec((mo, hq), lambda b, i, j: (0, 0)),
                  pl.BlockSpec((1, hq, g, ct), lambda b, i, j: (b, 0, i, j))],
        out_specs=pl.BlockSpec((1, mo, g, ct), lambda b, i, j: (b, 0, i, j)),
        out_shape=jax.ShapeDtypeStruct((nb, mo, r, c), BF16),
        scratch_shapes=[pltpu.VMEM((g, mo, ct), F32)],
        compiler_params=_params("parallel", "parallel", "parallel"),
        name=name,
    )(w, x4)


def _filter_spec_kernel(af_ref, ab_ref, mf_ref, l1f_ref, l1b_ref, k_ref):
    kb = mf_ref.shape[0]
    r = mf_ref.shape[1] // 2
    ct = k_ref.shape[2]
    scale = 1.0 / (l1f_ref[...] + l1b_ref[...] + EPS)

    for kk in range(kb):
        mfk = mf_ref[kk]
        xf = jnp.dot(mfk, af_ref[0, :, kk].reshape(2 * r, ct), preferred_element_type=F32)
        xb = jnp.dot(mfk, ab_ref[0, :, kk].reshape(2 * r, ct), preferred_element_type=F32)
        k_ref[kk, 0:r, :] = (xf[:r] + xb[:r]) * scale
        k_ref[kk, r:, :] = (xf[r:] - xb[r:]) * scale


def _filter_spec(ak5, mf, l1, *, kb, ct):
    _, _, hq, r, c2 = ak5.shape
    c = c2 // 2
    nct = c // ct
    return pl.pallas_call(
        _filter_spec_kernel,
        grid=(hq // kb, nct),
        in_specs=[pl.BlockSpec((1, 2, kb, r, ct), lambda i, j: (0, 0, i, 0, j)),
                  pl.BlockSpec((1, 2, kb, r, ct), lambda i, j: (0, 0, i, 0, j + nct)),
                  pl.BlockSpec((kb, 2 * r, 2 * r), lambda i, j: (i, 0, 0)),
                  pl.BlockSpec((1, ct), lambda i, j: (0, j)),
                  pl.BlockSpec((1, ct), lambda i, j: (0, j + nct))],
        out_specs=pl.BlockSpec((kb, 2 * r, ct), lambda i, j: (i, 0, j)),
        out_shape=jax.ShapeDtypeStruct((hq, 2 * r, c), F32),
        compiler_params=_params("parallel", "parallel"),
        name="filter_spec",
    )(ak5, ak5, mf, l1, l1)


def _spec_conv_kernel(a_ref, mf_ref, mi_ref, k_ref, dr_ref, di_ref, d_scr):
    kb = mf_ref.shape[0]
    r = mf_ref.shape[1] // 2
    ct = k_ref.shape[2]

    for kk in range(kb):
        a = a_ref[0, :, kk].reshape(2 * r, ct)
        x = jnp.dot(mf_ref[kk], a, preferred_element_type=F32)
        xr, xi = x[:r], x[r:]
        kr, ki = k_ref[kk, 0:r, :], k_ref[kk, r:, :]
        y = jnp.concatenate([xr * kr - xi * ki, xr * ki + xi * kr], axis=0).astype(BF16)
        dd = jnp.dot(mi_ref[kk], y, preferred_element_type=F32)
        d_scr[0, kk] = dd[:r]
        d_scr[1, kk] = dd[r:]
    dr_ref[0] = jnp.swapaxes(d_scr[0], 0, 1).astype(dr_ref.dtype)
    di_ref[0] = jnp.swapaxes(d_scr[1], 0, 1).astype(di_ref.dtype)


def _spec_conv(a5, mf, mi, khat, *, kb, ct):
    nb, _, hq, r, c = a5.shape
    mspec = pl.BlockSpec((kb, 2 * r, 2 * r), lambda i, j, b: (i, 0, 0))
    ospec = pl.BlockSpec((1, r, kb, ct), lambda i, j, b: (b, 0, i, j))
    oshape = jax.ShapeDtypeStruct((nb, r, hq, c), BF16)
    return pl.pallas_call(
        _spec_conv_kernel,
        grid=(hq // kb, c // ct, nb),
        in_specs=[pl.BlockSpec((1, 2, kb, r, ct), lambda i, j, b: (b, 0, i, 0, j)), mspec, mspec,
                  pl.BlockSpec((kb, 2 * r, ct), lambda i, j, b: (i, 0, j))],
        out_specs=[ospec, ospec],
        out_shape=[oshape, oshape],
        scratch_shapes=[pltpu.VMEM((2, kb, r, ct), F32)],
        compiler_params=_params("parallel", "parallel", "arbitrary"),
        name="spec_conv",
    )(a5, mf, mi, khat)


def _conv_out_kernel(wr_ref, wi_ref, dr_ref, di_ref, u_ref, x0_ref, ds_ref, o_ref, y_scr):
    g = dr_ref.shape[1]
    wr, wi = wr_ref[...], wi_ref[...]
    for j in range(g):
        y_scr[j] = (jnp.dot(wr, dr_ref[0, j], preferred_element_type=F32)
                    + jnp.dot(wi, di_ref[0, j], preferred_element_type=F32))
    yconv = jnp.swapaxes(y_scr[...], 0, 1)
    o_ref[0] = x0_ref[0] * (yconv + u_ref[0] * ds_ref[...])


def _conv_out(wr, wi, dr, di, u4, x04, ds, *, g, ct):
    nb, r, hq, c = dr.shape
    dspec = pl.BlockSpec((1, g, hq, ct), lambda b, i, j: (b, i, 0, j))
    nspec = pl.BlockSpec((1, hq, g, ct), lambda b, i, j: (b, 0, i, j))
    wspec = pl.BlockSpec((hq, hq), lambda b, i, j: (0, 0))
    return pl.pallas_call(
        _conv_out_kernel,
        grid=(nb, r // g, c // ct),
        in_specs=[wspec, wspec, dspec, dspec, nspec, nspec,
                  pl.BlockSpec((1, ct), lambda b, i, j: (0, j))],
        out_specs=nspec,
        out_shape=jax.ShapeDtypeStruct((nb, hq, r, c), F32),
        scratch_shapes=[pltpu.VMEM((g, hq, ct), F32)],
        compiler_params=_params("parallel", "parallel", "parallel"),
        name="conv_out",
    )(wr, wi, dr, di, u4, x04, ds)


def _out_proj_kernel(ma_ref, yb_ref, x_ref, gb_ref, wa_ref, wb_ref, g2_ref, x1_ref, h2_ref):
    mb = _rms(yb_ref[...], gb_ref[...]).astype(BF16)
    x1 = (x_ref[...]
          + jnp.dot(ma_ref[...], wa_ref[...], preferred_element_type=F32)
          + jnp.dot(mb, wb_ref[...], preferred_element_type=F32))
    x1_ref[...] = x1
    h2_ref[...] = _rms(x1, g2_ref[...]).astype(h2_ref.dtype)


def _out_proj(ma, yb, x2d, gb, w_out, g2, *, tm):
    m, d = x2d.shape
    d_a = ma.shape[1]
    d_b = yb.shape[1]
    row = lambda w: pl.BlockSpec((tm, w), lambda i: (i, 0))
    return pl.pallas_call(
        _out_proj_kernel,
        grid=(m // tm,),
        in_specs=[row(d_a), row(d_b), row(d), pl.BlockSpec((1, d_b), lambda i: (0, 0)),
                  pl.BlockSpec((d_a, d), lambda i: (0, 0)),
                  pl.BlockSpec((d_b, d), lambda i: (d_a // d_b, 0)),
                  pl.BlockSpec((1, d), lambda i: (0, 0))],
        out_specs=[row(d), row(d)],
        out_shape=[jax.ShapeDtypeStruct((m, d), F32), jax.ShapeDtypeStruct((m, d), BF16)],
        compiler_params=_params("parallel"),
        name="out_proj",
    )(ma, yb, x2d, gb, w_out, w_out, g2)


def _ffn_kernel(hp_ref, h_ref, hn_ref, x1_hbm, wg_ref, wv_ref, cw_ref, cb_ref, wd_ref, gf_ref,
                o_ref, hext_ref, x1_sem, *, seq_tiles, final_norm, d_ff):
    tm = h_ref.shape[0]
    tf = wg_ref.shape[1]
    i = pl.program_id(0)
    f = pl.program_id(1)
    x1_copy = pltpu.make_async_copy(x1_hbm.at[pl.ds(i * tm, tm), :], o_ref, x1_sem)

    def hidden_tile(first):
        gext = jnp.dot(hext_ref[...], wg_ref[...], preferred_element_type=F32)
        g = _conv3(gext, cw_ref, cb_ref, tm)
        val = jnp.dot(h_ref[...], wv_ref[...], preferred_element_type=F32)
        act = _gelu(g) * val
        dup = f * tf - _ragged_start(f, tf, d_ff)
        col = lax.broadcasted_iota(jnp.int32, act.shape, 1)
        act = jnp.where(col >= dup, act, 0.0).astype(BF16)
        if first:
            x1_copy.wait()
        o_ref[...] += jnp.dot(act, wd_ref[...], preferred_element_type=F32)

    @pl.when(f == 0)
    def _():
        x1_copy.start()
        _fill_ext(hext_ref, hp_ref[...], h_ref[...], hn_ref[...], i, seq_tiles)
        hidden_tile(True)

    @pl.when(f > 0)
    def _():
        hidden_tile(False)

    if final_norm:
        @pl.when(f == pl.num_programs(1) - 1)
        def _():
            o_ref[...] = _rms(o_ref[...], gf_ref[...])


def _ragged_start(j, tf, d_ff, base=0):
    return (base // LANE + jnp.minimum(j * (tf // LANE), (d_ff - tf) // LANE)) * LANE


def _ffn(h2, x1, w_up, cw, cb, wd, gf, *, seq_len, tm, tf, final_norm):
    m, d = x1.shape
    d_ff = wd.shape[0]
    assert d_ff % LANE == 0 and tf % LANE == 0 and d_ff >= tf
    el = pl.Element
    start = functools.partial(_ragged_start, tf=tf, d_ff=d_ff)
    return pl.pallas_call(
        functools.partial(_ffn_kernel, seq_tiles=seq_len // tm, final_norm=final_norm, d_ff=d_ff),
        grid=(m // tm, pl.cdiv(d_ff, tf)),
        in_specs=_halo_specs(tm, d, m) + [
            pl.BlockSpec(memory_space=pl.ANY),
            pl.BlockSpec((el(d), el(tf)), lambda i, j: (0, start(j))),
            pl.BlockSpec((el(d), el(tf)), lambda i, j: (0, start(j, base=d_ff))),
            pl.BlockSpec((el(3), el(tf)), lambda i, j: (0, start(j))),
            pl.BlockSpec((el(1), el(tf)), lambda i, j: (0, start(j))),
            pl.BlockSpec((el(tf), el(d)), lambda i, j: (start(j), 0)),
            pl.BlockSpec((1, d), lambda i, j: (0, 0)),
        ],
        out_specs=pl.BlockSpec((tm, d), lambda i, j: (i, 0)),
        out_shape=jax.ShapeDtypeStruct((m, d), F32),
        scratch_shapes=[pltpu.VMEM((tm + 2 * HALO, d), BF16), pltpu.SemaphoreType.DMA(())],
        compiler_params=_params("parallel", "arbitrary"),
        name="ffn",
    )(h2, h2, h2, x1, w_up, w_up, cw, cb, wd, gf)


@functools.lru_cache(maxsize=None)
def _dft_constants(seq_len):
    n = 2 * seq_len
    r = DFT_R
    q = n // r
    hq = q // 2
    k1 = np.arange(hq, dtype=np.float64)[:, None]
    n1 = np.arange(hq, dtype=np.float64)[None, :]
    th1 = 2.0 * np.pi * n1 * (k1 + 0.5) / q
    f1r, f1i = np.cos(th1), -np.sin(th1)
    f1s = np.concatenate([f1r, f1i], axis=0)
    f1rt, f1it = (2.0 / n) * f1r.T, (2.0 / n) * f1i.T
    k2 = np.arange(r, dtype=np.float64)[None, :, None]
    n2 = np.arange(r, dtype=np.float64)[None, None, :]
    kk = np.arange(hq, dtype=np.float64)[:, None, None]
    th2 = 2.0 * np.pi * (n2 * k2 / r + n2 * (kk + 0.5) / n)
    gr, gi = np.cos(th2), -np.sin(th2)
    mf = np.concatenate([np.concatenate([gr, -gi], axis=2),
                         np.concatenate([gi, gr], axis=2)], axis=1)
    grt, git = gr.transpose(0, 2, 1), gi.transpose(0, 2, 1)
    mi = np.concatenate([np.concatenate([grt, git], axis=2),
                         np.concatenate([-git, grt], axis=2)], axis=1)
    return f1s, f1rt, f1it, mf, mi


def _positional_features(seq_len, e_pad):
    t = jnp.linspace(0.0, 1.0, seq_len, dtype=F32)[:, None]
    w = (2.0 * math.pi / seq_len) * jnp.arange(seq_len, dtype=F32)[:, None]
    f = jnp.linspace(1e-4, N_BANDS - 1, N_BANDS, dtype=F32)[None]
    z = jnp.concatenate([t, jnp.cos(f * w), -jnp.sin(f * w)], axis=-1)
    return jnp.pad(z, ((0, 0), (0, e_pad - z.shape[1])))


def kernel(x, norm1_g, w_in, sgu_norm_g, sgu_w, sgu_b, hy_conv_w, hy_conv_b, hy_f_w1, hy_f_b1,
           hy_f_w2, hy_f_b2, hy_f_w3, hy_f_b3, hy_f_freq, hy_f_wout, hy_d_skip, outnorm_a_g,
           outnorm_b_g, w_out, norm2_g, ffn_w_up, ffn_dw_w, ffn_dw_b, ffn_w_down, final_g):
    nb, seq_len, d = x.shape
    depth = w_in.shape[0]
    d_a = sgu_norm_g.shape[1]
    d_b = hy_d_skip.shape[1]
    d_ff = ffn_dw_b.shape[1]
    m = nb * seq_len
    r = DFT_R
    hq = seq_len // r
    g = DFT_G
    kb = min(DFT_G, hq)
    tm = 512
    tf = 512

    f1s, f1rt, f1it, mf, mi = (jnp.asarray(a, F32).astype(BF16) for a in _dft_constants(seq_len))
    e_pad = hy_f_w2.shape[1]
    z = _positional_features(seq_len, e_pad)
    max_decay = math.log(DECAY_TARGET) / FAST_DECAY_PCT
    min_decay = math.log(DECAY_TARGET) / SLOW_DECAY_PCT
    deltas = jnp.abs(jnp.linspace(min_decay, max_decay, d_b, dtype=F32))
    deltas2 = jnp.concatenate([deltas, deltas])[None]

    x2d = x.reshape(m, d)
    row = lambda v: v.reshape(1, -1)
    for l in range(depth):
        w_in_l = w_in[l].astype(BF16)
        g1 = row(norm1_g[l])

        bias_full = jnp.repeat(sgu_b[l].T, d_a // N_HEADS_A, axis=1)
        ma = _in_proj_a(x2d, g1, w_in_l, row(sgu_norm_g[l]), sgu_w[l].astype(BF16),
                        bias_full, row(outnorm_a_g[l]), tm=512)

        x0, u = _in_proj_b(x2d, g1, w_in_l, hy_conv_w[l], row(hy_conv_b[l]),
                           col0=2 * d_a, d_b=d_b, seq_len=seq_len, tm=1024, tn=512)
        u4 = u.reshape(nb, hq, r, d_b)
        x04 = x0.reshape(nb, hq, r, d_b)

        w1p = jnp.pad(hy_f_w1[l], ((0, e_pad - hy_f_w1.shape[1]), (0, 0)))
        hw, l1 = _filter_mlp(z, w1p, row(hy_f_b1[l]), hy_f_w2[l], row(hy_f_b2[l]), hy_f_w3[l],
                             row(hy_f_b3[l]), row(hy_f_freq[l]), hy_f_wout[l], deltas2, tl=512)
        ak = _dft1(f1s, hw.reshape(1, hq, r, 2 * d_b), g=g, ct=512, name="filter_dft1")
        khat = _filter_spec(ak.reshape(1, 2, hq, r, 2 * d_b), mf, l1, kb=kb, ct=256)

        a = _dft1(f1s, u4, g=g, ct=512, name="u_dft1")
        dr, di = _spec_conv(a.reshape(nb, 2, hq, r, d_b), mf, mi, khat, kb=kb, ct=256)
        yb = _conv_out(f1rt, f1it, dr, di, u4, x04, row(hy_d_skip[l]), g=g, ct=512)

        x1, h2 = _out_proj(ma, yb.reshape(m, d_b), x2d, row(outnorm_b_g[l]),
                           w_out[l].astype(BF16), row(norm2_g[l]), tm=tm)

        x2d = _ffn(h2, x1, ffn_w_up[l].astype(BF16), ffn_dw_w[l], row(ffn_dw_b[l]),
                   ffn_w_down[l].astype(BF16), row(final_g), seq_len=seq_len, tm=1024, tf=tf,
                   final_norm=(l == depth - 1))
    return x2d.reshape(nb, seq_len, d)
```

```python
import functools
import math

import numpy as np
import jax
import jax.numpy as jnp
from jax import lax
from jax.experimental import pallas as pl
from jax.experimental.pallas import tpu as pltpu

EPS = 1e-6
CHUNK = 128
N_HEADS_A = 8
DECAY_TARGET = 1e-2
FAST_DECAY_PCT = 0.3
SLOW_DECAY_PCT = 1.5
N_BANDS = 16

LANE = 128
DFT_R = 128
DFT_G = 16
HALO = 16
VMEM_LIMIT = 56 * 1024 * 1024

F32 = jnp.float32
BF16 = jnp.bfloat16


def _gelu(x):
    return 0.5 * x * (1.0 + lax.erf(x * np.float32(math.sqrt(0.5))))


def _rms(x, g):
    return x * lax.rsqrt(jnp.mean(x * x, axis=-1, keepdims=True) + EPS) * g


def _params(*sem):
    return pltpu.CompilerParams(dimension_semantics=sem, vmem_limit_bytes=VMEM_LIMIT)


def _in_proj_a_kernel(x_ref, g1_ref, w_ref, gs_ref, ws_ref, bias_ref, go_ref, o_ref, h_ref, ya_ref):
    tm = x_ref.shape[0]
    d_a = o_ref.shape[1]
    hd = d_a // N_HEADS_A
    h = _rms(x_ref[...], g1_ref[...]).astype(BF16)
    h_ref[...] = h
    p = jnp.dot(h, w_ref[...], preferred_element_type=F32)
    gv = _gelu(p[:, d_a:])
    zv = _rms(gv, gs_ref[...]).astype(BF16)
    for c in range(tm // CHUNK):
        rows = slice(c * CHUNK, (c + 1) * CHUNK)
        for hh in range(N_HEADS_A):
            cols = slice(hh * hd, (hh + 1) * hd)
            s = jnp.dot(ws_ref[hh], zv[rows, cols], preferred_element_type=F32)
            ya_ref[rows, cols] = _gelu(p[rows, cols]) * (s + bias_ref[:, cols])
    o_ref[...] = _rms(ya_ref[...], go_ref[...]).astype(o_ref.dtype)


def _in_proj_a(x2d, g1, w_a, gs, ws, bias_full, go, *, tm):
    m, d = x2d.shape
    d_a = gs.shape[1]
    return pl.pallas_call(
        _in_proj_a_kernel,
        grid=(m // tm,),
        in_specs=[
            pl.BlockSpec((tm, d), lambda i: (i, 0)),
            pl.BlockSpec((1, d), lambda i: (0, 0)),
            pl.BlockSpec((d, 2 * d_a), lambda i: (0, 0)),
            pl.BlockSpec((1, d_a), lambda i: (0, 0)),
            pl.BlockSpec((N_HEADS_A, CHUNK, CHUNK), lambda i: (0, 0, 0)),
            pl.BlockSpec((CHUNK, d_a), lambda i: (0, 0)),
            pl.BlockSpec((1, d_a), lambda i: (0, 0)),
        ],
        out_specs=[pl.BlockSpec((tm, d_a), lambda i: (i, 0)), pl.BlockSpec((tm, d), lambda i: (i, 0))],
        out_shape=[jax.ShapeDtypeStruct((m, d_a), BF16), jax.ShapeDtypeStruct((m, d), BF16)],
        scratch_shapes=[pltpu.VMEM((tm, d_a), F32)],
        compiler_params=_params("parallel"),
        name="in_proj_a",
    )(x2d, g1, w_a, gs, ws, bias_full, go)


def _fill_ext(hext_ref, prev, main, nxt, i, seq_tiles):
    tm = main.shape[0]
    has_prev = (i % seq_tiles) != 0
    has_next = ((i + 1) % seq_tiles) != 0
    hext_ref[0:HALO, :] = jnp.where(has_prev, prev, jnp.zeros_like(prev))
    hext_ref[HALO:HALO + tm, :] = main
    hext_ref[HALO + tm:, :] = jnp.where(has_next, nxt, jnp.zeros_like(nxt))


def _conv3(gext, w_ref, b_ref, tm):
    n = gext.shape[0]
    prev = pltpu.roll(gext, 1, 0)[HALO:HALO + tm]
    nxt = pltpu.roll(gext, n - 1, 0)[HALO:HALO + tm]
    cur = gext[HALO:HALO + tm]
    return prev * w_ref[0:1, :] + cur * w_ref[1:2, :] + nxt * w_ref[2:3, :] + b_ref[...]


def _in_proj_b_kernel(hp_ref, h_ref, hn_ref, w0_ref, w1_ref, w2_ref,
                      cw0_ref, cw1_ref, cw2_ref, cb0_ref, cb1_ref, cb2_ref,
                      x0_ref, u_ref, hext_ref, *, seq_tiles):
    tm = h_ref.shape[0]
    i = pl.program_id(0)

    @pl.when(pl.program_id(1) == 0)
    def _():
        _fill_ext(hext_ref, hp_ref[...], h_ref[...], hn_ref[...], i, seq_tiles)

    hext = hext_ref[...]

    def branch(w_ref, cw_ref, cb_ref):
        gext = jnp.dot(hext, w_ref[...], preferred_element_type=F32)
        return _conv3(gext, cw_ref, cb_ref, tm)

    x0_ref[...] = branch(w0_ref, cw0_ref, cb0_ref)
    x1 = branch(w1_ref, cw1_ref, cb1_ref)
    v = branch(w2_ref, cw2_ref, cb2_ref)
    u_ref[...] = v * x1


def _halo_specs(tm, d, m):
    blocks_per_tile = tm // HALO
    last = m // HALO - 1
    return [
        pl.BlockSpec((HALO, d), lambda i, j: (jnp.maximum(i * blocks_per_tile - 1, 0), 0)),
        pl.BlockSpec((tm, d), lambda i, j: (i, 0)),
        pl.BlockSpec((HALO, d), lambda i, j: (jnp.minimum((i + 1) * blocks_per_tile, last), 0)),
    ]


def _in_proj_b(h, w_in, conv_w, conv_b, *, col0, d_b, seq_len, tm, tn):
    m, d = h.shape
    nj = d_b // tn
    w_specs = [pl.BlockSpec((d, tn), functools.partial(
        lambda i, j, off: (0, off + j), off=(col0 + g * d_b) // tn)) for g in range(3)]
    cw_specs = [pl.BlockSpec((3, tn), functools.partial(
        lambda i, j, off: (0, off + j), off=(g * d_b) // tn)) for g in range(3)]
    cb_specs = [pl.BlockSpec((1, tn), functools.partial(
        lambda i, j, off: (0, off + j), off=(g * d_b) // tn)) for g in range(3)]
    out_spec = pl.BlockSpec((tm, tn), lambda i, j: (i, j))
    return pl.pallas_call(
        functools.partial(_in_proj_b_kernel, seq_tiles=seq_len // tm),
        grid=(m // tm, nj),
        in_specs=_halo_specs(tm, d, m) + w_specs + cw_specs + cb_specs,
        out_specs=[out_spec, out_spec],
        out_shape=[jax.ShapeDtypeStruct((m, d_b), F32)] * 2,
        scratch_shapes=[pltpu.VMEM((tm + 2 * HALO, d), BF16)],
        compiler_params=_params("parallel", "arbitrary"),
        name="in_proj_b",
    )(h, h, h, w_in, w_in, w_in, conv_w, conv_w, conv_w, conv_b, conv_b, conv_b)


def _filter_feat_kernel(z_ref, w1_ref, b1_ref, w2_ref, b2_ref, w3_ref, b3_ref, fr_ref, h_ref):
    hi = lax.Precision.HIGHEST
    fr = fr_ref[...]
    h = z_ref[...]
    for w_ref, b_ref in ((w1_ref, b1_ref), (w2_ref, b2_ref), (w3_ref, b3_ref)):
        h = jnp.sin(fr * (jnp.dot(h, w_ref[...], precision=hi, preferred_element_type=F32) + b_ref[...]))
    h_ref[...] = h


def _filter_feat(z, w1, b1, w2, b2, w3, b3, fr, *, tl):
    seq_len, e = z.shape
    pack = LANE // e
    assert e * pack == LANE and seq_len % (pack * tl) == 0
    eye = jnp.eye(pack, dtype=F32)
    bd = lambda w: jnp.kron(eye, w)
    rep = lambda v: jnp.tile(v, (1, pack))
    full = lambda shape: pl.BlockSpec(shape, lambda i: (0, 0))
    rows = seq_len // pack
    h = pl.pallas_call(
        _filter_feat_kernel,
        grid=(rows // tl,),
        in_specs=[pl.BlockSpec((tl, LANE), lambda i: (i, 0)), full((LANE, LANE)), full((1, LANE)),
                  full((LANE, LANE)), full((1, LANE)), full((LANE, LANE)), full((1, LANE)),
                  full((1, LANE))],
        out_specs=pl.BlockSpec((tl, LANE), lambda i: (i, 0)),
        out_shape=jax.ShapeDtypeStruct((rows, LANE), F32),
        compiler_params=_params("parallel"),
        name="filter_feat",
    )(z.reshape(rows, LANE), bd(w1), rep(b1), bd(w2), rep(b2), bd(w3), rep(b3), rep(fr))
    return h.reshape(seq_len, e)


def _dot_bf16x3(a, b):
    a_hi = a.astype(BF16)
    b_hi = b.astype(BF16)
    a_lo = (a - a_hi.astype(F32)).astype(BF16)
    b_lo = (b - b_hi.astype(F32)).astype(BF16)
    dot = functools.partial(jnp.dot, preferred_element_type=F32)
    return dot(a_hi, b_hi) + dot(a_lo, b_hi) + dot(a_hi, b_lo)


def _dft1_kernel(w_ref, x_ref, o_ref, a_ref):
    g = x_ref.shape[2]
    xs = jnp.swapaxes(x_ref[0], 0, 1).astype(BF16)
    w = w_ref[...]
    for j in range(g):
        a_ref[j] = jnp.dot(w, xs[j], preferred_element_type=F32)
    o_ref[0] = jnp.swapaxes(a_ref[...], 0, 1).astype(o_ref.dtype)


def _dft1(w, x4, *, g, ct, name):
    nb, hq, r, c = x4.shape
    mo = w.shape[0]
    return pl.pallas_call(
        _dft1_kernel,
        grid=(nb, r // g, c // ct),
        in_specs=[pl.BlockSpec((mo, hq), lambda b, i, j: (0, 0)),
                  pl.BlockSpec((1, hq, g, ct), lambda b, i, j: (b, 0, i, j))],
        out_specs=pl.BlockSpec((1, mo, g, ct), lambda b, i, j: (b, 0, i, j)),
        out_shape=jax.ShapeDtypeStruct((nb, mo, r, c), BF16),
        scratch_shapes=[pltpu.VMEM((g, mo, ct), F32)],
        compiler_params=_params("parallel", "parallel", "parallel"),
        name=name,
    )(w, x4)


def _filter_dft1_kernel(w_ref, h3_ref, t_ref, wo_ref, dl_ref, o_ref, l1_ref, a_ref, *, d_b):
    hq, g, e = h3_ref.shape
    ct = wo_ref.shape[1]
    j = pl.program_id(0)
    gi = pl.program_id(1)
    h3 = jnp.swapaxes(h3_ref[...], 0, 1).reshape(g * hq, e)
    hw = _dot_bf16x3(h3, wo_ref[...])
    hw = hw * jnp.exp(-t_ref[...].reshape(g * hq, 1) * dl_ref[...])
    row = lax.broadcasted_iota(jnp.int32, hw.shape, 0)
    hw = jnp.where((row == 0) & (gi == 0) & (j * ct >= d_b), 0.0, hw)

    @pl.when(gi == 0)
    def _():
        l1_ref[...] = jnp.zeros_like(l1_ref)

    l1_ref[...] += jnp.sum(jnp.abs(hw), axis=0, keepdims=True)
    x3 = hw.reshape(g, hq, ct).astype(BF16)
    w = w_ref[...]
    for jj in range(g):
        a_ref[jj] = jnp.dot(w, x3[jj], preferred_element_type=F32)
    o_ref[0] = jnp.swapaxes(a_ref[...], 0, 1).astype(o_ref.dtype)


def _filter_dft1(w, h3, t_sw, wout, deltas2, *, d_b, g, ct):
    hq, r, e = h3.shape
    c2 = wout.shape[1]
    mo = w.shape[0]
    return pl.pallas_call(
        functools.partial(_filter_dft1_kernel, d_b=d_b),
        grid=(c2 // ct, r // g),
        in_specs=[pl.BlockSpec((mo, hq), lambda j, i: (0, 0)),
                  pl.BlockSpec((hq, g, e), lambda j, i: (0, i, 0)),
                  pl.BlockSpec((g, hq, 1), lambda j, i: (i, 0, 0)),
                  pl.BlockSpec((e, ct), lambda j, i: (0, j)),
                  pl.BlockSpec((1, ct), lambda j, i: (0, j))],
        out_specs=[pl.BlockSpec((1, mo, g, ct), lambda j, i: (0, 0, i, j)),
                   pl.BlockSpec((1, ct), lambda j, i: (0, j))],
        out_shape=[jax.ShapeDtypeStruct((1, mo, r, c2), BF16), jax.ShapeDtypeStruct((1, c2), F32)],
        scratch_shapes=[pltpu.VMEM((g, mo, ct), F32)],
        compiler_params=_params("parallel", "arbitrary"),
        name="filter_dft1",
    )(w, h3, t_sw, wout, deltas2)


def _filter_spec_kernel(af_ref, ab_ref, mf_ref, l1f_ref, l1b_ref, k_ref):
    kb = mf_ref.shape[0]
    r = mf_ref.shape[1] // 2
    ct = k_ref.shape[2]
    scale = 1.0 / (l1f_ref[...] + l1b_ref[...] + EPS)

    for kk in range(kb):
        mfk = mf_ref[kk]
        xf = jnp.dot(mfk, af_ref[0, :, kk].reshape(2 * r, ct), preferred_element_type=F32)
        xb = jnp.dot(mfk, ab_ref[0, :, kk].reshape(2 * r, ct), preferred_element_type=F32)
        k_ref[kk, 0:r, :] = (xf[:r] + xb[:r]) * scale
        k_ref[kk, r:, :] = (xf[r:] - xb[r:]) * scale


def _filter_spec(ak5, mf, l1, *, kb, ct):
    _, _, hq, r, c2 = ak5.shape
    c = c2 // 2
    nct = c // ct
    return pl.pallas_call(
        _filter_spec_kernel,
        grid=(hq // kb, nct),
        in_specs=[pl.BlockSpec((1, 2, kb, r, ct), lambda i, j: (0, 0, i, 0, j)),
                  pl.BlockSpec((1, 2, kb, r, ct), lambda i, j: (0, 0, i, 0, j + nct)),
                  pl.BlockSpec((kb, 2 * r, 2 * r), lambda i, j: (i, 0, 0)),
                  pl.BlockSpec((1, ct), lambda i, j: (0, j)),
                  pl.BlockSpec((1, ct), lambda i, j: (0, j + nct))],
        out_specs=pl.BlockSpec((kb, 2 * r, ct), lambda i, j: (i, 0, j)),
        out_shape=jax.ShapeDtypeStruct((hq, 2 * r, c), F32),
        compiler_params=_params("parallel", "parallel"),
        name="filter_spec",
    )(ak5, ak5, mf, l1, l1)


def _spec_conv_kernel(a_ref, mf_ref, mi_ref, k_ref, dr_ref, di_ref, d_scr):
    kb = mf_ref.shape[0]
    r = mf_ref.shape[1] // 2
    ct = k_ref.shape[2]

    for kk in range(kb):
        a = a_ref[0, :, kk].reshape(2 * r, ct)
        x = jnp.dot(mf_ref[kk], a, preferred_element_type=F32)
        xr, xi = x[:r], x[r:]
        kr, ki = k_ref[kk, 0:r, :], k_ref[kk, r:, :]
        y = jnp.concatenate([xr * kr - xi * ki, xr * ki + xi * kr], axis=0).astype(BF16)
        dd = jnp.dot(mi_ref[kk], y, preferred_element_type=F32)
        d_scr[0, kk] = dd[:r]
        d_scr[1, kk] = dd[r:]
    dr_ref[0] = jnp.swapaxes(d_scr[0], 0, 1).astype(dr_ref.dtype)
    di_ref[0] = jnp.swapaxes(d_scr[1], 0, 1).astype(di_ref.dtype)


def _spec_conv(a5, mf, mi, khat, *, kb, ct):
    nb, _, hq, r, c = a5.shape
    mspec = pl.BlockSpec((kb, 2 * r, 2 * r), lambda i, j, b: (i, 0, 0))
    ospec = pl.BlockSpec((1, r, kb, ct), lambda i, j, b: (b, 0, i, j))
    oshape = jax.ShapeDtypeStruct((nb, r, hq, c), BF16)
    return pl.pallas_call(
        _spec_conv_kernel,
        grid=(hq // kb, c // ct, nb),
        in_specs=[pl.BlockSpec((1, 2, kb, r, ct), lambda i, j, b: (b, 0, i, 0, j)), mspec, mspec,
                  pl.BlockSpec((kb, 2 * r, ct), lambda i, j, b: (i, 0, j))],
        out_specs=[ospec, ospec],
        out_shape=[oshape, oshape],
        scratch_shapes=[pltpu.VMEM((2, kb, r, ct), F32)],
        compiler_params=_params("parallel", "parallel", "arbitrary"),
        name="spec_conv",
    )(a5, mf, mi, khat)


def _conv_out_kernel(wr_ref, wi_ref, dr_ref, di_ref, u_ref, x0_ref, ds_ref, o_ref, y_scr):
    g = dr_ref.shape[1]
    wr, wi = wr_ref[...], wi_ref[...]
    for j in range(g):
        y_scr[j] = (jnp.dot(wr, dr_ref[0, j], preferred_element_type=F32)
                    + jnp.dot(wi, di_ref[0, j], preferred_element_type=F32))
    yconv = jnp.swapaxes(y_scr[...], 0, 1)
    o_ref[0] = x0_ref[0] * (yconv + u_ref[0] * ds_ref[...])


def _conv_out(wr, wi, dr, di, u4, x04, ds, *, g, ct):
    nb, r, hq, c = dr.shape
    dspec = pl.BlockSpec((1, g, hq, ct), lambda b, i, j: (b, i, 0, j))
    nspec = pl.BlockSpec((1, hq, g, ct), lambda b, i, j: (b, 0, i, j))
    wspec = pl.BlockSpec((hq, hq), lambda b, i, j: (0, 0))
    return pl.pallas_call(
        _conv_out_kernel,
        grid=(nb, r // g, c // ct),
        in_specs=[wspec, wspec, dspec, dspec, nspec, nspec,
                  pl.BlockSpec((1, ct), lambda b, i, j: (0, j))],
        out_specs=nspec,
        out_shape=jax.ShapeDtypeStruct((nb, hq, r, c), F32),
        scratch_shapes=[pltpu.VMEM((g, hq, ct), F32)],
        compiler_params=_params("parallel", "parallel", "parallel"),
        name="conv_out",
    )(wr, wi, dr, di, u4, x04, ds)


def _out_proj_kernel(ma_ref, yb_ref, x_ref, gb_ref, wa_ref, wb_ref, g2_ref, x1_ref, h2_ref):
    mb = _rms(yb_ref[...], gb_ref[...]).astype(BF16)
    x1 = (x_ref[...]
          + jnp.dot(ma_ref[...], wa_ref[...], preferred_element_type=F32)
          + jnp.dot(mb, wb_ref[...], preferred_element_type=F32))
    x1_ref[...] = x1
    h2_ref[...] = _rms(x1, g2_ref[...]).astype(h2_ref.dtype)


def _out_proj(ma, yb, x2d, gb, w_out, g2, *, tm):
    m, d = x2d.shape
    d_a = ma.shape[1]
    d_b = yb.shape[1]
    row = lambda w: pl.BlockSpec((tm, w), lambda i: (i, 0))
    return pl.pallas_call(
        _out_proj_kernel,
        grid=(m // tm,),
        in_specs=[row(d_a), row(d_b), row(d), pl.BlockSpec((1, d_b), lambda i: (0, 0)),
                  pl.BlockSpec((d_a, d), lambda i: (0, 0)),
                  pl.BlockSpec((d_b, d), lambda i: (d_a // d_b, 0)),
                  pl.BlockSpec((1, d), lambda i: (0, 0))],
        out_specs=[row(d), row(d)],
        out_shape=[jax.ShapeDtypeStruct((m, d), F32), jax.ShapeDtypeStruct((m, d), BF16)],
        compiler_params=_params("parallel"),
        name="out_proj",
    )(ma, yb, x2d, gb, w_out, w_out, g2)


def _ffn_kernel(hp_ref, h_ref, hn_ref, x1_hbm, wg_ref, wv_ref, cw_ref, cb_ref, wd_ref, gf_ref,
                o_ref, hext_ref, x1_sem, *, seq_tiles, final_norm, d_ff):
    tm = h_ref.shape[0]
    tf = wg_ref.shape[1]
    i = pl.program_id(0)
    f = pl.program_id(1)
    x1_copy = pltpu.make_async_copy(x1_hbm.at[pl.ds(i * tm, tm), :], o_ref, x1_sem)

    nf = pl.cdiv(d_ff, tf)
    assert nf >= 2
    dup = nf * tf - d_ff

    def hidden_tile(first, last):
        cols = slice(dup, tf) if last else slice(0, tf)
        gext = jnp.dot(hext_ref[...], wg_ref[:, cols], preferred_element_type=F32)
        g = _conv3(gext, cw_ref[:, cols], cb_ref[:, cols], tm)
        val = jnp.dot(h_ref[...], wv_ref[:, cols], preferred_element_type=F32)
        act = (_gelu(g) * val).astype(BF16)
        if first:
            x1_copy.wait()
        o_ref[...] += jnp.dot(act, wd_ref[cols, :], preferred_element_type=F32)

    @pl.when(f == 0)
    def _():
        x1_copy.start()
        _fill_ext(hext_ref, hp_ref[...], h_ref[...], hn_ref[...], i, seq_tiles)
        hidden_tile(True, False)

    @pl.when((f > 0) & (f < nf - 1))
    def _():
        hidden_tile(False, False)

    @pl.when(f == nf - 1)
    def _():
        hidden_tile(False, True)
        if final_norm:
            o_ref[...] = _rms(o_ref[...], gf_ref[...])


def _ragged_start(j, tf, d_ff, base=0):
    return (base // LANE + jnp.minimum(j * (tf // LANE), (d_ff - tf) // LANE)) * LANE


def _ffn(h2, x1, w_up, cw, cb, wd, gf, *, seq_len, tm, tf, final_norm):
    m, d = x1.shape
    d_ff = wd.shape[0]
    assert d_ff % LANE == 0 and tf % LANE == 0 and d_ff >= tf
    el = pl.Element
    start = functools.partial(_ragged_start, tf=tf, d_ff=d_ff)
    return pl.pallas_call(
        functools.partial(_ffn_kernel, seq_tiles=seq_len // tm, final_norm=final_norm, d_ff=d_ff),
        grid=(m // tm, pl.cdiv(d_ff, tf)),
        in_specs=_halo_specs(tm, d, m) + [
            pl.BlockSpec(memory_space=pl.ANY),
            pl.BlockSpec((el(d), el(tf)), lambda i, j: (0, start(j))),
            pl.BlockSpec((el(d), el(tf)), lambda i, j: (0, start(j, base=d_ff))),
            pl.BlockSpec((el(3), el(tf)), lambda i, j: (0, start(j))),
            pl.BlockSpec((el(1), el(tf)), lambda i, j: (0, start(j))),
            pl.BlockSpec((el(tf), el(d)), lambda i, j: (start(j), 0)),
            pl.BlockSpec((1, d), lambda i, j: (0, 0)),
        ],
        out_specs=pl.BlockSpec((tm, d), lambda i, j: (i, 0)),
        out_shape=jax.ShapeDtypeStruct((m, d), F32),
        scratch_shapes=[pltpu.VMEM((tm + 2 * HALO, d), BF16), pltpu.SemaphoreType.DMA(())],
        compiler_params=_params("parallel", "arbitrary"),
        name="ffn",
    )(h2, h2, h2, x1, w_up, w_up, cw, cb, wd, gf)


@functools.lru_cache(maxsize=None)
def _dft_constants(seq_len):
    n = 2 * seq_len
    r = DFT_R
    q = n // r
    hq = q // 2
    k1 = np.arange(hq, dtype=np.float64)[:, None]
    n1 = np.arange(hq, dtype=np.float64)[None, :]
    th1 = 2.0 * np.pi * n1 * (k1 + 0.5) / q
    f1r, f1i = np.cos(th1), -np.sin(th1)
    f1s = np.concatenate([f1r, f1i], axis=0)
    f1rt, f1it = (2.0 / n) * f1r.T, (2.0 / n) * f1i.T
    k2 = np.arange(r, dtype=np.float64)[None, :, None]
    n2 = np.arange(r, dtype=np.float64)[None, None, :]
    kk = np.arange(hq, dtype=np.float64)[:, None, None]
    th2 = 2.0 * np.pi * (n2 * k2 / r + n2 * (kk + 0.5) / n)
    gr, gi = np.cos(th2), -np.sin(th2)
    mf = np.concatenate([np.concatenate([gr, -gi], axis=2),
                         np.concatenate([gi, gr], axis=2)], axis=1)
    grt, git = gr.transpose(0, 2, 1), gi.transpose(0, 2, 1)
    mi = np.concatenate([np.concatenate([grt, git], axis=2),
                         np.concatenate([-git, grt], axis=2)], axis=1)
    return f1s, f1rt, f1it, mf, mi


def _positional_features(seq_len, e_pad):
    t = jnp.linspace(0.0, 1.0, seq_len, dtype=F32)[:, None]
    w = (2.0 * math.pi / seq_len) * jnp.arange(seq_len, dtype=F32)[:, None]
    f = jnp.linspace(1e-4, N_BANDS - 1, N_BANDS, dtype=F32)[None]
    z = jnp.concatenate([t, jnp.cos(f * w), -jnp.sin(f * w)], axis=-1)
    return jnp.pad(z, ((0, 0), (0, e_pad - z.shape[1])))


def kernel(x, norm1_g, w_in, sgu_norm_g, sgu_w, sgu_b, hy_conv_w, hy_conv_b, hy_f_w1, hy_f_b1,
           hy_f_w2, hy_f_b2, hy_f_w3, hy_f_b3, hy_f_freq, hy_f_wout, hy_d_skip, outnorm_a_g,
           outnorm_b_g, w_out, norm2_g, ffn_w_up, ffn_dw_w, ffn_dw_b, ffn_w_down, final_g):
    nb, seq_len, d = x.shape
    depth = w_in.shape[0]
    d_a = sgu_norm_g.shape[1]
    d_b = hy_d_skip.shape[1]
    d_ff = ffn_dw_b.shape[1]
    m = nb * seq_len
    r = DFT_R
    hq = seq_len // r
    g = DFT_G
    kb = min(DFT_G, hq)
    tm = 512
    tf = 512

    f1s, f1rt, f1it, mf, mi = (jnp.asarray(a, F32).astype(BF16) for a in _dft_constants(seq_len))
    e_pad = hy_f_w2.shape[1]
    z = _positional_features(seq_len, e_pad)
    max_decay = math.log(DECAY_TARGET) / FAST_DECAY_PCT
    min_decay = math.log(DECAY_TARGET) / SLOW_DECAY_PCT
    deltas = jnp.abs(jnp.linspace(min_decay, max_decay, d_b, dtype=F32))
    deltas2 = jnp.concatenate([deltas, deltas])[None]

    x2d = x.reshape(m, d)
    row = lambda v: v.reshape(1, -1)
    for l in range(depth):
        w_in_l = w_in[l].astype(BF16)
        g1 = row(norm1_g[l])

        bias_full = jnp.repeat(sgu_b[l].T, d_a // N_HEADS_A, axis=1)
        ma, h1 = _in_proj_a(x2d, g1, w_in_l, row(sgu_norm_g[l]), sgu_w[l].astype(BF16),
                            bias_full, row(outnorm_a_g[l]), tm=512)

        x0, u = _in_proj_b(h1, w_in_l, hy_conv_w[l], row(hy_conv_b[l]),
                           col0=2 * d_a, d_b=d_b, seq_len=seq_len, tm=1024, tn=512)
        u4 = u.reshape(nb, hq, r, d_b)
        x04 = x0.reshape(nb, hq, r, d_b)

        w1p = jnp.pad(hy_f_w1[l], ((0, e_pad - hy_f_w1.shape[1]), (0, 0)))
        h3 = _filter_feat(z, w1p, row(hy_f_b1[l]), hy_f_w2[l], row(hy_f_b2[l]), hy_f_w3[l],
                          row(hy_f_b3[l]), row(hy_f_freq[l]), tl=min(512, seq_len // 2))
        t_sw = z[:, 0].reshape(hq, r).T[:, :, None]
        ak, l1 = _filter_dft1(f1s, h3.reshape(hq, r, e_pad), t_sw, hy_f_wout[l], deltas2,
                              d_b=d_b, g=g, ct=512)
        khat = _filter_spec(ak.reshape(1, 2, hq, r, 2 * d_b), mf, l1, kb=kb, ct=256)

        a = _dft1(f1s, u4, g=g, ct=512, name="u_dft1")
        dr, di = _spec_conv(a.reshape(nb, 2, hq, r, d_b), mf, mi, khat, kb=kb, ct=256)
        yb = _conv_out(f1rt, f1it, dr, di, u4, x04, row(hy_d_skip[l]), g=g, ct=512)

        x1, h2 = _out_proj(ma, yb.reshape(m, d_b), x2d, row(outnorm_b_g[l]),
                           w_out[l].astype(BF16), row(norm2_g[l]), tm=tm)

        x2d = _ffn(h2, x1, ffn_w_up[l].astype(BF16), ffn_dw_w[l], row(ffn_dw_b[l]),
                   ffn_w_down[l].astype(BF16), row(final_g), seq_len=seq_len, tm=1024, tf=tf,
                   final_norm=(l == depth - 1))
    return x2d.reshape(nb, seq_len, d)
```

```python
import functools
import math

import numpy as np
import jax
import jax.numpy as jnp
from jax import lax
from jax.experimental import pallas as pl
from jax.experimental.pallas import tpu as pltpu

EPS = 1e-6
CHUNK = 128
N_HEADS_A = 8
DECAY_TARGET = 1e-2
FAST_DECAY_PCT = 0.3
SLOW_DECAY_PCT = 1.5
N_BANDS = 16

LANE = 128
DFT_R = 128
DFT_G = 16
HALO = 16
VMEM_LIMIT = 56 * 1024 * 1024

F32 = jnp.float32
BF16 = jnp.bfloat16


def _gelu(x):
    return 0.5 * x * (1.0 + lax.erf(x * np.float32(math.sqrt(0.5))))


def _rms(x, g):
    return x * lax.rsqrt(jnp.mean(x * x, axis=-1, keepdims=True) + EPS) * g


def _params(*sem):
    return pltpu.CompilerParams(dimension_semantics=sem, vmem_limit_bytes=VMEM_LIMIT)


def _in_proj_a_kernel(x_ref, g1_ref, w_ref, gs_ref, ws_ref, bias_ref, go_ref, o_ref, h_ref,
                      ya_ref, wbf_ref):
    tm = x_ref.shape[0]
    d_a = o_ref.shape[1]
    hd = d_a // N_HEADS_A

    @pl.when(pl.program_id(0) == 0)
    def _():
        wbf_ref[...] = w_ref[...].astype(BF16)

    h = _rms(x_ref[...], g1_ref[...]).astype(BF16)
    h_ref[...] = h
    p = jnp.dot(h, wbf_ref[...], preferred_element_type=F32)
    gv = _gelu(p[:, d_a:])
    zv = _rms(gv, gs_ref[...]).astype(BF16)
    for c in range(tm // CHUNK):
        rows = slice(c * CHUNK, (c + 1) * CHUNK)
        for hh in range(N_HEADS_A):
            cols = slice(hh * hd, (hh + 1) * hd)
            s = jnp.dot(ws_ref[hh], zv[rows, cols], preferred_element_type=F32)
            ya_ref[rows, cols] = _gelu(p[rows, cols]) * (s + bias_ref[:, cols])
    o_ref[...] = _rms(ya_ref[...], go_ref[...]).astype(o_ref.dtype)


def _in_proj_a(x2d, g1, w_a, gs, ws, bias_full, go, *, tm):
    m, d = x2d.shape
    d_a = gs.shape[1]
    return pl.pallas_call(
        _in_proj_a_kernel,
        grid=(m // tm,),
        in_specs=[
            pl.BlockSpec((tm, d), lambda i: (i, 0)),
            pl.BlockSpec((1, d), lambda i: (0, 0)),
            pl.BlockSpec((d, 2 * d_a), lambda i: (0, 0), pipeline_mode=pl.Buffered(1)),
            pl.BlockSpec((1, d_a), lambda i: (0, 0)),
            pl.BlockSpec((N_HEADS_A, CHUNK, CHUNK), lambda i: (0, 0, 0)),
            pl.BlockSpec((CHUNK, d_a), lambda i: (0, 0)),
            pl.BlockSpec((1, d_a), lambda i: (0, 0)),
        ],
        out_specs=[pl.BlockSpec((tm, d_a), lambda i: (i, 0)), pl.BlockSpec((tm, d), lambda i: (i, 0))],
        out_shape=[jax.ShapeDtypeStruct((m, d_a), BF16), jax.ShapeDtypeStruct((m, d), BF16)],
        scratch_shapes=[pltpu.VMEM((tm, d_a), F32), pltpu.VMEM((d, 2 * d_a), BF16)],
        compiler_params=_params("arbitrary"),
        name="in_proj_a",
    )(x2d, g1, w_a, gs, ws, bias_full, go)


def _fill_ext(hext_ref, prev, main, nxt, i, seq_tiles):
    tm = main.shape[0]
    has_prev = (i % seq_tiles) != 0
    has_next = ((i + 1) % seq_tiles) != 0
    hext_ref[0:HALO, :] = jnp.where(has_prev, prev, jnp.zeros_like(prev))
    hext_ref[HALO:HALO + tm, :] = main
    hext_ref[HALO + tm:, :] = jnp.where(has_next, nxt, jnp.zeros_like(nxt))


def _conv3(gext, w_ref, b_ref, tm):
    n = gext.shape[0]
    prev = pltpu.roll(gext, 1, 0)[HALO:HALO + tm]
    nxt = pltpu.roll(gext, n - 1, 0)[HALO:HALO + tm]
    cur = gext[HALO:HALO + tm]
    return prev * w_ref[0:1, :] + cur * w_ref[1:2, :] + nxt * w_ref[2:3, :] + b_ref[...]


def _in_proj_b_kernel(hp_ref, h_ref, hn_ref, w0_ref, w1_ref, w2_ref,
                      cw0_ref, cw1_ref, cw2_ref, cb0_ref, cb1_ref, cb2_ref,
                      x0_ref, u_ref, hext_ref, *, seq_tiles):
    tm = h_ref.shape[0]
    i = pl.program_id(0)

    @pl.when(pl.program_id(1) == 0)
    def _():
        _fill_ext(hext_ref, hp_ref[...], h_ref[...], hn_ref[...], i, seq_tiles)

    hext = hext_ref[...]

    def branch(w_ref, cw_ref, cb_ref):
        gext = jnp.dot(hext, w_ref[...].astype(BF16), preferred_element_type=F32)
        return _conv3(gext, cw_ref, cb_ref, tm)

    x0_ref[...] = branch(w0_ref, cw0_ref, cb0_ref)
    x1 = branch(w1_ref, cw1_ref, cb1_ref)
    v = branch(w2_ref, cw2_ref, cb2_ref)
    u_ref[...] = v * x1


def _halo_specs(tm, d, m):
    blocks_per_tile = tm // HALO
    last = m // HALO - 1
    return [
        pl.BlockSpec((HALO, d), lambda i, j: (jnp.maximum(i * blocks_per_tile - 1, 0), 0)),
        pl.BlockSpec((tm, d), lambda i, j: (i, 0)),
        pl.BlockSpec((HALO, d), lambda i, j: (jnp.minimum((i + 1) * blocks_per_tile, last), 0)),
    ]


def _in_proj_b(h, w_in, conv_w, conv_b, *, col0, d_b, seq_len, tm, tn):
    m, d = h.shape
    nj = d_b // tn
    w_specs = [pl.BlockSpec((d, tn), functools.partial(
        lambda i, j, off: (0, off + j), off=(col0 + g * d_b) // tn)) for g in range(3)]
    cw_specs = [pl.BlockSpec((3, tn), functools.partial(
        lambda i, j, off: (0, off + j), off=(g * d_b) // tn)) for g in range(3)]
    cb_specs = [pl.BlockSpec((1, tn), functools.partial(
        lambda i, j, off: (0, off + j), off=(g * d_b) // tn)) for g in range(3)]
    out_spec = pl.BlockSpec((tm, tn), lambda i, j: (i, j))
    return pl.pallas_call(
        functools.partial(_in_proj_b_kernel, seq_tiles=seq_len // tm),
        grid=(m // tm, nj),
        in_specs=_halo_specs(tm, d, m) + w_specs + cw_specs + cb_specs,
        out_specs=[out_spec, out_spec],
        out_shape=[jax.ShapeDtypeStruct((m, d_b), F32)] * 2,
        scratch_shapes=[pltpu.VMEM((tm + 2 * HALO, d), BF16)],
        compiler_params=_params("parallel", "arbitrary"),
        name="in_proj_b",
    )(h, h, h, w_in, w_in, w_in, conv_w, conv_w, conv_w, conv_b, conv_b, conv_b)


def _filter_feat_kernel(z_ref, w1_ref, b1_ref, w2_ref, b2_ref, w3_ref, b3_ref, fr_ref, h_ref):
    hi = lax.Precision.HIGHEST
    fr = fr_ref[...]
    h = z_ref[...]
    for w_ref, b_ref in ((w1_ref, b1_ref), (w2_ref, b2_ref), (w3_ref, b3_ref)):
        h = jnp.sin(fr * (jnp.dot(h, w_ref[...], precision=hi, preferred_element_type=F32) + b_ref[...]))
    h_ref[...] = h


def _filter_feat(z, w1, b1, w2, b2, w3, b3, fr, *, tl):
    seq_len, e = z.shape
    pack = LANE // e
    assert e * pack == LANE and seq_len % (pack * tl) == 0
    eye = jnp.eye(pack, dtype=F32)
    bd = lambda w: jnp.kron(eye, w)
    rep = lambda v: jnp.tile(v, (1, pack))
    full = lambda shape: pl.BlockSpec(shape, lambda i: (0, 0))
    rows = seq_len // pack
    h = pl.pallas_call(
        _filter_feat_kernel,
        grid=(rows // tl,),
        in_specs=[pl.BlockSpec((tl, LANE), lambda i: (i, 0)), full((LANE, LANE)), full((1, LANE)),
                  full((LANE, LANE)), full((1, LANE)), full((LANE, LANE)), full((1, LANE)),
                  full((1, LANE))],
        out_specs=pl.BlockSpec((tl, LANE), lambda i: (i, 0)),
        out_shape=jax.ShapeDtypeStruct((rows, LANE), F32),
        compiler_params=_params("parallel"),
        name="filter_feat",
    )(z.reshape(rows, LANE), bd(w1), rep(b1), bd(w2), rep(b2), bd(w3), rep(b3), rep(fr))
    return h.reshape(seq_len, e)


def _dot_bf16x3(a, b):
    a_hi = a.astype(BF16)
    b_hi = b.astype(BF16)
    a_lo = (a - a_hi.astype(F32)).astype(BF16)
    b_lo = (b - b_hi.astype(F32)).astype(BF16)
    dot = functools.partial(jnp.dot, preferred_element_type=F32)
    return dot(a_hi, b_hi) + dot(a_lo, b_hi) + dot(a_hi, b_lo)


def _dft1_kernel(w_ref, x_ref, o_ref, a_ref):
    g = x_ref.shape[2]
    xs = jnp.swapaxes(x_ref[0], 0, 1).astype(BF16)
    w = w_ref[...]
    for j in range(g):
        a_ref[j] = jnp.dot(w, xs[j], preferred_element_type=F32)
    o_ref[0] = jnp.swapaxes(a_ref[...], 0, 1).astype(o_ref.dtype)


def _dft1(w, x4, *, g, ct, name):
    nb, hq, r, c = x4.shape
    mo = w.shape[0]
    return pl.pallas_call(
        _dft1_kernel,
        grid=(nb, r // g, c // ct),
        in_specs=[pl.BlockSpec((mo, hq), lambda b, i, j: (0, 0)),
                  pl.BlockSpec((1, hq, g, ct), lambda b, i, j: (b, 0, i, j))],
        out_specs=pl.BlockSpec((1, mo, g, ct), lambda b, i, j: (b, 0, i, j)),
        out_shape=jax.ShapeDtypeStruct((nb, mo, r, c), BF16),
        scratch_shapes=[pltpu.VMEM((g, mo, ct), F32)],
        compiler_params=_params("parallel", "parallel", "parallel"),
        name=name,
    )(w, x4)


def _filter_dft1_kernel(w_ref, h3_ref, t_ref, wo_ref, dl_ref, o_ref, l1_ref, a_ref, *, d_b):
    hq, g, e = h3_ref.shape
    ct = wo_ref.shape[1]
    j = pl.program_id(0)
    gi = pl.program_id(1)
    h3 = jnp.swapaxes(h3_ref[...], 0, 1).reshape(g * hq, e)
    hw = _dot_bf16x3(h3, wo_ref[...])
    hw = hw * jnp.exp(-t_ref[...].reshape(g * hq, 1) * dl_ref[...])
    row = lax.broadcasted_iota(jnp.int32, hw.shape, 0)
    hw = jnp.where((row == 0) & (gi == 0) & (j * ct >= d_b), 0.0, hw)

    @pl.when(gi == 0)
    def _():
        l1_ref[...] = jnp.zeros_like(l1_ref)

    l1_ref[...] += jnp.sum(jnp.abs(hw), axis=0, keepdims=True)
    x3 = hw.reshape(g, hq, ct).astype(BF16)
    w = w_ref[...]
    for jj in range(g):
        a_ref[jj] = jnp.dot(w, x3[jj], preferred_element_type=F32)
    o_ref[0] = jnp.swapaxes(a_ref[...], 0, 1).astype(o_ref.dtype)


def _filter_dft1(w, h3, t_sw, wout, deltas2, *, d_b, g, ct):
    hq, r, e = h3.shape
    c2 = wout.shape[1]
    mo = w.shape[0]
    return pl.pallas_call(
        functools.partial(_filter_dft1_kernel, d_b=d_b),
        grid=(c2 // ct, r // g),
        in_specs=[pl.BlockSpec((mo, hq), lambda j, i: (0, 0)),
                  pl.BlockSpec((hq, g, e), lambda j, i: (0, i, 0)),
                  pl.BlockSpec((g, hq, 1), lambda j, i: (i, 0, 0)),
                  pl.BlockSpec((e, ct), lambda j, i: (0, j)),
                  pl.BlockSpec((1, ct), lambda j, i: (0, j))],
        out_specs=[pl.BlockSpec((1, mo, g, ct), lambda j, i: (0, 0, i, j)),
                   pl.BlockSpec((1, ct), lambda j, i: (0, j))],
        out_shape=[jax.ShapeDtypeStruct((1, mo, r, c2), BF16), jax.ShapeDtypeStruct((1, c2), F32)],
        scratch_shapes=[pltpu.VMEM((g, mo, ct), F32)],
        compiler_params=_params("parallel", "arbitrary"),
        name="filter_dft1",
    )(w, h3, t_sw, wout, deltas2)


def _filter_spec_kernel(af_ref, ab_ref, mf_ref, l1f_ref, l1b_ref, k_ref):
    kb = mf_ref.shape[0]
    r = mf_ref.shape[1] // 2
    ct = k_ref.shape[2]
    scale = 1.0 / (l1f_ref[...] + l1b_ref[...] + EPS)

    for kk in range(kb):
        mfk = mf_ref[kk]
        xf = jnp.dot(mfk, af_ref[0, :, kk].reshape(2 * r, ct), preferred_element_type=F32)
        xb = jnp.dot(mfk, ab_ref[0, :, kk].reshape(2 * r, ct), preferred_element_type=F32)
        k_ref[kk, 0:r, :] = (xf[:r] + xb[:r]) * scale
        k_ref[kk, r:, :] = (xf[r:] - xb[r:]) * scale


def _filter_spec(ak5, mf, l1, *, kb, ct):
    _, _, hq, r, c2 = ak5.shape
    c = c2 // 2
    nct = c // ct
    return pl.pallas_call(
        _filter_spec_kernel,
        grid=(hq // kb, nct),
        in_specs=[pl.BlockSpec((1, 2, kb, r, ct), lambda i, j: (0, 0, i, 0, j)),
                  pl.BlockSpec((1, 2, kb, r, ct), lambda i, j: (0, 0, i, 0, j + nct)),
                  pl.BlockSpec((kb, 2 * r, 2 * r), lambda i, j: (i, 0, 0)),
                  pl.BlockSpec((1, ct), lambda i, j: (0, j)),
                  pl.BlockSpec((1, ct), lambda i, j: (0, j + nct))],
        out_specs=pl.BlockSpec((kb, 2 * r, ct), lambda i, j: (i, 0, j)),
        out_shape=jax.ShapeDtypeStruct((hq, 2 * r, c), F32),
        compiler_params=_params("parallel", "parallel"),
        name="filter_spec",
    )(ak5, ak5, mf, l1, l1)


def _spec_conv_kernel(a_ref, mf_ref, mi_ref, k_ref, dr_ref, di_ref, d_scr):
    kb = mf_ref.shape[0]
    r = mf_ref.shape[1] // 2
    ct = k_ref.shape[2]

    for kk in range(kb):
        a = a_ref[0, :, kk].reshape(2 * r, ct)
        x = jnp.dot(mf_ref[kk], a, preferred_element_type=F32)
        xr, xi = x[:r], x[r:]
        kr, ki = k_ref[kk, 0:r, :], k_ref[kk, r:, :]
        y = jnp.concatenate([xr * kr - xi * ki, xr * ki + xi * kr], axis=0).astype(BF16)
        dd = jnp.dot(mi_ref[kk], y, preferred_element_type=F32)
        d_scr[0, kk] = dd[:r]
        d_scr[1, kk] = dd[r:]
    dr_ref[0] = jnp.swapaxes(d_scr[0], 0, 1).astype(dr_ref.dtype)
    di_ref[0] = jnp.swapaxes(d_scr[1], 0, 1).astype(di_ref.dtype)


def _spec_conv(a5, mf, mi, khat, *, kb, ct):
    nb, _, hq, r, c = a5.shape
    mspec = pl.BlockSpec((kb, 2 * r, 2 * r), lambda i, j, b: (i, 0, 0))
    ospec = pl.BlockSpec((1, r, kb, ct), lambda i, j, b: (b, 0, i, j))
    oshape = jax.ShapeDtypeStruct((nb, r, hq, c), BF16)
    return pl.pallas_call(
        _spec_conv_kernel,
        grid=(hq // kb, c // ct, nb),
        in_specs=[pl.BlockSpec((1, 2, kb, r, ct), lambda i, j, b: (b, 0, i, 0, j)), mspec, mspec,
                  pl.BlockSpec((kb, 2 * r, ct), lambda i, j, b: (i, 0, j))],
        out_specs=[ospec, ospec],
        out_shape=[oshape, oshape],
        scratch_shapes=[pltpu.VMEM((2, kb, r, ct), F32)],
        compiler_params=_params("parallel", "parallel", "arbitrary"),
        name="spec_conv",
    )(a5, mf, mi, khat)


def _conv_out_kernel(wr_ref, wi_ref, dr_ref, di_ref, u_ref, x0_ref, ds_ref, o_ref, y_scr):
    g = dr_ref.shape[1]
    wr, wi = wr_ref[...], wi_ref[...]
    for j in range(g):
        y_scr[j] = (jnp.dot(wr, dr_ref[0, j], preferred_element_type=F32)
                    + jnp.dot(wi, di_ref[0, j], preferred_element_type=F32))
    yconv = jnp.swapaxes(y_scr[...], 0, 1)
    o_ref[0] = x0_ref[0] * (yconv + u_ref[0] * ds_ref[...])


def _conv_out(wr, wi, dr, di, u4, x04, ds, *, g, ct):
    nb, r, hq, c = dr.shape
    dspec = pl.BlockSpec((1, g, hq, ct), lambda b, i, j: (b, i, 0, j))
    nspec = pl.BlockSpec((1, hq, g, ct), lambda b, i, j: (b, 0, i, j))
    wspec = pl.BlockSpec((hq, hq), lambda b, i, j: (0, 0))
    return pl.pallas_call(
        _conv_out_kernel,
        grid=(nb, r // g, c // ct),
        in_specs=[wspec, wspec, dspec, dspec, nspec, nspec,
                  pl.BlockSpec((1, ct), lambda b, i, j: (0, j))],
        out_specs=nspec,
        out_shape=jax.ShapeDtypeStruct((nb, hq, r, c), F32),
        scratch_shapes=[pltpu.VMEM((g, hq, ct), F32)],
        compiler_params=_params("parallel", "parallel", "parallel"),
        name="conv_out",
    )(wr, wi, dr, di, u4, x04, ds)


def _out_proj_kernel(ma_ref, yb_ref, x_ref, gb_ref, wa_ref, wb_ref, g2_ref, x1_ref, h2_ref,
                     wabf_ref, wbbf_ref):
    @pl.when(pl.program_id(0) == 0)
    def _():
        wabf_ref[...] = wa_ref[...].astype(BF16)
        wbbf_ref[...] = wb_ref[...].astype(BF16)

    mb = _rms(yb_ref[...], gb_ref[...]).astype(BF16)
    x1 = (x_ref[...]
          + jnp.dot(ma_ref[...], wabf_ref[...], preferred_element_type=F32)
          + jnp.dot(mb, wbbf_ref[...], preferred_element_type=F32))
    x1_ref[...] = x1
    h2_ref[...] = _rms(x1, g2_ref[...]).astype(h2_ref.dtype)


def _out_proj(ma, yb, x2d, gb, w_out, g2, *, tm):
    m, d = x2d.shape
    d_a = ma.shape[1]
    d_b = yb.shape[1]
    row = lambda w: pl.BlockSpec((tm, w), lambda i: (i, 0))
    return pl.pallas_call(
        _out_proj_kernel,
        grid=(m // tm,),
        in_specs=[row(d_a), row(d_b), row(d), pl.BlockSpec((1, d_b), lambda i: (0, 0)),
                  pl.BlockSpec((d_a, d), lambda i: (0, 0), pipeline_mode=pl.Buffered(1)),
                  pl.BlockSpec((d_b, d), lambda i: (d_a // d_b, 0), pipeline_mode=pl.Buffered(1)),
                  pl.BlockSpec((1, d), lambda i: (0, 0))],
        out_specs=[row(d), row(d)],
        out_shape=[jax.ShapeDtypeStruct((m, d), F32), jax.ShapeDtypeStruct((m, d), BF16)],
        scratch_shapes=[pltpu.VMEM((d_a, d), BF16), pltpu.VMEM((d_b, d), BF16)],
        compiler_params=_params("arbitrary"),
        name="out_proj",
    )(ma, yb, x2d, gb, w_out, w_out, g2)


def _ffn_kernel(hp_ref, h_ref, hn_ref, x1_hbm, wg_ref, wv_ref, cw_ref, cb_ref, wd_ref, gf_ref,
                o_ref, hext_ref, x1_sem, *, seq_tiles, final_norm, d_ff):
    tm = h_ref.shape[0]
    tf = wg_ref.shape[1]
    i = pl.program_id(0)
    f = pl.program_id(1)
    x1_copy = pltpu.make_async_copy(x1_hbm.at[pl.ds(i * tm, tm), :], o_ref, x1_sem)

    nf = pl.cdiv(d_ff, tf)
    assert nf >= 2
    dup = nf * tf - d_ff

    def hidden_tile(first, last):
        cols = slice(dup, tf) if last else slice(0, tf)
        gext = jnp.dot(hext_ref[...], wg_ref[:, cols].astype(BF16), preferred_element_type=F32)
        g = _conv3(gext, cw_ref[:, cols], cb_ref[:, cols], tm)
        val = jnp.dot(h_ref[...], wv_ref[:, cols].astype(BF16), preferred_element_type=F32)
        act = (_gelu(g) * val).astype(BF16)
        if first:
            x1_copy.wait()
        o_ref[...] += jnp.dot(act, wd_ref[cols, :].astype(BF16), preferred_element_type=F32)

    @pl.when(f == 0)
    def _():
        x1_copy.start()
        _fill_ext(hext_ref, hp_ref[...], h_ref[...], hn_ref[...], i, seq_tiles)
        hidden_tile(True, False)

    @pl.when((f > 0) & (f < nf - 1))
    def _():
        hidden_tile(False, False)

    @pl.when(f == nf - 1)
    def _():
        hidden_tile(False, True)
        if final_norm:
            o_ref[...] = _rms(o_ref[...], gf_ref[...])


def _ragged_start(j, tf, d_ff, base=0):
    return (base // LANE + jnp.minimum(j * (tf // LANE), (d_ff - tf) // LANE)) * LANE


def _ffn(h2, x1, w_up, cw, cb, wd, gf, *, seq_len, tm, tf, final_norm):
    m, d = x1.shape
    d_ff = wd.shape[0]
    assert d_ff % LANE == 0 and tf % LANE == 0 and d_ff >= tf
    el = pl.Element
    start = functools.partial(_ragged_start, tf=tf, d_ff=d_ff)
    return pl.pallas_call(
        functools.partial(_ffn_kernel, seq_tiles=seq_len // tm, final_norm=final_norm, d_ff=d_ff),
        grid=(m // tm, pl.cdiv(d_ff, tf)),
        in_specs=_halo_specs(tm, d, m) + [
            pl.BlockSpec(memory_space=pl.ANY),
            pl.BlockSpec((el(d), el(tf)), lambda i, j: (0, start(j))),
            pl.BlockSpec((el(d), el(tf)), lambda i, j: (0, start(j, base=d_ff))),
            pl.BlockSpec((el(3), el(tf)), lambda i, j: (0, start(j))),
            pl.BlockSpec((el(1), el(tf)), lambda i, j: (0, start(j))),
            pl.BlockSpec((el(tf), el(d)), lambda i, j: (start(j), 0)),
            pl.BlockSpec((1, d), lambda i, j: (0, 0)),
        ],
        out_specs=pl.BlockSpec((tm, d), lambda i, j: (i, 0)),
        out_shape=jax.ShapeDtypeStruct((m, d), F32),
        scratch_shapes=[pltpu.VMEM((tm + 2 * HALO, d), BF16), pltpu.SemaphoreType.DMA(())],
        compiler_params=_params("parallel", "arbitrary"),
        name="ffn",
    )(h2, h2, h2, x1, w_up, w_up, cw, cb, wd, gf)


@functools.lru_cache(maxsize=None)
def _dft_constants(seq_len):
    n = 2 * seq_len
    r = DFT_R
    q = n // r
    hq = q // 2
    k1 = np.arange(hq, dtype=np.float64)[:, None]
    n1 = np.arange(hq, dtype=np.float64)[None, :]
    th1 = 2.0 * np.pi * n1 * (k1 + 0.5) / q
    f1r, f1i = np.cos(th1), -np.sin(th1)
    f1s = np.concatenate([f1r, f1i], axis=0)
    f1rt, f1it = (2.0 / n) * f1r.T, (2.0 / n) * f1i.T
    k2 = np.arange(r, dtype=np.float64)[None, :, None]
    n2 = np.arange(r, dtype=np.float64)[None, None, :]
    kk = np.arange(hq, dtype=np.float64)[:, None, None]
    th2 = 2.0 * np.pi * (n2 * k2 / r + n2 * (kk + 0.5) / n)
    gr, gi = np.cos(th2), -np.sin(th2)
    mf = np.concatenate([np.concatenate([gr, -gi], axis=2),
                         np.concatenate([gi, gr], axis=2)], axis=1)
    grt, git = gr.transpose(0, 2, 1), gi.transpose(0, 2, 1)
    mi = np.concatenate([np.concatenate([grt, git], axis=2),
                         np.concatenate([-git, grt], axis=2)], axis=1)
    return f1s, f1rt, f1it, mf, mi


def _positional_features(seq_len, e_pad):
    t = jnp.linspace(0.0, 1.0, seq_len, dtype=F32)[:, None]
    w = (2.0 * math.pi / seq_len) * jnp.arange(seq_len, dtype=F32)[:, None]
    f = jnp.linspace(1e-4, N_BANDS - 1, N_BANDS, dtype=F32)[None]
    z = jnp.concatenate([t, jnp.cos(f * w), -jnp.sin(f * w)], axis=-1)
    return jnp.pad(z, ((0, 0), (0, e_pad - z.shape[1])))


def kernel(x, norm1_g, w_in, sgu_norm_g, sgu_w, sgu_b, hy_conv_w, hy_conv_b, hy_f_w1, hy_f_b1,
           hy_f_w2, hy_f_b2, hy_f_w3, hy_f_b3, hy_f_freq, hy_f_wout, hy_d_skip, outnorm_a_g,
           outnorm_b_g, w_out, norm2_g, ffn_w_up, ffn_dw_w, ffn_dw_b, ffn_w_down, final_g):
    nb, seq_len, d = x.shape
    depth = w_in.shape[0]
    d_a = sgu_norm_g.shape[1]
    d_b = hy_d_skip.shape[1]
    d_ff = ffn_dw_b.shape[1]
    m = nb * seq_len
    r = DFT_R
    hq = seq_len // r
    g = DFT_G
    kb = min(DFT_G, hq)
    tm = 512
    tf = 512

    f1s, f1rt, f1it, mf, mi = (jnp.asarray(a, F32).astype(BF16) for a in _dft_constants(seq_len))
    e_pad = hy_f_w2.shape[1]
    z = _positional_features(seq_len, e_pad)
    max_decay = math.log(DECAY_TARGET) / FAST_DECAY_PCT
    min_decay = math.log(DECAY_TARGET) / SLOW_DECAY_PCT
    deltas = jnp.abs(jnp.linspace(min_decay, max_decay, d_b, dtype=F32))
    deltas2 = jnp.concatenate([deltas, deltas])[None]

    x2d = x.reshape(m, d)
    row = lambda v: v.reshape(1, -1)
    for l in range(depth):
        w_in_l = w_in[l]
        g1 = row(norm1_g[l])

        bias_full = jnp.repeat(sgu_b[l].T, d_a // N_HEADS_A, axis=1)
        ma, h1 = _in_proj_a(x2d, g1, w_in_l, row(sgu_norm_g[l]), sgu_w[l].astype(BF16),
                            bias_full, row(outnorm_a_g[l]), tm=512)

        x0, u = _in_proj_b(h1, w_in_l, hy_conv_w[l], row(hy_conv_b[l]),
                           col0=2 * d_a, d_b=d_b, seq_len=seq_len, tm=1024, tn=512)
        u4 = u.reshape(nb, hq, r, d_b)
        x04 = x0.reshape(nb, hq, r, d_b)

        w1p = jnp.pad(hy_f_w1[l], ((0, e_pad - hy_f_w1.shape[1]), (0, 0)))
        h3 = _filter_feat(z, w1p, row(hy_f_b1[l]), hy_f_w2[l], row(hy_f_b2[l]), hy_f_w3[l],
                          row(hy_f_b3[l]), row(hy_f_freq[l]), tl=min(512, seq_len // 2))
        t_sw = z[:, 0].reshape(hq, r).T[:, :, None]
        ak, l1 = _filter_dft1(f1s, h3.reshape(hq, r, e_pad), t_sw, hy_f_wout[l], deltas2,
                              d_b=d_b, g=g, ct=512)
        khat = _filter_spec(ak.reshape(1, 2, hq, r, 2 * d_b), mf, l1, kb=kb, ct=256)

        a = _dft1(f1s, u4, g=g, ct=512, name="u_dft1")
        dr, di = _spec_conv(a.reshape(nb, 2, hq, r, d_b), mf, mi, khat, kb=kb, ct=256)
        yb = _conv_out(f1rt, f1it, dr, di, u4, x04, row(hy_d_skip[l]), g=g, ct=512)

        x1, h2 = _out_proj(ma, yb.reshape(m, d_b), x2d, row(outnorm_b_g[l]),
                           w_out[l], row(norm2_g[l]), tm=tm)

        x2d = _ffn(h2, x1, ffn_w_up[l], ffn_dw_w[l], row(ffn_dw_b[l]),
                   ffn_w_down[l], row(final_g), seq_len=seq_len, tm=1024, tf=tf,
                   final_norm=(l == depth - 1))
    return x2d.reshape(nb, seq_len, d)
```

```python
import functools
import math

import numpy as np
import jax
import jax.numpy as jnp
from jax import lax
from jax.experimental import pallas as pl
from jax.experimental.pallas import tpu as pltpu

EPS = 1e-6
CHUNK = 128
N_HEADS_A = 8
DECAY_TARGET = 1e-2
FAST_DECAY_PCT = 0.3
SLOW_DECAY_PCT = 1.5
N_BANDS = 16

LANE = 128
DFT_R = 128
DFT_G = 16
HALO = 16
VMEM_LIMIT = 56 * 1024 * 1024

F32 = jnp.float32
BF16 = jnp.bfloat16


def _gelu(x):
    return 0.5 * x * (1.0 + lax.erf(x * np.float32(math.sqrt(0.5))))


def _rms(x, g):
    return x * lax.rsqrt(jnp.mean(x * x, axis=-1, keepdims=True) + EPS) * g


def _params(*sem):
    return pltpu.CompilerParams(dimension_semantics=sem, vmem_limit_bytes=VMEM_LIMIT)


def _in_proj_a_kernel(x_ref, g1_ref, w_ref, gs_ref, ws_ref, bias_ref, go_ref, o_ref, h_ref,
                      ya_ref, wbf_ref):
    tm = x_ref.shape[0]
    d_a = o_ref.shape[1]
    hd = d_a // N_HEADS_A

    @pl.when(pl.program_id(0) == 0)
    def _():
        wbf_ref[...] = w_ref[...].astype(BF16)

    h = _rms(x_ref[...], g1_ref[...]).astype(BF16)
    h_ref[...] = h
    pv = jnp.dot(h, wbf_ref[:, d_a:], preferred_element_type=F32)
    zv = _rms(_gelu(pv), gs_ref[...]).astype(BF16)
    zu = _gelu(jnp.dot(h, wbf_ref[:, :d_a], preferred_element_type=F32))
    for c in range(tm // CHUNK):
        rows = slice(c * CHUNK, (c + 1) * CHUNK)
        for hh in range(N_HEADS_A):
            cols = slice(hh * hd, (hh + 1) * hd)
            s = jnp.dot(ws_ref[hh], zv[rows, cols], preferred_element_type=F32)
            ya_ref[rows, cols] = zu[rows, cols] * (s + bias_ref[:, cols])
    o_ref[...] = _rms(ya_ref[...], go_ref[...]).astype(o_ref.dtype)


def _in_proj_a(x2d, g1, w_a, gs, ws, bias_full, go, *, tm):
    m, d = x2d.shape
    d_a = gs.shape[1]
    return pl.pallas_call(
        _in_proj_a_kernel,
        grid=(m // tm,),
        in_specs=[
            pl.BlockSpec((tm, d), lambda i: (i, 0)),
            pl.BlockSpec((1, d), lambda i: (0, 0)),
            pl.BlockSpec((d, 2 * d_a), lambda i: (0, 0), pipeline_mode=pl.Buffered(1)),
            pl.BlockSpec((1, d_a), lambda i: (0, 0)),
            pl.BlockSpec((N_HEADS_A, CHUNK, CHUNK), lambda i: (0, 0, 0)),
            pl.BlockSpec((CHUNK, d_a), lambda i: (0, 0)),
            pl.BlockSpec((1, d_a), lambda i: (0, 0)),
        ],
        out_specs=[pl.BlockSpec((tm, d_a), lambda i: (i, 0)), pl.BlockSpec((tm, d), lambda i: (i, 0))],
        out_shape=[jax.ShapeDtypeStruct((m, d_a), BF16), jax.ShapeDtypeStruct((m, d), BF16)],
        scratch_shapes=[pltpu.VMEM((tm, d_a), F32), pltpu.VMEM((d, 2 * d_a), BF16)],
        compiler_params=_params("arbitrary"),
        name="in_proj_a",
    )(x2d, g1, w_a, gs, ws, bias_full, go)


def _fill_ext(hext_ref, prev, main, nxt, i, seq_tiles):
    tm = main.shape[0]
    has_prev = (i % seq_tiles) != 0
    has_next = ((i + 1) % seq_tiles) != 0
    hext_ref[0:HALO, :] = jnp.where(has_prev, prev, jnp.zeros_like(prev))
    hext_ref[HALO:HALO + tm, :] = main
    hext_ref[HALO + tm:, :] = jnp.where(has_next, nxt, jnp.zeros_like(nxt))


def _conv3(gext, w_ref, b_ref, tm):
    n = gext.shape[0]
    prev = pltpu.roll(gext, 1, 0)[HALO:HALO + tm]
    nxt = pltpu.roll(gext, n - 1, 0)[HALO:HALO + tm]
    cur = gext[HALO:HALO + tm]
    return prev * w_ref[0:1, :] + cur * w_ref[1:2, :] + nxt * w_ref[2:3, :] + b_ref[...]


def _in_proj_b_kernel(hp_ref, h_ref, hn_ref, w0_ref, w1_ref, w2_ref,
                      cw0_ref, cw1_ref, cw2_ref, cb0_ref, cb1_ref, cb2_ref,
                      x0_ref, u_ref, hext_ref, *, seq_tiles):
    tm = h_ref.shape[0]
    i = pl.program_id(0)

    @pl.when(pl.program_id(1) == 0)
    def _():
        _fill_ext(hext_ref, hp_ref[...], h_ref[...], hn_ref[...], i, seq_tiles)

    hext = hext_ref[...]

    def branch(w_ref, cw_ref, cb_ref):
        gext = jnp.dot(hext, w_ref[...].astype(BF16), preferred_element_type=F32)
        return _conv3(gext, cw_ref, cb_ref, tm)

    x0_ref[...] = branch(w0_ref, cw0_ref, cb0_ref)
    x1 = branch(w1_ref, cw1_ref, cb1_ref)
    v = branch(w2_ref, cw2_ref, cb2_ref)
    u_ref[...] = v * x1


def _halo_specs(tm, d, m):
    blocks_per_tile = tm // HALO
    last = m // HALO - 1
    return [
        pl.BlockSpec((HALO, d), lambda i, j: (jnp.maximum(i * blocks_per_tile - 1, 0), 0)),
        pl.BlockSpec((tm, d), lambda i, j: (i, 0)),
        pl.BlockSpec((HALO, d), lambda i, j: (jnp.minimum((i + 1) * blocks_per_tile, last), 0)),
    ]


def _in_proj_b(h, w_in, conv_w, conv_b, *, col0, d_b, seq_len, tm, tn):
    m, d = h.shape
    nj = d_b // tn
    w_specs = [pl.BlockSpec((d, tn), functools.partial(
        lambda i, j, off: (0, off + j), off=(col0 + g * d_b) // tn)) for g in range(3)]
    cw_specs = [pl.BlockSpec((3, tn), functools.partial(
        lambda i, j, off: (0, off + j), off=(g * d_b) // tn)) for g in range(3)]
    cb_specs = [pl.BlockSpec((1, tn), functools.partial(
        lambda i, j, off: (0, off + j), off=(g * d_b) // tn)) for g in range(3)]
    out_spec = pl.BlockSpec((tm, tn), lambda i, j: (i, j))
    return pl.pallas_call(
        functools.partial(_in_proj_b_kernel, seq_tiles=seq_len // tm),
        grid=(m // tm, nj),
        in_specs=_halo_specs(tm, d, m) + w_specs + cw_specs + cb_specs,
        out_specs=[out_spec, out_spec],
        out_shape=[jax.ShapeDtypeStruct((m, d_b), F32)] * 2,
        scratch_shapes=[pltpu.VMEM((tm + 2 * HALO, d), BF16)],
        compiler_params=_params("parallel", "arbitrary"),
        name="in_proj_b",
    )(h, h, h, w_in, w_in, w_in, conv_w, conv_w, conv_w, conv_b, conv_b, conv_b)


def _filter_feat_kernel(z_ref, w1_ref, b1_ref, w2_ref, b2_ref, w3_ref, b3_ref, fr_ref, h_ref):
    hi = lax.Precision.HIGHEST
    fr = fr_ref[...]
    h = z_ref[...]
    for w_ref, b_ref in ((w1_ref, b1_ref), (w2_ref, b2_ref), (w3_ref, b3_ref)):
        h = jnp.sin(fr * (jnp.dot(h, w_ref[...], precision=hi, preferred_element_type=F32) + b_ref[...]))
    h_ref[...] = h


def _filter_feat(z, w1, b1, w2, b2, w3, b3, fr, *, tl):
    seq_len, e = z.shape
    pack = LANE // e
    assert e * pack == LANE and seq_len % (pack * tl) == 0
    eye = jnp.eye(pack, dtype=F32)
    bd = lambda w: jnp.kron(eye, w)
    rep = lambda v: jnp.tile(v, (1, pack))
    full = lambda shape: pl.BlockSpec(shape, lambda i: (0, 0))
    rows = seq_len // pack
    h = pl.pallas_call(
        _filter_feat_kernel,
        grid=(rows // tl,),
        in_specs=[pl.BlockSpec((tl, LANE), lambda i: (i, 0)), full((LANE, LANE)), full((1, LANE)),
                  full((LANE, LANE)), full((1, LANE)), full((LANE, LANE)), full((1, LANE)),
                  full((1, LANE))],
        out_specs=pl.BlockSpec((tl, LANE), lambda i: (i, 0)),
        out_shape=jax.ShapeDtypeStruct((rows, LANE), F32),
        compiler_params=_params("parallel"),
        name="filter_feat",
    )(z.reshape(rows, LANE), bd(w1), rep(b1), bd(w2), rep(b2), bd(w3), rep(b3), rep(fr))
    return h.reshape(seq_len, e)


def _dot_bf16x3(a, b):
    a_hi = a.astype(BF16)
    b_hi = b.astype(BF16)
    a_lo = (a - a_hi.astype(F32)).astype(BF16)
    b_lo = (b - b_hi.astype(F32)).astype(BF16)
    dot = functools.partial(jnp.dot, preferred_element_type=F32)
    return dot(a_hi, b_hi) + dot(a_lo, b_hi) + dot(a_hi, b_lo)


def _dft1_kernel(w_ref, x_ref, o_ref, a_ref):
    g = x_ref.shape[2]
    xs = jnp.swapaxes(x_ref[0].astype(BF16), 0, 1)
    w = w_ref[...]
    for j in range(g):
        a_ref[j] = jnp.dot(w, xs[j], preferred_element_type=F32).astype(a_ref.dtype)
    o_ref[0] = jnp.swapaxes(a_ref[...], 0, 1)


def _dft1(w, x4, *, g, ct, name):
    nb, hq, r, c = x4.shape
    mo = w.shape[0]
    return pl.pallas_call(
        _dft1_kernel,
        grid=(nb, r // g, c // ct),
        in_specs=[pl.BlockSpec((mo, hq), lambda b, i, j: (0, 0)),
                  pl.BlockSpec((1, hq, g, ct), lambda b, i, j: (b, 0, i, j))],
        out_specs=pl.BlockSpec((1, mo, g, ct), lambda b, i, j: (b, 0, i, j)),
        out_shape=jax.ShapeDtypeStruct((nb, mo, r, c), BF16),
        scratch_shapes=[pltpu.VMEM((g, mo, ct), BF16)],
        compiler_params=_params("parallel", "parallel", "parallel"),
        name=name,
    )(w, x4)


def _filter_dft1_kernel(w_ref, h3_ref, t_ref, wo_ref, dl_ref, o_ref, l1_ref, a_ref, *, d_b):
    hq, g, e = h3_ref.shape
    ct = wo_ref.shape[1]
    j = pl.program_id(0)
    gi = pl.program_id(1)
    h3 = jnp.swapaxes(h3_ref[...], 0, 1).reshape(g * hq, e)
    hw = _dot_bf16x3(h3, wo_ref[...])
    hw = hw * jnp.exp(-t_ref[...].reshape(g * hq, 1) * dl_ref[...])
    row = lax.broadcasted_iota(jnp.int32, hw.shape, 0)
    hw = jnp.where((row == 0) & (gi == 0) & (j * ct >= d_b), 0.0, hw)

    @pl.when(gi == 0)
    def _():
        l1_ref[...] = jnp.zeros_like(l1_ref)

    l1_ref[...] += jnp.sum(jnp.abs(hw), axis=0, keepdims=True)
    x3 = hw.reshape(g, hq, ct).astype(BF16)
    w = w_ref[...]
    for jj in range(g):
        a_ref[jj] = jnp.dot(w, x3[jj], preferred_element_type=F32).astype(a_ref.dtype)
    o_ref[0] = jnp.swapaxes(a_ref[...], 0, 1)


def _filter_dft1(w, h3, t_sw, wout, deltas2, *, d_b, g, ct):
    hq, r, e = h3.shape
    c2 = wout.shape[1]
    mo = w.shape[0]
    return pl.pallas_call(
        functools.partial(_filter_dft1_kernel, d_b=d_b),
        grid=(c2 // ct, r // g),
        in_specs=[pl.BlockSpec((mo, hq), lambda j, i: (0, 0)),
                  pl.BlockSpec((hq, g, e), lambda j, i: (0, i, 0)),
                  pl.BlockSpec((g, hq, 1), lambda j, i: (i, 0, 0)),
                  pl.BlockSpec((e, ct), lambda j, i: (0, j)),
                  pl.BlockSpec((1, ct), lambda j, i: (0, j))],
        out_specs=[pl.BlockSpec((1, mo, g, ct), lambda j, i: (0, 0, i, j)),
                   pl.BlockSpec((1, ct), lambda j, i: (0, j))],
        out_shape=[jax.ShapeDtypeStruct((1, mo, r, c2), BF16), jax.ShapeDtypeStruct((1, c2), F32)],
        scratch_shapes=[pltpu.VMEM((g, mo, ct), BF16)],
        compiler_params=_params("parallel", "arbitrary"),
        name="filter_dft1",
    )(w, h3, t_sw, wout, deltas2)


def _filter_spec_kernel(af_ref, ab_ref, mf_ref, l1f_ref, l1b_ref, k_ref):
    kb = mf_ref.shape[0]
    r = mf_ref.shape[1] // 2
    ct = k_ref.shape[2]
    scale = 1.0 / (l1f_ref[...] + l1b_ref[...] + EPS)

    for kk in range(kb):
        mfk = mf_ref[kk]
        xf = jnp.dot(mfk, af_ref[0, :, kk].reshape(2 * r, ct), preferred_element_type=F32)
        xb = jnp.dot(mfk, ab_ref[0, :, kk].reshape(2 * r, ct), preferred_element_type=F32)
        k_ref[kk, 0:r, :] = (xf[:r] + xb[:r]) * scale
        k_ref[kk, r:, :] = (xf[r:] - xb[r:]) * scale


def _filter_spec(ak5, mf, l1, *, kb, ct):
    _, _, hq, r, c2 = ak5.shape
    c = c2 // 2
    nct = c // ct
    return pl.pallas_call(
        _filter_spec_kernel,
        grid=(hq // kb, nct),
        in_specs=[pl.BlockSpec((1, 2, kb, r, ct), lambda i, j: (0, 0, i, 0, j)),
                  pl.BlockSpec((1, 2, kb, r, ct), lambda i, j: (0, 0, i, 0, j + nct)),
                  pl.BlockSpec((kb, 2 * r, 2 * r), lambda i, j: (i, 0, 0)),
                  pl.BlockSpec((1, ct), lambda i, j: (0, j)),
                  pl.BlockSpec((1, ct), lambda i, j: (0, j + nct))],
        out_specs=pl.BlockSpec((kb, 2 * r, ct), lambda i, j: (i, 0, j)),
        out_shape=jax.ShapeDtypeStruct((hq, 2 * r, c), F32),
        compiler_params=_params("parallel", "parallel"),
        name="filter_spec",
    )(ak5, ak5, mf, l1, l1)


def _spec_conv_kernel(a_ref, mf_ref, mi_ref, k_ref, dr_ref, di_ref, d_scr):
    kb = mf_ref.shape[0]
    r = mf_ref.shape[1] // 2
    ct = k_ref.shape[2]

    for kk in range(kb):
        a = a_ref[0, :, kk].reshape(2 * r, ct)
        x = jnp.dot(mf_ref[kk], a, preferred_element_type=F32)
        xr, xi = x[:r], x[r:]
        kr, ki = k_ref[kk, 0:r, :], k_ref[kk, r:, :]
        y = jnp.concatenate([xr * kr - xi * ki, xr * ki + xi * kr], axis=0).astype(BF16)
        dd = jnp.dot(mi_ref[kk], y, preferred_element_type=F32)
        d_scr[0, kk] = dd[:r].astype(d_scr.dtype)
        d_scr[1, kk] = dd[r:].astype(d_scr.dtype)
    dr_ref[0] = jnp.swapaxes(d_scr[0], 0, 1)
    di_ref[0] = jnp.swapaxes(d_scr[1], 0, 1)


def _spec_conv(a5, mf, mi, khat, *, kb, ct):
    nb, _, hq, r, c = a5.shape
    mspec = pl.BlockSpec((kb, 2 * r, 2 * r), lambda i, j, b: (i, 0, 0))
    ospec = pl.BlockSpec((1, r, kb, ct), lambda i, j, b: (b, 0, i, j))
    oshape = jax.ShapeDtypeStruct((nb, r, hq, c), BF16)
    return pl.pallas_call(
        _spec_conv_kernel,
        grid=(hq // kb, c // ct, nb),
        in_specs=[pl.BlockSpec((1, 2, kb, r, ct), lambda i, j, b: (b, 0, i, 0, j)), mspec, mspec,
                  pl.BlockSpec((kb, 2 * r, ct), lambda i, j, b: (i, 0, j))],
        out_specs=[ospec, ospec],
        out_shape=[oshape, oshape],
        scratch_shapes=[pltpu.VMEM((2, kb, r, ct), BF16)],
        compiler_params=_params("parallel", "parallel", "arbitrary"),
        name="spec_conv",
    )(a5, mf, mi, khat)


def _conv_out_kernel(wr_ref, wi_ref, dr_ref, di_ref, u_ref, x0_ref, ds_ref, o_ref, y_scr):
    g = dr_ref.shape[1]
    wr, wi = wr_ref[...], wi_ref[...]
    for j in range(g):
        y_scr[j] = (jnp.dot(wr, dr_ref[0, j], preferred_element_type=F32)
                    + jnp.dot(wi, di_ref[0, j], preferred_element_type=F32))
    yconv = jnp.swapaxes(y_scr[...], 0, 1)
    o_ref[0] = x0_ref[0] * (yconv + u_ref[0] * ds_ref[...])


def _conv_out(wr, wi, dr, di, u4, x04, ds, *, g, ct):
    nb, r, hq, c = dr.shape
    dspec = pl.BlockSpec((1, g, hq, ct), lambda b, i, j: (b, i, 0, j))
    nspec = pl.BlockSpec((1, hq, g, ct), lambda b, i, j: (b, 0, i, j))
    wspec = pl.BlockSpec((hq, hq), lambda b, i, j: (0, 0))
    return pl.pallas_call(
        _conv_out_kernel,
        grid=(nb, r // g, c // ct),
        in_specs=[wspec, wspec, dspec, dspec, nspec, nspec,
                  pl.BlockSpec((1, ct), lambda b, i, j: (0, j))],
        out_specs=nspec,
        out_shape=jax.ShapeDtypeStruct((nb, hq, r, c), F32),
        scratch_shapes=[pltpu.VMEM((g, hq, ct), F32)],
        compiler_params=_params("parallel", "parallel", "parallel"),
        name="conv_out",
    )(wr, wi, dr, di, u4, x04, ds)


def _out_proj_kernel(ma_ref, yb_ref, x_ref, gb_ref, wa_ref, wb_ref, g2_ref, x1_ref, h2_ref,
                     wabf_ref, wbbf_ref):
    @pl.when(pl.program_id(0) == 0)
    def _():
        wabf_ref[...] = wa_ref[...].astype(BF16)
        wbbf_ref[...] = wb_ref[...].astype(BF16)

    mb = _rms(yb_ref[...], gb_ref[...]).astype(BF16)
    x1 = (x_ref[...]
          + jnp.dot(ma_ref[...], wabf_ref[...], preferred_element_type=F32)
          + jnp.dot(mb, wbbf_ref[...], preferred_element_type=F32))
    x1_ref[...] = x1
    h2_ref[...] = _rms(x1, g2_ref[...]).astype(h2_ref.dtype)


def _out_proj(ma, yb, x2d, gb, w_out, g2, *, tm):
    m, d = x2d.shape
    d_a = ma.shape[1]
    d_b = yb.shape[1]
    row = lambda w: pl.BlockSpec((tm, w), lambda i: (i, 0))
    return pl.pallas_call(
        _out_proj_kernel,
        grid=(m // tm,),
        in_specs=[row(d_a), row(d_b), row(d), pl.BlockSpec((1, d_b), lambda i: (0, 0)),
                  pl.BlockSpec((d_a, d), lambda i: (0, 0), pipeline_mode=pl.Buffered(1)),
                  pl.BlockSpec((d_b, d), lambda i: (d_a // d_b, 0), pipeline_mode=pl.Buffered(1)),
                  pl.BlockSpec((1, d), lambda i: (0, 0))],
        out_specs=[row(d), row(d)],
        out_shape=[jax.ShapeDtypeStruct((m, d), F32), jax.ShapeDtypeStruct((m, d), BF16)],
        scratch_shapes=[pltpu.VMEM((d_a, d), BF16), pltpu.VMEM((d_b, d), BF16)],
        compiler_params=_params("arbitrary"),
        name="out_proj",
    )(ma, yb, x2d, gb, w_out, w_out, g2)


def _ffn_kernel(hp_ref, h_ref, hn_ref, x1_hbm, wg_ref, wv_ref, cw_ref, cb_ref, wd_ref, gf_ref,
                o_ref, hext_ref, x1_sem, *, seq_tiles, final_norm, d_ff):
    tm = h_ref.shape[0]
    tf = wg_ref.shape[1]
    i = pl.program_id(0)
    f = pl.program_id(1)
    x1_copy = pltpu.make_async_copy(x1_hbm.at[pl.ds(i * tm, tm), :], o_ref, x1_sem)

    nf = pl.cdiv(d_ff, tf)
    assert nf >= 2
    dup = nf * tf - d_ff

    def hidden_tile(first, last):
        cols = slice(dup, tf) if last else slice(0, tf)
        gext = jnp.dot(hext_ref[...], wg_ref[:, cols].astype(BF16), preferred_element_type=F32)
        g = _conv3(gext, cw_ref[:, cols], cb_ref[:, cols], tm)
        val = jnp.dot(h_ref[...], wv_ref[:, cols].astype(BF16), preferred_element_type=F32)
        act = (_gelu(g) * val).astype(BF16)
        if first:
            x1_copy.wait()
        o_ref[...] += jnp.dot(act, wd_ref[cols, :].astype(BF16), preferred_element_type=F32)

    @pl.when(f == 0)
    def _():
        x1_copy.start()
        _fill_ext(hext_ref, hp_ref[...], h_ref[...], hn_ref[...], i, seq_tiles)
        hidden_tile(True, False)

    @pl.when((f > 0) & (f < nf - 1))
    def _():
        hidden_tile(False, False)

    @pl.when(f == nf - 1)
    def _():
        hidden_tile(False, True)
        if final_norm:
            o_ref[...] = _rms(o_ref[...], gf_ref[...])


def _ragged_start(j, tf, d_ff, base=0):
    return (base // LANE + jnp.minimum(j * (tf // LANE), (d_ff - tf) // LANE)) * LANE


def _ffn(h2, x1, w_up, cw, cb, wd, gf, *, seq_len, tm, tf, final_norm):
    m, d = x1.shape
    d_ff = wd.shape[0]
    assert d_ff % LANE == 0 and tf % LANE == 0 and d_ff >= tf
    el = pl.Element
    start = functools.partial(_ragged_start, tf=tf, d_ff=d_ff)
    return pl.pallas_call(
        functools.partial(_ffn_kernel, seq_tiles=seq_len // tm, final_norm=final_norm, d_ff=d_ff),
        grid=(m // tm, pl.cdiv(d_ff, tf)),
        in_specs=_halo_specs(tm, d, m) + [
            pl.BlockSpec(memory_space=pl.ANY),
            pl.BlockSpec((el(d), el(tf)), lambda i, j: (0, start(j))),
            pl.BlockSpec((el(d), el(tf)), lambda i, j: (0, start(j, base=d_ff))),
            pl.BlockSpec((el(3), el(tf)), lambda i, j: (0, start(j))),
            pl.BlockSpec((el(1), el(tf)), lambda i, j: (0, start(j))),
            pl.BlockSpec((el(tf), el(d)), lambda i, j: (start(j), 0)),
            pl.BlockSpec((1, d), lambda i, j: (0, 0)),
        ],
        out_specs=pl.BlockSpec((tm, d), lambda i, j: (i, 0)),
        out_shape=jax.ShapeDtypeStruct((m, d), F32),
        scratch_shapes=[pltpu.VMEM((tm + 2 * HALO, d), BF16), pltpu.SemaphoreType.DMA(())],
        compiler_params=_params("parallel", "arbitrary"),
        name="ffn",
    )(h2, h2, h2, x1, w_up, w_up, cw, cb, wd, gf)


@functools.lru_cache(maxsize=None)
def _dft_constants(seq_len):
    n = 2 * seq_len
    r = DFT_R
    q = n // r
    hq = q // 2
    k1 = np.arange(hq, dtype=np.float64)[:, None]
    n1 = np.arange(hq, dtype=np.float64)[None, :]
    th1 = 2.0 * np.pi * n1 * (k1 + 0.5) / q
    f1r, f1i = np.cos(th1), -np.sin(th1)
    f1s = np.concatenate([f1r, f1i], axis=0)
    f1rt, f1it = (2.0 / n) * f1r.T, (2.0 / n) * f1i.T
    k2 = np.arange(r, dtype=np.float64)[None, :, None]
    n2 = np.arange(r, dtype=np.float64)[None, None, :]
    kk = np.arange(hq, dtype=np.float64)[:, None, None]
    th2 = 2.0 * np.pi * (n2 * k2 / r + n2 * (kk + 0.5) / n)
    gr, gi = np.cos(th2), -np.sin(th2)
    mf = np.concatenate([np.concatenate([gr, -gi], axis=2),
                         np.concatenate([gi, gr], axis=2)], axis=1)
    grt, git = gr.transpose(0, 2, 1), gi.transpose(0, 2, 1)
    mi = np.concatenate([np.concatenate([grt, git], axis=2),
                         np.concatenate([-git, grt], axis=2)], axis=1)
    return f1s, f1rt, f1it, mf, mi


def _positional_features(seq_len, e_pad):
    t = jnp.linspace(0.0, 1.0, seq_len, dtype=F32)[:, None]
    w = (2.0 * math.pi / seq_len) * jnp.arange(seq_len, dtype=F32)[:, None]
    f = jnp.linspace(1e-4, N_BANDS - 1, N_BANDS, dtype=F32)[None]
    z = jnp.concatenate([t, jnp.cos(f * w), -jnp.sin(f * w)], axis=-1)
    return jnp.pad(z, ((0, 0), (0, e_pad - z.shape[1])))


def kernel(x, norm1_g, w_in, sgu_norm_g, sgu_w, sgu_b, hy_conv_w, hy_conv_b, hy_f_w1, hy_f_b1,
           hy_f_w2, hy_f_b2, hy_f_w3, hy_f_b3, hy_f_freq, hy_f_wout, hy_d_skip, outnorm_a_g,
           outnorm_b_g, w_out, norm2_g, ffn_w_up, ffn_dw_w, ffn_dw_b, ffn_w_down, final_g):
    nb, seq_len, d = x.shape
    depth = w_in.shape[0]
    d_a = sgu_norm_g.shape[1]
    d_b = hy_d_skip.shape[1]
    d_ff = ffn_dw_b.shape[1]
    m = nb * seq_len
    r = DFT_R
    hq = seq_len // r
    g = DFT_G
    kb = min(DFT_G, hq)
    tm = 512
    tf = 512

    f1s, f1rt, f1it, mf, mi = (jnp.asarray(a, F32).astype(BF16) for a in _dft_constants(seq_len))
    e_pad = hy_f_w2.shape[1]
    z = _positional_features(seq_len, e_pad)
    max_decay = math.log(DECAY_TARGET) / FAST_DECAY_PCT
    min_decay = math.log(DECAY_TARGET) / SLOW_DECAY_PCT
    deltas = jnp.abs(jnp.linspace(min_decay, max_decay, d_b, dtype=F32))
    deltas2 = jnp.concatenate([deltas, deltas])[None]

    x2d = x.reshape(m, d)
    row = lambda v: v.reshape(1, -1)
    for l in range(depth):
        w_in_l = w_in[l]
        g1 = row(norm1_g[l])

        bias_full = jnp.repeat(sgu_b[l].T, d_a // N_HEADS_A, axis=1)
        ma, h1 = _in_proj_a(x2d, g1, w_in_l, row(sgu_norm_g[l]), sgu_w[l].astype(BF16),
                            bias_full, row(outnorm_a_g[l]), tm=512)

        x0, u = _in_proj_b(h1, w_in_l, hy_conv_w[l], row(hy_conv_b[l]),
                           col0=2 * d_a, d_b=d_b, seq_len=seq_len, tm=1024, tn=512)
        u4 = u.reshape(nb, hq, r, d_b)
        x04 = x0.reshape(nb, hq, r, d_b)

        w1p = jnp.pad(hy_f_w1[l], ((0, e_pad - hy_f_w1.shape[1]), (0, 0)))
        h3 = _filter_feat(z, w1p, row(hy_f_b1[l]), hy_f_w2[l], row(hy_f_b2[l]), hy_f_w3[l],
                          row(hy_f_b3[l]), row(hy_f_freq[l]), tl=min(512, seq_len // 2))
        t_sw = z[:, 0].reshape(hq, r).T[:, :, None]
        ak, l1 = _filter_dft1(f1s, h3.reshape(hq, r, e_pad), t_sw, hy_f_wout[l], deltas2,
                              d_b=d_b, g=g, ct=512)
        khat = _filter_spec(ak.reshape(1, 2, hq, r, 2 * d_b), mf, l1, kb=kb, ct=256)

        a = _dft1(f1s, u4, g=g, ct=512, name="u_dft1")
        dr, di = _spec_conv(a.reshape(nb, 2, hq, r, d_b), mf, mi, khat, kb=kb, ct=256)
        yb = _conv_out(f1rt, f1it, dr, di, u4, x04, row(hy_d_skip[l]), g=g, ct=512)

        x1, h2 = _out_proj(ma, yb.reshape(m, d_b), x2d, row(outnorm_b_g[l]),
                           w_out[l], row(norm2_g[l]), tm=tm)

        x2d = _ffn(h2, x1, ffn_w_up[l], ffn_dw_w[l], row(ffn_dw_b[l]),
                   ffn_w_down[l], row(final_g), seq_len=seq_len, tm=1024, tf=tf,
                   final_norm=(l == depth - 1))
    return x2d.reshape(nb, seq_len, d)
```

```python
import functools
import math

import numpy as np
import jax
import jax.numpy as jnp
from jax import lax
from jax.experimental import pallas as pl
from jax.experimental.pallas import tpu as pltpu

EPS = 1e-6
CHUNK = 128
N_HEADS_A = 8
DECAY_TARGET = 1e-2
FAST_DECAY_PCT = 0.3
SLOW_DECAY_PCT = 1.5
N_BANDS = 16

LANE = 128
DFT_R = 128
DFT_G = 16
HALO = 16
VMEM_LIMIT = 56 * 1024 * 1024

F32 = jnp.float32
BF16 = jnp.bfloat16


def _gelu(x):
    return 0.5 * x * (1.0 + lax.erf(x * np.float32(math.sqrt(0.5))))


def _rms(x, g):
    return x * lax.rsqrt(jnp.mean(x * x, axis=-1, keepdims=True) + EPS) * g


def _params(*sem):
    return pltpu.CompilerParams(dimension_semantics=sem, vmem_limit_bytes=VMEM_LIMIT)


def _in_proj_a_kernel(x_ref, g1_ref, w_ref, gs_ref, ws_ref, bias_ref, go_ref, o_ref, h_ref,
                      ya_ref, wbf_ref):
    tm = x_ref.shape[0]
    d_a = o_ref.shape[1]
    hd = d_a // N_HEADS_A

    @pl.when(pl.program_id(0) == 0)
    def _():
        wbf_ref[...] = w_ref[...].astype(BF16)

    h = _rms(x_ref[...], g1_ref[...]).astype(BF16)
    h_ref[...] = h
    pv = jnp.dot(h, wbf_ref[:, d_a:], preferred_element_type=F32)
    zv = _rms(_gelu(pv), gs_ref[...]).astype(BF16)
    zu = _gelu(jnp.dot(h, wbf_ref[:, :d_a], preferred_element_type=F32))
    for c in range(tm // CHUNK):
        rows = slice(c * CHUNK, (c + 1) * CHUNK)
        for hh in range(N_HEADS_A):
            cols = slice(hh * hd, (hh + 1) * hd)
            s = jnp.dot(ws_ref[hh], zv[rows, cols], preferred_element_type=F32)
            ya_ref[rows, cols] = zu[rows, cols] * (s + bias_ref[:, cols])
    o_ref[...] = _rms(ya_ref[...], go_ref[...]).astype(o_ref.dtype)


def _in_proj_a(x2d, g1, w_a, gs, ws, bias_full, go, *, tm):
    m, d = x2d.shape
    d_a = gs.shape[1]
    return pl.pallas_call(
        _in_proj_a_kernel,
        grid=(m // tm,),
        in_specs=[
            pl.BlockSpec((tm, d), lambda i: (i, 0)),
            pl.BlockSpec((1, d), lambda i: (0, 0)),
            pl.BlockSpec((d, 2 * d_a), lambda i: (0, 0), pipeline_mode=pl.Buffered(1)),
            pl.BlockSpec((1, d_a), lambda i: (0, 0)),
            pl.BlockSpec((N_HEADS_A, CHUNK, CHUNK), lambda i: (0, 0, 0)),
            pl.BlockSpec((CHUNK, d_a), lambda i: (0, 0)),
            pl.BlockSpec((1, d_a), lambda i: (0, 0)),
        ],
        out_specs=[pl.BlockSpec((tm, d_a), lambda i: (i, 0)), pl.BlockSpec((tm, d), lambda i: (i, 0))],
        out_shape=[jax.ShapeDtypeStruct((m, d_a), BF16), jax.ShapeDtypeStruct((m, d), BF16)],
        scratch_shapes=[pltpu.VMEM((tm, d_a), F32), pltpu.VMEM((d, 2 * d_a), BF16)],
        compiler_params=_params("arbitrary"),
        name="in_proj_a",
    )(x2d, g1, w_a, gs, ws, bias_full, go)


def _fill_ext(hext_ref, prev, main, nxt, i, seq_tiles):
    tm = main.shape[0]
    has_prev = (i % seq_tiles) != 0
    has_next = ((i + 1) % seq_tiles) != 0
    hext_ref[0:HALO, :] = jnp.where(has_prev, prev, jnp.zeros_like(prev))
    hext_ref[HALO:HALO + tm, :] = main
    hext_ref[HALO + tm:, :] = jnp.where(has_next, nxt, jnp.zeros_like(nxt))


def _conv3(gext, w_ref, b_ref, tm):
    n = gext.shape[0]
    prev = pltpu.roll(gext, 1, 0)[HALO:HALO + tm]
    nxt = pltpu.roll(gext, n - 1, 0)[HALO:HALO + tm]
    cur = gext[HALO:HALO + tm]
    return prev * w_ref[0:1, :] + cur * w_ref[1:2, :] + nxt * w_ref[2:3, :] + b_ref[...]


def _in_proj_b_kernel(hp_ref, h_ref, hn_ref, w0_ref, w1_ref, w2_ref,
                      cw0_ref, cw1_ref, cw2_ref, cb0_ref, cb1_ref, cb2_ref,
                      x0_ref, u_ref, hext_ref, *, seq_tiles):
    tm = h_ref.shape[0]
    i = pl.program_id(0)

    @pl.when(pl.program_id(1) == 0)
    def _():
        _fill_ext(hext_ref, hp_ref[...], h_ref[...], hn_ref[...], i, seq_tiles)

    hext = hext_ref[...]

    def branch(w_ref, cw_ref, cb_ref):
        gext = jnp.dot(hext, w_ref[...].astype(BF16), preferred_element_type=F32)
        return _conv3(gext, cw_ref, cb_ref, tm)

    x0_ref[...] = branch(w0_ref, cw0_ref, cb0_ref).astype(x0_ref.dtype)
    x1 = branch(w1_ref, cw1_ref, cb1_ref)
    v = branch(w2_ref, cw2_ref, cb2_ref)
    u_ref[...] = (v * x1).astype(u_ref.dtype)


def _halo_specs(tm, d, m):
    blocks_per_tile = tm // HALO
    last = m // HALO - 1
    return [
        pl.BlockSpec((HALO, d), lambda i, j: (jnp.maximum(i * blocks_per_tile - 1, 0), 0)),
        pl.BlockSpec((tm, d), lambda i, j: (i, 0)),
        pl.BlockSpec((HALO, d), lambda i, j: (jnp.minimum((i + 1) * blocks_per_tile, last), 0)),
    ]


def _in_proj_b(h, w_in, conv_w, conv_b, *, col0, d_b, seq_len, tm, tn):
    m, d = h.shape
    nj = d_b // tn
    w_specs = [pl.BlockSpec((d, tn), functools.partial(
        lambda i, j, off: (0, off + j), off=(col0 + g * d_b) // tn)) for g in range(3)]
    cw_specs = [pl.BlockSpec((3, tn), functools.partial(
        lambda i, j, off: (0, off + j), off=(g * d_b) // tn)) for g in range(3)]
    cb_specs = [pl.BlockSpec((1, tn), functools.partial(
        lambda i, j, off: (0, off + j), off=(g * d_b) // tn)) for g in range(3)]
    out_spec = pl.BlockSpec((tm, tn), lambda i, j: (i, j))
    return pl.pallas_call(
        functools.partial(_in_proj_b_kernel, seq_tiles=seq_len // tm),
        grid=(m // tm, nj),
        in_specs=_halo_specs(tm, d, m) + w_specs + cw_specs + cb_specs,
        out_specs=[out_spec, out_spec],
        out_shape=[jax.ShapeDtypeStruct((m, d_b), BF16)] * 2,
        scratch_shapes=[pltpu.VMEM((tm + 2 * HALO, d), BF16)],
        compiler_params=_params("parallel", "arbitrary"),
        name="in_proj_b",
    )(h, h, h, w_in, w_in, w_in, conv_w, conv_w, conv_w, conv_b, conv_b, conv_b)


def _filter_feat_kernel(z_ref, w1_ref, b1_ref, w2_ref, b2_ref, w3_ref, b3_ref, fr_ref, h_ref):
    hi = lax.Precision.HIGHEST
    fr = fr_ref[...]
    h = z_ref[...]
    for w_ref, b_ref in ((w1_ref, b1_ref), (w2_ref, b2_ref), (w3_ref, b3_ref)):
        h = jnp.sin(fr * (jnp.dot(h, w_ref[...], precision=hi, preferred_element_type=F32) + b_ref[...]))
    h_ref[...] = h


def _filter_feat(z, w1, b1, w2, b2, w3, b3, fr, *, tl):
    seq_len, e = z.shape
    pack = LANE // e
    assert e * pack == LANE and seq_len % (pack * tl) == 0
    eye = jnp.eye(pack, dtype=F32)
    bd = lambda w: jnp.kron(eye, w)
    rep = lambda v: jnp.tile(v, (1, pack))
    full = lambda shape: pl.BlockSpec(shape, lambda i: (0, 0))
    rows = seq_len // pack
    h = pl.pallas_call(
        _filter_feat_kernel,
        grid=(rows // tl,),
        in_specs=[pl.BlockSpec((tl, LANE), lambda i: (i, 0)), full((LANE, LANE)), full((1, LANE)),
                  full((LANE, LANE)), full((1, LANE)), full((LANE, LANE)), full((1, LANE)),
                  full((1, LANE))],
        out_specs=pl.BlockSpec((tl, LANE), lambda i: (i, 0)),
        out_shape=jax.ShapeDtypeStruct((rows, LANE), F32),
        compiler_params=_params("parallel"),
        name="filter_feat",
    )(z.reshape(rows, LANE), bd(w1), rep(b1), bd(w2), rep(b2), bd(w3), rep(b3), rep(fr))
    return h.reshape(seq_len, e)


def _dot_bf16x3(a, b):
    a_hi = a.astype(BF16)
    b_hi = b.astype(BF16)
    a_lo = (a - a_hi.astype(F32)).astype(BF16)
    b_lo = (b - b_hi.astype(F32)).astype(BF16)
    dot = functools.partial(jnp.dot, preferred_element_type=F32)
    return dot(a_hi, b_hi) + dot(a_lo, b_hi) + dot(a_hi, b_lo)


def _dft1_kernel(w_ref, x_ref, o_ref, a_ref):
    g = x_ref.shape[2]
    xs = jnp.swapaxes(x_ref[0].astype(BF16), 0, 1)
    w = w_ref[...]
    for j in range(g):
        a_ref[j] = jnp.dot(w, xs[j], preferred_element_type=F32).astype(a_ref.dtype)
    o_ref[0] = jnp.swapaxes(a_ref[...], 0, 1)


def _dft1(w, x4, *, g, ct, name):
    nb, hq, r, c = x4.shape
    mo = w.shape[0]
    return pl.pallas_call(
        _dft1_kernel,
        grid=(nb, r // g, c // ct),
        in_specs=[pl.BlockSpec((mo, hq), lambda b, i, j: (0, 0)),
                  pl.BlockSpec((1, hq, g, ct), lambda b, i, j: (b, 0, i, j))],
        out_specs=pl.BlockSpec((1, mo, g, ct), lambda b, i, j: (b, 0, i, j)),
        out_shape=jax.ShapeDtypeStruct((nb, mo, r, c), BF16),
        scratch_shapes=[pltpu.VMEM((g, mo, ct), BF16)],
        compiler_params=_params("parallel", "parallel", "parallel"),
        name=name,
    )(w, x4)


def _filter_dft1_kernel(w_ref, h3_ref, t_ref, wo_ref, dl_ref, o_ref, l1_ref, a_ref, *, d_b):
    hq, g, e = h3_ref.shape
    ct = wo_ref.shape[1]
    j = pl.program_id(0)
    gi = pl.program_id(1)
    h3 = jnp.swapaxes(h3_ref[...], 0, 1).reshape(g * hq, e)
    hw = _dot_bf16x3(h3, wo_ref[...])
    hw = hw * jnp.exp(-t_ref[...].reshape(g * hq, 1) * dl_ref[...])
    row = lax.broadcasted_iota(jnp.int32, hw.shape, 0)
    hw = jnp.where((row == 0) & (gi == 0) & (j * ct >= d_b), 0.0, hw)

    @pl.when(gi == 0)
    def _():
        l1_ref[...] = jnp.zeros_like(l1_ref)

    l1_ref[...] += jnp.sum(jnp.abs(hw), axis=0, keepdims=True)
    x3 = hw.reshape(g, hq, ct).astype(BF16)
    w = w_ref[...]
    for jj in range(g):
        a_ref[jj] = jnp.dot(w, x3[jj], preferred_element_type=F32).astype(a_ref.dtype)
    o_ref[0] = jnp.swapaxes(a_ref[...], 0, 1)


def _filter_dft1(w, h3, t_sw, wout, deltas2, *, d_b, g, ct):
    hq, r, e = h3.shape
    c2 = wout.shape[1]
    mo = w.shape[0]
    return pl.pallas_call(
        functools.partial(_filter_dft1_kernel, d_b=d_b),
        grid=(c2 // ct, r // g),
        in_specs=[pl.BlockSpec((mo, hq), lambda j, i: (0, 0)),
                  pl.BlockSpec((hq, g, e), lambda j, i: (0, i, 0)),
                  pl.BlockSpec((g, hq, 1), lambda j, i: (i, 0, 0)),
                  pl.BlockSpec((e, ct), lambda j, i: (0, j)),
                  pl.BlockSpec((1, ct), lambda j, i: (0, j))],
        out_specs=[pl.BlockSpec((1, mo, g, ct), lambda j, i: (0, 0, i, j)),
                   pl.BlockSpec((1, ct), lambda j, i: (0, j))],
        out_shape=[jax.ShapeDtypeStruct((1, mo, r, c2), BF16), jax.ShapeDtypeStruct((1, c2), F32)],
        scratch_shapes=[pltpu.VMEM((g, mo, ct), BF16)],
        compiler_params=_params("parallel", "arbitrary"),
        name="filter_dft1",
    )(w, h3, t_sw, wout, deltas2)


def _filter_spec_kernel(af_ref, ab_ref, mf_ref, l1f_ref, l1b_ref, k_ref):
    kb = mf_ref.shape[0]
    r = mf_ref.shape[1] // 2
    ct = k_ref.shape[2]
    scale = 1.0 / (l1f_ref[...] + l1b_ref[...] + EPS)

    for kk in range(kb):
        mfk = mf_ref[kk]
        xf = jnp.dot(mfk, af_ref[0, :, kk].reshape(2 * r, ct), preferred_element_type=F32)
        xb = jnp.dot(mfk, ab_ref[0, :, kk].reshape(2 * r, ct), preferred_element_type=F32)
        k_ref[kk, 0:r, :] = ((xf[:r] + xb[:r]) * scale).astype(k_ref.dtype)
        k_ref[kk, r:, :] = ((xf[r:] - xb[r:]) * scale).astype(k_ref.dtype)


def _filter_spec(ak5, mf, l1, *, kb, ct):
    _, _, hq, r, c2 = ak5.shape
    c = c2 // 2
    nct = c // ct
    return pl.pallas_call(
        _filter_spec_kernel,
        grid=(hq // kb, nct),
        in_specs=[pl.BlockSpec((1, 2, kb, r, ct), lambda i, j: (0, 0, i, 0, j)),
                  pl.BlockSpec((1, 2, kb, r, ct), lambda i, j: (0, 0, i, 0, j + nct)),
                  pl.BlockSpec((kb, 2 * r, 2 * r), lambda i, j: (i, 0, 0)),
                  pl.BlockSpec((1, ct), lambda i, j: (0, j)),
                  pl.BlockSpec((1, ct), lambda i, j: (0, j + nct))],
        out_specs=pl.BlockSpec((kb, 2 * r, ct), lambda i, j: (i, 0, j)),
        out_shape=jax.ShapeDtypeStruct((hq, 2 * r, c), BF16),
        compiler_params=_params("parallel", "parallel"),
        name="filter_spec",
    )(ak5, ak5, mf, l1, l1)


def _spec_conv_kernel(a_ref, mf_ref, mi_ref, k_ref, dr_ref, di_ref, d_scr):
    kb = mf_ref.shape[0]
    r = mf_ref.shape[1] // 2
    ct = k_ref.shape[2]

    for kk in range(kb):
        a = a_ref[0, :, kk].reshape(2 * r, ct)
        x = jnp.dot(mf_ref[kk], a, preferred_element_type=F32)
        xr, xi = x[:r], x[r:]
        kr, ki = k_ref[kk, 0:r, :].astype(F32), k_ref[kk, r:, :].astype(F32)
        y = jnp.concatenate([xr * kr - xi * ki, xr * ki + xi * kr], axis=0).astype(BF16)
        dd = jnp.dot(mi_ref[kk], y, preferred_element_type=F32)
        d_scr[0, kk] = dd[:r].astype(d_scr.dtype)
        d_scr[1, kk] = dd[r:].astype(d_scr.dtype)
    dr_ref[0] = jnp.swapaxes(d_scr[0], 0, 1)
    di_ref[0] = jnp.swapaxes(d_scr[1], 0, 1)


def _spec_conv(a5, mf, mi, khat, *, kb, ct):
    nb, _, hq, r, c = a5.shape
    mspec = pl.BlockSpec((kb, 2 * r, 2 * r), lambda i, j, b: (i, 0, 0))
    ospec = pl.BlockSpec((1, r, kb, ct), lambda i, j, b: (b, 0, i, j))
    oshape = jax.ShapeDtypeStruct((nb, r, hq, c), BF16)
    return pl.pallas_call(
        _spec_conv_kernel,
        grid=(hq // kb, c // ct, nb),
        in_specs=[pl.BlockSpec((1, 2, kb, r, ct), lambda i, j, b: (b, 0, i, 0, j)), mspec, mspec,
                  pl.BlockSpec((kb, 2 * r, ct), lambda i, j, b: (i, 0, j))],
        out_specs=[ospec, ospec],
        out_shape=[oshape, oshape],
        scratch_shapes=[pltpu.VMEM((2, kb, r, ct), BF16)],
        compiler_params=_params("parallel", "parallel", "arbitrary"),
        name="spec_conv",
    )(a5, mf, mi, khat)


def _conv_out_kernel(wr_ref, wi_ref, dr_ref, di_ref, u_ref, x0_ref, ds_ref, o_ref, y_scr):
    g = dr_ref.shape[1]
    wr, wi = wr_ref[...], wi_ref[...]
    for j in range(g):
        y_scr[j] = (jnp.dot(wr, dr_ref[0, j], preferred_element_type=F32)
                    + jnp.dot(wi, di_ref[0, j], preferred_element_type=F32))
    yconv = jnp.swapaxes(y_scr[...], 0, 1)
    y_b = x0_ref[0].astype(F32) * (yconv + u_ref[0].astype(F32) * ds_ref[...])
    o_ref[0] = y_b.astype(o_ref.dtype)


def _conv_out(wr, wi, dr, di, u4, x04, ds, *, g, ct):
    nb, r, hq, c = dr.shape
    dspec = pl.BlockSpec((1, g, hq, ct), lambda b, i, j: (b, i, 0, j))
    nspec = pl.BlockSpec((1, hq, g, ct), lambda b, i, j: (b, 0, i, j))
    wspec = pl.BlockSpec((hq, hq), lambda b, i, j: (0, 0))
    return pl.pallas_call(
        _conv_out_kernel,
        grid=(nb, r // g, c // ct),
        in_specs=[wspec, wspec, dspec, dspec, nspec, nspec,
                  pl.BlockSpec((1, ct), lambda b, i, j: (0, j))],
        out_specs=nspec,
        out_shape=jax.ShapeDtypeStruct((nb, hq, r, c), BF16),
        scratch_shapes=[pltpu.VMEM((g, hq, ct), F32)],
        compiler_params=_params("parallel", "parallel", "parallel"),
        name="conv_out",
    )(wr, wi, dr, di, u4, x04, ds)


def _out_proj_kernel(ma_ref, yb_ref, x_ref, gb_ref, wa_ref, wb_ref, g2_ref, x1_ref, h2_ref,
                     wabf_ref, wbbf_ref):
    @pl.when(pl.program_id(0) == 0)
    def _():
        wabf_ref[...] = wa_ref[...].astype(BF16)
        wbbf_ref[...] = wb_ref[...].astype(BF16)

    mb = _rms(yb_ref[...].astype(F32), gb_ref[...]).astype(BF16)
    x1 = (x_ref[...]
          + jnp.dot(ma_ref[...], wabf_ref[...], preferred_element_type=F32)
          + jnp.dot(mb, wbbf_ref[...], preferred_element_type=F32))
    x1_ref[...] = x1
    h2_ref[...] = _rms(x1, g2_ref[...]).astype(h2_ref.dtype)


def _out_proj(ma, yb, x2d, gb, w_out, g2, *, tm):
    m, d = x2d.shape
    d_a = ma.shape[1]
    d_b = yb.shape[1]
    row = lambda w: pl.BlockSpec((tm, w), lambda i: (i, 0))
    return pl.pallas_call(
        _out_proj_kernel,
        grid=(m // tm,),
        in_specs=[row(d_a), row(d_b), row(d), pl.BlockSpec((1, d_b), lambda i: (0, 0)),
                  pl.BlockSpec((d_a, d), lambda i: (0, 0), pipeline_mode=pl.Buffered(1)),
                  pl.BlockSpec((d_b, d), lambda i: (d_a // d_b, 0), pipeline_mode=pl.Buffered(1)),
                  pl.BlockSpec((1, d), lambda i: (0, 0))],
        out_specs=[row(d), row(d)],
        out_shape=[jax.ShapeDtypeStruct((m, d), F32), jax.ShapeDtypeStruct((m, d), BF16)],
        scratch_shapes=[pltpu.VMEM((d_a, d), BF16), pltpu.VMEM((d_b, d), BF16)],
        compiler_params=_params("arbitrary"),
        name="out_proj",
    )(ma, yb, x2d, gb, w_out, w_out, g2)


def _ffn_kernel(hp_ref, h_ref, hn_ref, x1_hbm, wg_ref, wv_ref, cw_ref, cb_ref, wd_ref, gf_ref,
                o_ref, hext_ref, x1_sem, *, seq_tiles, final_norm, d_ff):
    tm = h_ref.shape[0]
    tf = wg_ref.shape[1]
    i = pl.program_id(0)
    f = pl.program_id(1)
    x1_copy = pltpu.make_async_copy(x1_hbm.at[pl.ds(i * tm, tm), :], o_ref, x1_sem)

    nf = pl.cdiv(d_ff, tf)
    assert nf >= 2
    dup = nf * tf - d_ff

    def hidden_tile(first, last):
        cols = slice(dup, tf) if last else slice(0, tf)
        gext = jnp.dot(hext_ref[...], wg_ref[:, cols].astype(BF16), preferred_element_type=F32)
        g = _conv3(gext, cw_ref[:, cols], cb_ref[:, cols], tm)
        val = jnp.dot(h_ref[...], wv_ref[:, cols].astype(BF16), preferred_element_type=F32)
        act = (_gelu(g) * val).astype(BF16)
        if first:
            x1_copy.wait()
        o_ref[...] += jnp.dot(act, wd_ref[cols, :].astype(BF16), preferred_element_type=F32)

    @pl.when(f == 0)
    def _():
        x1_copy.start()
        _fill_ext(hext_ref, hp_ref[...], h_ref[...], hn_ref[...], i, seq_tiles)
        hidden_tile(True, False)

    @pl.when((f > 0) & (f < nf - 1))
    def _():
        hidden_tile(False, False)

    @pl.when(f == nf - 1)
    def _():
        hidden_tile(False, True)
        if final_norm:
            o_ref[...] = _rms(o_ref[...], gf_ref[...])


def _ragged_start(j, tf, d_ff, base=0):
    return (base // LANE + jnp.minimum(j * (tf // LANE), (d_ff - tf) // LANE)) * LANE


def _ffn(h2, x1, w_up, cw, cb, wd, gf, *, seq_len, tm, tf, final_norm):
    m, d = x1.shape
    d_ff = wd.shape[0]
    assert d_ff % LANE == 0 and tf % LANE == 0 and d_ff >= tf
    el = pl.Element
    start = functools.partial(_ragged_start, tf=tf, d_ff=d_ff)
    return pl.pallas_call(
        functools.partial(_ffn_kernel, seq_tiles=seq_len // tm, final_norm=final_norm, d_ff=d_ff),
        grid=(m // tm, pl.cdiv(d_ff, tf)),
        in_specs=_halo_specs(tm, d, m) + [
            pl.BlockSpec(memory_space=pl.ANY),
            pl.BlockSpec((el(d), el(tf)), lambda i, j: (0, start(j))),
            pl.BlockSpec((el(d), el(tf)), lambda i, j: (0, start(j, base=d_ff))),
            pl.BlockSpec((el(3), el(tf)), lambda i, j: (0, start(j))),
            pl.BlockSpec((el(1), el(tf)), lambda i, j: (0, start(j))),
            pl.BlockSpec((el(tf), el(d)), lambda i, j: (start(j), 0)),
            pl.BlockSpec((1, d), lambda i, j: (0, 0)),
        ],
        out_specs=pl.BlockSpec((tm, d), lambda i, j: (i, 0)),
        out_shape=jax.ShapeDtypeStruct((m, d), F32),
        scratch_shapes=[pltpu.VMEM((tm + 2 * HALO, d), BF16), pltpu.SemaphoreType.DMA(())],
        compiler_params=_params("parallel", "arbitrary"),
        name="ffn",
    )(h2, h2, h2, x1, w_up, w_up, cw, cb, wd, gf)


@functools.lru_cache(maxsize=None)
def _dft_constants(seq_len):
    n = 2 * seq_len
    r = DFT_R
    q = n // r
    hq = q // 2
    k1 = np.arange(hq, dtype=np.float64)[:, None]
    n1 = np.arange(hq, dtype=np.float64)[None, :]
    th1 = 2.0 * np.pi * n1 * (k1 + 0.5) / q
    f1r, f1i = np.cos(th1), -np.sin(th1)
    f1s = np.concatenate([f1r, f1i], axis=0)
    f1rt, f1it = (2.0 / n) * f1r.T, (2.0 / n) * f1i.T
    k2 = np.arange(r, dtype=np.float64)[None, :, None]
    n2 = np.arange(r, dtype=np.float64)[None, None, :]
    kk = np.arange(hq, dtype=np.float64)[:, None, None]
    th2 = 2.0 * np.pi * (n2 * k2 / r + n2 * (kk + 0.5) / n)
    gr, gi = np.cos(th2), -np.sin(th2)
    mf = np.concatenate([np.concatenate([gr, -gi], axis=2),
                         np.concatenate([gi, gr], axis=2)], axis=1)
    grt, git = gr.transpose(0, 2, 1), gi.transpose(0, 2, 1)
    mi = np.concatenate([np.concatenate([grt, git], axis=2),
                         np.concatenate([-git, grt], axis=2)], axis=1)
    return f1s, f1rt, f1it, mf, mi


def _positional_features(seq_len, e_pad):
    t = jnp.linspace(0.0, 1.0, seq_len, dtype=F32)[:, None]
    w = (2.0 * math.pi / seq_len) * jnp.arange(seq_len, dtype=F32)[:, None]
    f = jnp.linspace(1e-4, N_BANDS - 1, N_BANDS, dtype=F32)[None]
    z = jnp.concatenate([t, jnp.cos(f * w), -jnp.sin(f * w)], axis=-1)
    return jnp.pad(z, ((0, 0), (0, e_pad - z.shape[1])))


def kernel(x, norm1_g, w_in, sgu_norm_g, sgu_w, sgu_b, hy_conv_w, hy_conv_b, hy_f_w1, hy_f_b1,
           hy_f_w2, hy_f_b2, hy_f_w3, hy_f_b3, hy_f_freq, hy_f_wout, hy_d_skip, outnorm_a_g,
           outnorm_b_g, w_out, norm2_g, ffn_w_up, ffn_dw_w, ffn_dw_b, ffn_w_down, final_g):
    nb, seq_len, d = x.shape
    depth = w_in.shape[0]
    d_a = sgu_norm_g.shape[1]
    d_b = hy_d_skip.shape[1]
    d_ff = ffn_dw_b.shape[1]
    m = nb * seq_len
    r = DFT_R
    hq = seq_len // r
    g = DFT_G
    kb = min(DFT_G, hq)
    tm = 512
    tf = 512

    f1s, f1rt, f1it, mf, mi = (jnp.asarray(a, F32).astype(BF16) for a in _dft_constants(seq_len))
    e_pad = hy_f_w2.shape[1]
    z = _positional_features(seq_len, e_pad)
    max_decay = math.log(DECAY_TARGET) / FAST_DECAY_PCT
    min_decay = math.log(DECAY_TARGET) / SLOW_DECAY_PCT
    deltas = jnp.abs(jnp.linspace(min_decay, max_decay, d_b, dtype=F32))
    deltas2 = jnp.concatenate([deltas, deltas])[None]

    x2d = x.reshape(m, d)
    row = lambda v: v.reshape(1, -1)
    for l in range(depth):
        w_in_l = w_in[l]
        g1 = row(norm1_g[l])

        bias_full = jnp.repeat(sgu_b[l].T, d_a // N_HEADS_A, axis=1)
        ma, h1 = _in_proj_a(x2d, g1, w_in_l, row(sgu_norm_g[l]), sgu_w[l].astype(BF16),
                            bias_full, row(outnorm_a_g[l]), tm=512)

        x0, u = _in_proj_b(h1, w_in_l, hy_conv_w[l], row(hy_conv_b[l]),
                           col0=2 * d_a, d_b=d_b, seq_len=seq_len, tm=1024, tn=512)
        u4 = u.reshape(nb, hq, r, d_b)
        x04 = x0.reshape(nb, hq, r, d_b)

        w1p = jnp.pad(hy_f_w1[l], ((0, e_pad - hy_f_w1.shape[1]), (0, 0)))
        h3 = _filter_feat(z, w1p, row(hy_f_b1[l]), hy_f_w2[l], row(hy_f_b2[l]), hy_f_w3[l],
                          row(hy_f_b3[l]), row(hy_f_freq[l]), tl=min(512, seq_len // 2))
        t_sw = z[:, 0].reshape(hq, r).T[:, :, None]
        ak, l1 = _filter_dft1(f1s, h3.reshape(hq, r, e_pad), t_sw, hy_f_wout[l], deltas2,
                              d_b=d_b, g=g, ct=512)
        khat = _filter_spec(ak.reshape(1, 2, hq, r, 2 * d_b), mf, l1, kb=kb, ct=256)

        a = _dft1(f1s, u4, g=g, ct=512, name="u_dft1")
        dr, di = _spec_conv(a.reshape(nb, 2, hq, r, d_b), mf, mi, khat, kb=kb, ct=256)
        yb = _conv_out(f1rt, f1it, dr, di, u4, x04, row(hy_d_skip[l]), g=g, ct=512)

        x1, h2 = _out_proj(ma, yb.reshape(m, d_b), x2d, row(outnorm_b_g[l]),
                           w_out[l], row(norm2_g[l]), tm=tm)

        x2d = _ffn(h2, x1, ffn_w_up[l], ffn_dw_w[l], row(ffn_dw_b[l]),
                   ffn_w_down[l], row(final_g), seq_len=seq_len, tm=1024, tf=tf,
                   final_norm=(l == depth - 1))
    return x2d.reshape(nb, seq_len, d)
```

```python
import functools
import math

import numpy as np
import jax
import jax.numpy as jnp
from jax import lax
from jax.experimental import pallas as pl
from jax.experimental.pallas import tpu as pltpu

EPS = 1e-6
CHUNK = 128
N_HEADS_A = 8
DECAY_TARGET = 1e-2
FAST_DECAY_PCT = 0.3
SLOW_DECAY_PCT = 1.5
N_BANDS = 16

LANE = 128
DFT_R = 128
DFT_G = 16
HALO = 16
VMEM_LIMIT = 56 * 1024 * 1024

F32 = jnp.float32
BF16 = jnp.bfloat16


def _gelu(x):
    return 0.5 * x * (1.0 + lax.erf(x * np.float32(math.sqrt(0.5))))


def _rms(x, g):
    return x * lax.rsqrt(jnp.mean(x * x, axis=-1, keepdims=True) + EPS) * g


def _params(*sem):
    return pltpu.CompilerParams(dimension_semantics=sem, vmem_limit_bytes=VMEM_LIMIT)


def _in_proj_a_kernel(x_ref, g1_ref, w_ref, gs_ref, ws_ref, bias_ref, go_ref, side_ref,
                      o_ref, h_ref, side_bf_ref, ya_ref, wbf_ref):
    tm = x_ref.shape[0]
    d_a = o_ref.shape[1]
    hd = d_a // N_HEADS_A
    side_bf_ref[...] = side_ref[...].astype(BF16)

    @pl.when(pl.program_id(0) == 0)
    def _():
        wbf_ref[...] = w_ref[...].astype(BF16)

    h = _rms(x_ref[...], g1_ref[...]).astype(BF16)
    h_ref[...] = h
    pv = jnp.dot(h, wbf_ref[:, d_a:], preferred_element_type=F32)
    zv = _rms(_gelu(pv), gs_ref[...]).astype(BF16)
    zu = _gelu(jnp.dot(h, wbf_ref[:, :d_a], preferred_element_type=F32))
    for c in range(tm // CHUNK):
        rows = slice(c * CHUNK, (c + 1) * CHUNK)
        for hh in range(N_HEADS_A):
            cols = slice(hh * hd, (hh + 1) * hd)
            s = jnp.dot(ws_ref[hh], zv[rows, cols], preferred_element_type=F32)
            ya_ref[rows, cols] = zu[rows, cols] * (s + bias_ref[:, cols])
    o_ref[...] = _rms(ya_ref[...], go_ref[...]).astype(o_ref.dtype)


def _in_proj_a(x2d, g1, w_a, gs, ws, bias_full, go, side, *, tm):
    m, d = x2d.shape
    d_a = gs.shape[1]
    side_rows = side.shape[0] // (m // tm)
    assert side_rows * (m // tm) == side.shape[0] and side_rows % 16 == 0
    side_spec = pl.BlockSpec((side_rows, side.shape[1]), lambda i: (i, 0))
    return pl.pallas_call(
        _in_proj_a_kernel,
        grid=(m // tm,),
        in_specs=[
            pl.BlockSpec((tm, d), lambda i: (i, 0)),
            pl.BlockSpec((1, d), lambda i: (0, 0)),
            pl.BlockSpec((d, 2 * d_a), lambda i: (0, 0), pipeline_mode=pl.Buffered(1)),
            pl.BlockSpec((1, d_a), lambda i: (0, 0)),
            pl.BlockSpec((N_HEADS_A, CHUNK, CHUNK), lambda i: (0, 0, 0)),
            pl.BlockSpec((CHUNK, d_a), lambda i: (0, 0)),
            pl.BlockSpec((1, d_a), lambda i: (0, 0)),
            side_spec,
        ],
        out_specs=[pl.BlockSpec((tm, d_a), lambda i: (i, 0)), pl.BlockSpec((tm, d), lambda i: (i, 0)),
                   side_spec],
        out_shape=[jax.ShapeDtypeStruct((m, d_a), BF16), jax.ShapeDtypeStruct((m, d), BF16),
                   jax.ShapeDtypeStruct(side.shape, BF16)],
        scratch_shapes=[pltpu.VMEM((tm, d_a), F32), pltpu.VMEM((d, 2 * d_a), BF16)],
        compiler_params=_params("arbitrary"),
        name="in_proj_a",
    )(x2d, g1, w_a, gs, ws, bias_full, go, side)


def _fill_ext(hext_ref, prev, main, nxt, i, seq_tiles):
    tm = main.shape[0]
    has_prev = (i % seq_tiles) != 0
    has_next = ((i + 1) % seq_tiles) != 0
    hext_ref[0:HALO, :] = jnp.where(has_prev, prev, jnp.zeros_like(prev))
    hext_ref[HALO:HALO + tm, :] = main
    hext_ref[HALO + tm:, :] = jnp.where(has_next, nxt, jnp.zeros_like(nxt))


def _conv3(gext, w_ref, b_ref, tm):
    n = gext.shape[0]
    prev = pltpu.roll(gext, 1, 0)[HALO:HALO + tm]
    nxt = pltpu.roll(gext, n - 1, 0)[HALO:HALO + tm]
    cur = gext[HALO:HALO + tm]
    return prev * w_ref[0:1, :] + cur * w_ref[1:2, :] + nxt * w_ref[2:3, :] + b_ref[...]


def _in_proj_b_kernel(hp_ref, h_ref, hn_ref, w0_ref, w1_ref, w2_ref,
                      cw0_ref, cw1_ref, cw2_ref, cb0_ref, cb1_ref, cb2_ref,
                      x0_ref, u_ref, hext_ref, *, seq_tiles):
    tm = h_ref.shape[0]
    i = pl.program_id(0)

    @pl.when(pl.program_id(1) == 0)
    def _():
        _fill_ext(hext_ref, hp_ref[...], h_ref[...], hn_ref[...], i, seq_tiles)

    hext = hext_ref[...]

    def branch(w_ref, cw_ref, cb_ref):
        gext = jnp.dot(hext, w_ref[...].astype(BF16), preferred_element_type=F32)
        return _conv3(gext, cw_ref, cb_ref, tm)

    x0_ref[...] = branch(w0_ref, cw0_ref, cb0_ref).astype(x0_ref.dtype)
    x1 = branch(w1_ref, cw1_ref, cb1_ref)
    v = branch(w2_ref, cw2_ref, cb2_ref)
    u_ref[...] = (v * x1).astype(u_ref.dtype)


def _halo_specs(tm, d, m):
    blocks_per_tile = tm // HALO
    last = m // HALO - 1
    return [
        pl.BlockSpec((HALO, d), lambda i, j: (jnp.maximum(i * blocks_per_tile - 1, 0), 0)),
        pl.BlockSpec((tm, d), lambda i, j: (i, 0)),
        pl.BlockSpec((HALO, d), lambda i, j: (jnp.minimum((i + 1) * blocks_per_tile, last), 0)),
    ]


def _in_proj_b(h, w_in, conv_w, conv_b, *, col0, d_b, seq_len, tm, tn):
    m, d = h.shape
    nj = d_b // tn
    w_specs = [pl.BlockSpec((d, tn), functools.partial(
        lambda i, j, off: (0, off + j), off=(col0 + g * d_b) // tn)) for g in range(3)]
    cw_specs = [pl.BlockSpec((3, tn), functools.partial(
        lambda i, j, off: (0, off + j), off=(g * d_b) // tn)) for g in range(3)]
    cb_specs = [pl.BlockSpec((1, tn), functools.partial(
        lambda i, j, off: (0, off + j), off=(g * d_b) // tn)) for g in range(3)]
    out_spec = pl.BlockSpec((tm, tn), lambda i, j: (i, j))
    return pl.pallas_call(
        functools.partial(_in_proj_b_kernel, seq_tiles=seq_len // tm),
        grid=(m // tm, nj),
        in_specs=_halo_specs(tm, d, m) + w_specs + cw_specs + cb_specs,
        out_specs=[out_spec, out_spec],
        out_shape=[jax.ShapeDtypeStruct((m, d_b), BF16)] * 2,
        scratch_shapes=[pltpu.VMEM((tm + 2 * HALO, d), BF16)],
        compiler_params=_params("parallel", "arbitrary"),
        name="in_proj_b",
    )(h, h, h, w_in, w_in, w_in, conv_w, conv_w, conv_w, conv_b, conv_b, conv_b)


def _filter_feat_kernel(z_ref, w1_ref, b1_ref, w2_ref, b2_ref, w3_ref, b3_ref, fr_ref, h_ref):
    hi = lax.Precision.HIGHEST
    fr = fr_ref[...]
    h = z_ref[...]
    for w_ref, b_ref in ((w1_ref, b1_ref), (w2_ref, b2_ref), (w3_ref, b3_ref)):
        h = jnp.sin(fr * (jnp.dot(h, w_ref[...], precision=hi, preferred_element_type=F32) + b_ref[...]))
    h_ref[...] = h


def _filter_feat(z, w1, b1, w2, b2, w3, b3, fr, *, tl):
    seq_len, e = z.shape
    pack = LANE // e
    assert e * pack == LANE and seq_len % (pack * tl) == 0
    eye = jnp.eye(pack, dtype=F32)
    bd = lambda w: jnp.kron(eye, w)
    rep = lambda v: jnp.tile(v, (1, pack))
    full = lambda shape: pl.BlockSpec(shape, lambda i: (0, 0))
    rows = seq_len // pack
    h = pl.pallas_call(
        _filter_feat_kernel,
        grid=(rows // tl,),
        in_specs=[pl.BlockSpec((tl, LANE), lambda i: (i, 0)), full((LANE, LANE)), full((1, LANE)),
                  full((LANE, LANE)), full((1, LANE)), full((LANE, LANE)), full((1, LANE)),
                  full((1, LANE))],
        out_specs=pl.BlockSpec((tl, LANE), lambda i: (i, 0)),
        out_shape=jax.ShapeDtypeStruct((rows, LANE), F32),
        compiler_params=_params("parallel"),
        name="filter_feat",
    )(z.reshape(rows, LANE), bd(w1), rep(b1), bd(w2), rep(b2), bd(w3), rep(b3), rep(fr))
    return h.reshape(seq_len, e)


def _dot_bf16x3(a, b):
    a_hi = a.astype(BF16)
    b_hi = b.astype(BF16)
    a_lo = (a - a_hi.astype(F32)).astype(BF16)
    b_lo = (b - b_hi.astype(F32)).astype(BF16)
    dot = functools.partial(jnp.dot, preferred_element_type=F32)
    return dot(a_hi, b_hi) + dot(a_lo, b_hi) + dot(a_hi, b_lo)


def _dft1_kernel(w_ref, x_ref, o_ref, a_ref):
    g = x_ref.shape[2]
    xs = jnp.swapaxes(x_ref[0].astype(BF16), 0, 1)
    w = w_ref[...]
    for j in range(g):
        a_ref[j] = jnp.dot(w, xs[j], preferred_element_type=F32).astype(a_ref.dtype)
    o_ref[0] = jnp.swapaxes(a_ref[...], 0, 1)


def _dft1(w, x4, *, g, ct, name):
    nb, hq, r, c = x4.shape
    mo = w.shape[0]
    return pl.pallas_call(
        _dft1_kernel,
        grid=(nb, r // g, c // ct),
        in_specs=[pl.BlockSpec((mo, hq), lambda b, i, j: (0, 0)),
                  pl.BlockSpec((1, hq, g, ct), lambda b, i, j: (b, 0, i, j))],
        out_specs=pl.BlockSpec((1, mo, g, ct), lambda b, i, j: (b, 0, i, j)),
        out_shape=jax.ShapeDtypeStruct((nb, mo, r, c), BF16),
        scratch_shapes=[pltpu.VMEM((g, mo, ct), BF16)],
        compiler_params=_params("parallel", "parallel", "parallel"),
        name=name,
    )(w, x4)


def _filter_dft1_kernel(w_ref, h3_ref, t_ref, wo_ref, dl_ref, o_ref, l1_ref, a_ref, *, d_b):
    hq, g, e = h3_ref.shape
    ct = wo_ref.shape[1]
    j = pl.program_id(0)
    gi = pl.program_id(1)
    h3 = jnp.swapaxes(h3_ref[...], 0, 1).reshape(g * hq, e)
    hw = _dot_bf16x3(h3, wo_ref[...])
    hw = hw * jnp.exp(-t_ref[...].reshape(g * hq, 1) * dl_ref[...])
    row = lax.broadcasted_iota(jnp.int32, hw.shape, 0)
    hw = jnp.where((row == 0) & (gi == 0) & (j * ct >= d_b), 0.0, hw)

    @pl.when(gi == 0)
    def _():
        l1_ref[...] = jnp.zeros_like(l1_ref)

    l1_ref[...] += jnp.sum(jnp.abs(hw), axis=0, keepdims=True)
    x3 = hw.reshape(g, hq, ct).astype(BF16)
    w = w_ref[...]
    for jj in range(g):
        a_ref[jj] = jnp.dot(w, x3[jj], preferred_element_type=F32).astype(a_ref.dtype)
    o_ref[0] = jnp.swapaxes(a_ref[...], 0, 1)


def _filter_dft1(w, h3, t_sw, wout, deltas2, *, d_b, g, ct):
    hq, r, e = h3.shape
    c2 = wout.shape[1]
    mo = w.shape[0]
    return pl.pallas_call(
        functools.partial(_filter_dft1_kernel, d_b=d_b),
        grid=(c2 // ct, r // g),
        in_specs=[pl.BlockSpec((mo, hq), lambda j, i: (0, 0)),
                  pl.BlockSpec((hq, g, e), lambda j, i: (0, i, 0)),
                  pl.BlockSpec((g, hq, 1), lambda j, i: (i, 0, 0)),
                  pl.BlockSpec((e, ct), lambda j, i: (0, j)),
                  pl.BlockSpec((1, ct), lambda j, i: (0, j))],
        out_specs=[pl.BlockSpec((1, mo, g, ct), lambda j, i: (0, 0, i, j)),
                   pl.BlockSpec((1, ct), lambda j, i: (0, j))],
        out_shape=[jax.ShapeDtypeStruct((1, mo, r, c2), BF16), jax.ShapeDtypeStruct((1, c2), F32)],
        scratch_shapes=[pltpu.VMEM((g, mo, ct), BF16)],
        compiler_params=_params("parallel", "arbitrary"),
        name="filter_dft1",
    )(w, h3, t_sw, wout, deltas2)


def _filter_spec_kernel(af_ref, ab_ref, mf_ref, l1f_ref, l1b_ref, k_ref):
    kb = mf_ref.shape[0]
    r = mf_ref.shape[1] // 2
    ct = k_ref.shape[2]
    scale = 1.0 / (l1f_ref[...] + l1b_ref[...] + EPS)

    for kk in range(kb):
        mfk = mf_ref[kk]
        xf = jnp.dot(mfk, af_ref[0, :, kk].reshape(2 * r, ct), preferred_element_type=F32)
        xb = jnp.dot(mfk, ab_ref[0, :, kk].reshape(2 * r, ct), preferred_element_type=F32)
        k_ref[kk, 0:r, :] = ((xf[:r] + xb[:r]) * scale).astype(k_ref.dtype)
        k_ref[kk, r:, :] = ((xf[r:] - xb[r:]) * scale).astype(k_ref.dtype)


def _filter_spec(ak5, mf, l1, *, kb, ct):
    _, _, hq, r, c2 = ak5.shape
    c = c2 // 2
    nct = c // ct
    return pl.pallas_call(
        _filter_spec_kernel,
        grid=(hq // kb, nct),
        in_specs=[pl.BlockSpec((1, 2, kb, r, ct), lambda i, j: (0, 0, i, 0, j)),
                  pl.BlockSpec((1, 2, kb, r, ct), lambda i, j: (0, 0, i, 0, j + nct)),
                  pl.BlockSpec((kb, 2 * r, 2 * r), lambda i, j: (i, 0, 0)),
                  pl.BlockSpec((1, ct), lambda i, j: (0, j)),
                  pl.BlockSpec((1, ct), lambda i, j: (0, j + nct))],
        out_specs=pl.BlockSpec((kb, 2 * r, ct), lambda i, j: (i, 0, j)),
        out_shape=jax.ShapeDtypeStruct((hq, 2 * r, c), BF16),
        compiler_params=_params("parallel", "parallel"),
        name="filter_spec",
    )(ak5, ak5, mf, l1, l1)


def _spec_conv_kernel(a_ref, mf_ref, mi_ref, k_ref, dr_ref, di_ref, d_scr):
    kb = mf_ref.shape[0]
    r = mf_ref.shape[1] // 2
    ct = k_ref.shape[2]

    for kk in range(kb):
        a = a_ref[0, :, kk].reshape(2 * r, ct)
        x = jnp.dot(mf_ref[kk], a, preferred_element_type=F32)
        xr, xi = x[:r].astype(BF16), x[r:].astype(BF16)
        kr, ki = k_ref[kk, 0:r, :], k_ref[kk, r:, :]
        y = jnp.concatenate([xr * kr - xi * ki, xr * ki + xi * kr], axis=0)
        dd = jnp.dot(mi_ref[kk], y, preferred_element_type=F32)
        d_scr[0, kk] = dd[:r].astype(d_scr.dtype)
        d_scr[1, kk] = dd[r:].astype(d_scr.dtype)
    dr_ref[0] = jnp.swapaxes(d_scr[0], 0, 1)
    di_ref[0] = jnp.swapaxes(d_scr[1], 0, 1)


def _spec_conv(a5, mf, mi, khat, *, kb, ct):
    nb, _, hq, r, c = a5.shape
    mspec = pl.BlockSpec((kb, 2 * r, 2 * r), lambda i, j, b: (i, 0, 0))
    ospec = pl.BlockSpec((1, r, kb, ct), lambda i, j, b: (b, 0, i, j))
    oshape = jax.ShapeDtypeStruct((nb, r, hq, c), BF16)
    return pl.pallas_call(
        _spec_conv_kernel,
        grid=(hq // kb, c // ct, nb),
        in_specs=[pl.BlockSpec((1, 2, kb, r, ct), lambda i, j, b: (b, 0, i, 0, j)), mspec, mspec,
                  pl.BlockSpec((kb, 2 * r, ct), lambda i, j, b: (i, 0, j))],
        out_specs=[ospec, ospec],
        out_shape=[oshape, oshape],
        scratch_shapes=[pltpu.VMEM((2, kb, r, ct), BF16)],
        compiler_params=_params("parallel", "parallel", "arbitrary"),
        name="spec_conv",
    )(a5, mf, mi, khat)


def _conv_out_kernel(wr_ref, wi_ref, dr_ref, di_ref, u_ref, x0_ref, ds_ref, o_ref, y_scr):
    g = dr_ref.shape[1]
    wr, wi = wr_ref[...], wi_ref[...]
    for j in range(g):
        y_scr[j] = (jnp.dot(wr, dr_ref[0, j], preferred_element_type=F32)
                    + jnp.dot(wi, di_ref[0, j], preferred_element_type=F32))
    yconv = jnp.swapaxes(y_scr[...], 0, 1)
    y_b = x0_ref[0].astype(F32) * (yconv + u_ref[0].astype(F32) * ds_ref[...])
    o_ref[0] = y_b.astype(o_ref.dtype)


def _conv_out(wr, wi, dr, di, u4, x04, ds, *, g, ct):
    nb, r, hq, c = dr.shape
    dspec = pl.BlockSpec((1, g, hq, ct), lambda b, i, j: (b, i, 0, j))
    nspec = pl.BlockSpec((1, hq, g, ct), lambda b, i, j: (b, 0, i, j))
    wspec = pl.BlockSpec((hq, hq), lambda b, i, j: (0, 0))
    return pl.pallas_call(
        _conv_out_kernel,
        grid=(nb, r // g, c // ct),
        in_specs=[wspec, wspec, dspec, dspec, nspec, nspec,
                  pl.BlockSpec((1, ct), lambda b, i, j: (0, j))],
        out_specs=nspec,
        out_shape=jax.ShapeDtypeStruct((nb, hq, r, c), BF16),
        scratch_shapes=[pltpu.VMEM((g, hq, ct), F32)],
        compiler_params=_params("parallel", "parallel", "parallel"),
        name="conv_out",
    )(wr, wi, dr, di, u4, x04, ds)


def _out_proj_kernel(ma_ref, yb_ref, x_ref, gb_ref, wa_ref, wb_ref, g2_ref, x1_ref, h2_ref,
                     wabf_ref, wbbf_ref):
    @pl.when(pl.program_id(0) == 0)
    def _():
        wabf_ref[...] = wa_ref[...].astype(BF16)
        wbbf_ref[...] = wb_ref[...].astype(BF16)

    mb = _rms(yb_ref[...].astype(F32), gb_ref[...]).astype(BF16)
    x1 = (x_ref[...]
          + jnp.dot(ma_ref[...], wabf_ref[...], preferred_element_type=F32)
          + jnp.dot(mb, wbbf_ref[...], preferred_element_type=F32))
    x1_ref[...] = x1
    h2_ref[...] = _rms(x1, g2_ref[...]).astype(h2_ref.dtype)


def _out_proj(ma, yb, x2d, gb, w_out, g2, *, tm):
    m, d = x2d.shape
    d_a = ma.shape[1]
    d_b = yb.shape[1]
    row = lambda w: pl.BlockSpec((tm, w), lambda i: (i, 0))
    return pl.pallas_call(
        _out_proj_kernel,
        grid=(m // tm,),
        in_specs=[row(d_a), row(d_b), row(d), pl.BlockSpec((1, d_b), lambda i: (0, 0)),
                  pl.BlockSpec((d_a, d), lambda i: (0, 0), pipeline_mode=pl.Buffered(1)),
                  pl.BlockSpec((d_b, d), lambda i: (d_a // d_b, 0), pipeline_mode=pl.Buffered(1)),
                  pl.BlockSpec((1, d), lambda i: (0, 0))],
        out_specs=[row(d), row(d)],
        out_shape=[jax.ShapeDtypeStruct((m, d), F32), jax.ShapeDtypeStruct((m, d), BF16)],
        scratch_shapes=[pltpu.VMEM((d_a, d), BF16), pltpu.VMEM((d_b, d), BF16)],
        compiler_params=_params("arbitrary"),
        name="out_proj",
    )(ma, yb, x2d, gb, w_out, w_out, g2)


def _ffn_kernel(hp_ref, h_ref, hn_ref, x1_hbm, wg_ref, wv_ref, cw_ref, cb_ref, wd_ref, gf_ref,
                o_ref, hext_ref, x1_sem, *, seq_tiles, final_norm, d_ff):
    tm = h_ref.shape[0]
    tf = wg_ref.shape[1]
    i = pl.program_id(0)
    f = pl.program_id(1)
    x1_copy = pltpu.make_async_copy(x1_hbm.at[pl.ds(i * tm, tm), :], o_ref, x1_sem)

    nf = pl.cdiv(d_ff, tf)
    assert nf >= 2
    dup = nf * tf - d_ff

    def hidden_tile(first, last):
        cols = slice(dup, tf) if last else slice(0, tf)
        gext = jnp.dot(hext_ref[...], wg_ref[:, cols].astype(BF16), preferred_element_type=F32)
        g = _conv3(gext, cw_ref[:, cols], cb_ref[:, cols], tm)
        val = jnp.dot(h_ref[...], wv_ref[:, cols].astype(BF16), preferred_element_type=F32)
        act = (_gelu(g) * val).astype(BF16)
        if first:
            x1_copy.wait()
        o_ref[...] += jnp.dot(act, wd_ref[cols, :].astype(BF16), preferred_element_type=F32)

    @pl.when(f == 0)
    def _():
        x1_copy.start()
        _fill_ext(hext_ref, hp_ref[...], h_ref[...], hn_ref[...], i, seq_tiles)
        hidden_tile(True, False)

    @pl.when((f > 0) & (f < nf - 1))
    def _():
        hidden_tile(False, False)

    @pl.when(f == nf - 1)
    def _():
        hidden_tile(False, True)
        if final_norm:
            o_ref[...] = _rms(o_ref[...], gf_ref[...])


def _ragged_start(j, tf, d_ff, base=0):
    return (base // LANE + jnp.minimum(j * (tf // LANE), (d_ff - tf) // LANE)) * LANE


def _ffn(h2, x1, w_up, cw, cb, wd, gf, *, seq_len, tm, tf, final_norm):
    m, d = x1.shape
    d_ff = wd.shape[0]
    assert d_ff % LANE == 0 and tf % LANE == 0 and d_ff >= tf
    el = pl.Element
    start = functools.partial(_ragged_start, tf=tf, d_ff=d_ff)
    return pl.pallas_call(
        functools.partial(_ffn_kernel, seq_tiles=seq_len // tm, final_norm=final_norm, d_ff=d_ff),
        grid=(m // tm, pl.cdiv(d_ff, tf)),
        in_specs=_halo_specs(tm, d, m) + [
            pl.BlockSpec(memory_space=pl.ANY),
            pl.BlockSpec((el(d), el(tf)), lambda i, j: (0, start(j))),
            pl.BlockSpec((el(d), el(tf)), lambda i, j: (0, start(j, base=d_ff))),
            pl.BlockSpec((el(3), el(tf)), lambda i, j: (0, start(j))),
            pl.BlockSpec((el(1), el(tf)), lambda i, j: (0, start(j))),
            pl.BlockSpec((el(tf), el(d)), lambda i, j: (start(j), 0)),
            pl.BlockSpec((1, d), lambda i, j: (0, 0)),
        ],
        out_specs=pl.BlockSpec((tm, d), lambda i, j: (i, 0)),
        out_shape=jax.ShapeDtypeStruct((m, d), F32),
        scratch_shapes=[pltpu.VMEM((tm + 2 * HALO, d), BF16), pltpu.SemaphoreType.DMA(())],
        compiler_params=_params("parallel", "arbitrary"),
        name="ffn",
    )(h2, h2, h2, x1, w_up, w_up, cw, cb, wd, gf)


@functools.lru_cache(maxsize=None)
def _dft_constants(seq_len):
    n = 2 * seq_len
    r = DFT_R
    q = n // r
    hq = q // 2
    k1 = np.arange(hq, dtype=np.float64)[:, None]
    n1 = np.arange(hq, dtype=np.float64)[None, :]
    th1 = 2.0 * np.pi * n1 * (k1 + 0.5) / q
    f1r, f1i = np.cos(th1), -np.sin(th1)
    f1s = np.concatenate([f1r, f1i], axis=0)
    f1rt, f1it = (2.0 / n) * f1r.T, (2.0 / n) * f1i.T
    k2 = np.arange(r, dtype=np.float64)[None, :, None]
    n2 = np.arange(r, dtype=np.float64)[None, None, :]
    kk = np.arange(hq, dtype=np.float64)[:, None, None]
    th2 = 2.0 * np.pi * (n2 * k2 / r + n2 * (kk + 0.5) / n)
    gr, gi = np.cos(th2), -np.sin(th2)
    mf = np.concatenate([np.concatenate([gr, -gi], axis=2),
                         np.concatenate([gi, gr], axis=2)], axis=1)
    grt, git = gr.transpose(0, 2, 1), gi.transpose(0, 2, 1)
    mi = np.concatenate([np.concatenate([grt, git], axis=2),
                         np.concatenate([-git, grt], axis=2)], axis=1)
    return f1s, f1rt, f1it, mf, mi


def _positional_features(seq_len, e_pad):
    t = jnp.linspace(0.0, 1.0, seq_len, dtype=F32)[:, None]
    w = (2.0 * math.pi / seq_len) * jnp.arange(seq_len, dtype=F32)[:, None]
    f = jnp.linspace(1e-4, N_BANDS - 1, N_BANDS, dtype=F32)[None]
    z = jnp.concatenate([t, jnp.cos(f * w), -jnp.sin(f * w)], axis=-1)
    return jnp.pad(z, ((0, 0), (0, e_pad - z.shape[1])))


def kernel(x, norm1_g, w_in, sgu_norm_g, sgu_w, sgu_b, hy_conv_w, hy_conv_b, hy_f_w1, hy_f_b1,
           hy_f_w2, hy_f_b2, hy_f_w3, hy_f_b3, hy_f_freq, hy_f_wout, hy_d_skip, outnorm_a_g,
           outnorm_b_g, w_out, norm2_g, ffn_w_up, ffn_dw_w, ffn_dw_b, ffn_w_down, final_g):
    nb, seq_len, d = x.shape
    depth = w_in.shape[0]
    d_a = sgu_norm_g.shape[1]
    d_b = hy_d_skip.shape[1]
    d_ff = ffn_dw_b.shape[1]
    m = nb * seq_len
    r = DFT_R
    hq = seq_len // r
    g = DFT_G
    kb = min(DFT_G, hq)
    tm = 512
    tf = 512

    f1s, f1rt, f1it, mf, mi = (jnp.asarray(a, F32).astype(BF16) for a in _dft_constants(seq_len))
    e_pad = hy_f_w2.shape[1]
    z = _positional_features(seq_len, e_pad)
    max_decay = math.log(DECAY_TARGET) / FAST_DECAY_PCT
    min_decay = math.log(DECAY_TARGET) / SLOW_DECAY_PCT
    deltas = jnp.abs(jnp.linspace(min_decay, max_decay, d_b, dtype=F32))
    deltas2 = jnp.concatenate([deltas, deltas])[None]

    x2d = x.reshape(m, d)
    row = lambda v: v.reshape(1, -1)
    for l in range(depth):
        w_in_l = w_in[l]
        g1 = row(norm1_g[l])

        bias_full = jnp.repeat(sgu_b[l].T, d_a // N_HEADS_A, axis=1)
        ma, h1, w_up_bf = _in_proj_a(x2d, g1, w_in_l, row(sgu_norm_g[l]), sgu_w[l].astype(BF16),
                                     bias_full, row(outnorm_a_g[l]), ffn_w_up[l], tm=512)

        x0, u = _in_proj_b(h1, w_in_l, hy_conv_w[l], row(hy_conv_b[l]),
                           col0=2 * d_a, d_b=d_b, seq_len=seq_len, tm=1024, tn=512)
        u4 = u.reshape(nb, hq, r, d_b)
        x04 = x0.reshape(nb, hq, r, d_b)

        w1p = jnp.pad(hy_f_w1[l], ((0, e_pad - hy_f_w1.shape[1]), (0, 0)))
        h3 = _filter_feat(z, w1p, row(hy_f_b1[l]), hy_f_w2[l], row(hy_f_b2[l]), hy_f_w3[l],
                          row(hy_f_b3[l]), row(hy_f_freq[l]), tl=min(512, seq_len // 2))
        t_sw = z[:, 0].reshape(hq, r).T[:, :, None]
        ak, l1 = _filter_dft1(f1s, h3.reshape(hq, r, e_pad), t_sw, hy_f_wout[l], deltas2,
                              d_b=d_b, g=g, ct=512)
        khat = _filter_spec(ak.reshape(1, 2, hq, r, 2 * d_b), mf, l1, kb=kb, ct=256)

        a = _dft1(f1s, u4, g=g, ct=512, name="u_dft1")
        dr, di = _spec_conv(a.reshape(nb, 2, hq, r, d_b), mf, mi, khat, kb=kb, ct=256)
        yb = _conv_out(f1rt, f1it, dr, di, u4, x04, row(hy_d_skip[l]), g=g, ct=512)

        x1, h2 = _out_proj(ma, yb.reshape(m, d_b), x2d, row(outnorm_b_g[l]),
                           w_out[l], row(norm2_g[l]), tm=tm)

        x2d = _ffn(h2, x1, w_up_bf, ffn_dw_w[l], row(ffn_dw_b[l]),
                   ffn_w_down[l], row(final_g), seq_len=seq_len, tm=1024, tf=tf,
                   final_norm=(l == depth - 1))
    return x2d.reshape(nb, seq_len, d)
```

```python
import functools
import math

import numpy as np
import jax
import jax.numpy as jnp
from jax import lax
from jax.experimental import pallas as pl
from jax.experimental.pallas import tpu as pltpu

EPS = 1e-6
CHUNK = 128
N_HEADS_A = 8
DECAY_TARGET = 1e-2
FAST_DECAY_PCT = 0.3
SLOW_DECAY_PCT = 1.5
N_BANDS = 16

LANE = 128
BF16_SUBLANES = 16
DFT_R = 128
DFT_G = BF16_SUBLANES
HALO = BF16_SUBLANES
V7X_VMEM_BYTES = 64 * 1024 * 1024
VMEM_LIMIT = V7X_VMEM_BYTES * 7 // 8

TM_IN_A = 512
TM_IN_B = 1024
TN_IN_B = 512
TM_OUT = 512
TM_FFN = 1024
TF_FFN = 512
CT_DFT = 1024
CT_SPEC = 256
TL_FEAT = 512

F32 = jnp.float32
BF16 = jnp.bfloat16


def _gelu(x):
    return 0.5 * x * (1.0 + lax.erf(x * np.float32(math.sqrt(0.5))))


def _rms(x, g):
    return x * lax.rsqrt(jnp.mean(x * x, axis=-1, keepdims=True) + EPS) * g


def _params(*sem):
    return pltpu.CompilerParams(dimension_semantics=sem, vmem_limit_bytes=VMEM_LIMIT)


def _in_proj_a_kernel(x_ref, g1_ref, w_ref, gs_ref, ws_ref, bias_ref, go_ref, side_ref,
                      o_ref, h_ref, side_bf_ref, ya_ref, wbf_ref):
    tm = x_ref.shape[0]
    d_a = o_ref.shape[1]
    hd = d_a // N_HEADS_A
    side_bf_ref[...] = side_ref[...].astype(BF16)

    @pl.when(pl.program_id(0) == 0)
    def _():
        wbf_ref[...] = w_ref[...].astype(BF16)

    h = _rms(x_ref[...], g1_ref[...]).astype(BF16)
    h_ref[...] = h
    pv = jnp.dot(h, wbf_ref[:, d_a:], preferred_element_type=F32)
    zv = _rms(_gelu(pv), gs_ref[...]).astype(BF16)
    zu = _gelu(jnp.dot(h, wbf_ref[:, :d_a], preferred_element_type=F32))
    for c in range(tm // CHUNK):
        rows = slice(c * CHUNK, (c + 1) * CHUNK)
        for hh in range(N_HEADS_A):
            cols = slice(hh * hd, (hh + 1) * hd)
            s = jnp.dot(ws_ref[hh], zv[rows, cols], preferred_element_type=F32)
            ya_ref[rows, cols] = zu[rows, cols] * (s + bias_ref[:, cols])
    o_ref[...] = _rms(ya_ref[...], go_ref[...]).astype(o_ref.dtype)


def _in_proj_a(x2d, g1, w_a, gs, ws, bias_full, go, side, *, tm):
    m, d = x2d.shape
    d_a = gs.shape[1]
    side_rows = side.shape[0] // (m // tm)
    assert side_rows * (m // tm) == side.shape[0] and side_rows % 16 == 0
    side_spec = pl.BlockSpec((side_rows, side.shape[1]), lambda i: (i, 0))
    return pl.pallas_call(
        _in_proj_a_kernel,
        grid=(m // tm,),
        in_specs=[
            pl.BlockSpec((tm, d), lambda i: (i, 0)),
            pl.BlockSpec((1, d), lambda i: (0, 0)),
            pl.BlockSpec((d, 2 * d_a), lambda i: (0, 0), pipeline_mode=pl.Buffered(1)),
            pl.BlockSpec((1, d_a), lambda i: (0, 0)),
            pl.BlockSpec((N_HEADS_A, CHUNK, CHUNK), lambda i: (0, 0, 0)),
            pl.BlockSpec((CHUNK, d_a), lambda i: (0, 0)),
            pl.BlockSpec((1, d_a), lambda i: (0, 0)),
            side_spec,
        ],
        out_specs=[pl.BlockSpec((tm, d_a), lambda i: (i, 0)), pl.BlockSpec((tm, d), lambda i: (i, 0)),
                   side_spec],
        out_shape=[jax.ShapeDtypeStruct((m, d_a), BF16), jax.ShapeDtypeStruct((m, d), BF16),
                   jax.ShapeDtypeStruct(side.shape, BF16)],
        scratch_shapes=[pltpu.VMEM((tm, d_a), F32), pltpu.VMEM((d, 2 * d_a), BF16)],
        compiler_params=_params("arbitrary"),
        name="in_proj_a",
    )(x2d, g1, w_a, gs, ws, bias_full, go, side)


def _fill_ext(hext_ref, prev, main, nxt, i, seq_tiles):
    tm = main.shape[0]
    has_prev = (i % seq_tiles) != 0
    has_next = ((i + 1) % seq_tiles) != 0
    hext_ref[0:HALO, :] = jnp.where(has_prev, prev, jnp.zeros_like(prev))
    hext_ref[HALO:HALO + tm, :] = main
    hext_ref[HALO + tm:, :] = jnp.where(has_next, nxt, jnp.zeros_like(nxt))


def _conv3(gext, w_ref, b_ref, tm):
    n = gext.shape[0]
    prev = pltpu.roll(gext, 1, 0)[HALO:HALO + tm]
    nxt = pltpu.roll(gext, n - 1, 0)[HALO:HALO + tm]
    cur = gext[HALO:HALO + tm]
    return prev * w_ref[0:1, :] + cur * w_ref[1:2, :] + nxt * w_ref[2:3, :] + b_ref[...]


def _in_proj_b_kernel(hp_ref, h_ref, hn_ref, w0_ref, w1_ref, w2_ref,
                      cw0_ref, cw1_ref, cw2_ref, cb0_ref, cb1_ref, cb2_ref,
                      x0_ref, u_ref, hext_ref, *, seq_tiles):
    tm = h_ref.shape[0]
    i = pl.program_id(0)

    @pl.when(pl.program_id(1) == 0)
    def _():
        _fill_ext(hext_ref, hp_ref[...], h_ref[...], hn_ref[...], i, seq_tiles)

    hext = hext_ref[...]

    def branch(w_ref, cw_ref, cb_ref):
        gext = jnp.dot(hext, w_ref[...].astype(BF16), preferred_element_type=F32)
        return _conv3(gext, cw_ref, cb_ref, tm)

    x0_ref[...] = branch(w0_ref, cw0_ref, cb0_ref).astype(x0_ref.dtype)
    x1 = branch(w1_ref, cw1_ref, cb1_ref)
    v = branch(w2_ref, cw2_ref, cb2_ref)
    u_ref[...] = (v * x1).astype(u_ref.dtype)


def _halo_specs(tm, d, m):
    blocks_per_tile = tm // HALO
    last = m // HALO - 1
    return [
        pl.BlockSpec((HALO, d), lambda i, j: (jnp.maximum(i * blocks_per_tile - 1, 0), 0)),
        pl.BlockSpec((tm, d), lambda i, j: (i, 0)),
        pl.BlockSpec((HALO, d), lambda i, j: (jnp.minimum((i + 1) * blocks_per_tile, last), 0)),
    ]


def _in_proj_b(h, w_in, conv_w, conv_b, *, col0, d_b, seq_len, tm, tn):
    m, d = h.shape
    nj = d_b // tn
    w_specs = [pl.BlockSpec((d, tn), functools.partial(
        lambda i, j, off: (0, off + j), off=(col0 + g * d_b) // tn)) for g in range(3)]
    cw_specs = [pl.BlockSpec((3, tn), functools.partial(
        lambda i, j, off: (0, off + j), off=(g * d_b) // tn)) for g in range(3)]
    cb_specs = [pl.BlockSpec((1, tn), functools.partial(
        lambda i, j, off: (0, off + j), off=(g * d_b) // tn)) for g in range(3)]
    out_spec = pl.BlockSpec((tm, tn), lambda i, j: (i, j))
    return pl.pallas_call(
        functools.partial(_in_proj_b_kernel, seq_tiles=seq_len // tm),
        grid=(m // tm, nj),
        in_specs=_halo_specs(tm, d, m) + w_specs + cw_specs + cb_specs,
        out_specs=[out_spec, out_spec],
        out_shape=[jax.ShapeDtypeStruct((m, d_b), BF16)] * 2,
        scratch_shapes=[pltpu.VMEM((tm + 2 * HALO, d), BF16)],
        compiler_params=_params("parallel", "arbitrary"),
        name="in_proj_b",
    )(h, h, h, w_in, w_in, w_in, conv_w, conv_w, conv_w, conv_b, conv_b, conv_b)


def _filter_feat_kernel(z_ref, w1_ref, b1_ref, w2_ref, b2_ref, w3_ref, b3_ref, fr_ref, h_ref):
    hi = lax.Precision.HIGHEST
    fr = fr_ref[...]
    h = z_ref[...]
    for w_ref, b_ref in ((w1_ref, b1_ref), (w2_ref, b2_ref), (w3_ref, b3_ref)):
        h = jnp.sin(fr * (jnp.dot(h, w_ref[...], precision=hi, preferred_element_type=F32) + b_ref[...]))
    h_ref[...] = h


def _filter_feat(z, w1, b1, w2, b2, w3, b3, fr, *, tl):
    seq_len, e = z.shape
    pack = LANE // e
    assert e * pack == LANE and seq_len % (pack * tl) == 0
    eye = jnp.eye(pack, dtype=F32)
    bd = lambda w: jnp.kron(eye, w)
    rep = lambda v: jnp.tile(v, (1, pack))
    full = lambda shape: pl.BlockSpec(shape, lambda i: (0, 0))
    rows = seq_len // pack
    h = pl.pallas_call(
        _filter_feat_kernel,
        grid=(rows // tl,),
        in_specs=[pl.BlockSpec((tl, LANE), lambda i: (i, 0)), full((LANE, LANE)), full((1, LANE)),
                  full((LANE, LANE)), full((1, LANE)), full((LANE, LANE)), full((1, LANE)),
                  full((1, LANE))],
        out_specs=pl.BlockSpec((tl, LANE), lambda i: (i, 0)),
        out_shape=jax.ShapeDtypeStruct((rows, LANE), F32),
        compiler_params=_params("parallel"),
        name="filter_feat",
    )(z.reshape(rows, LANE), bd(w1), rep(b1), bd(w2), rep(b2), bd(w3), rep(b3), rep(fr))
    return h.reshape(seq_len, e)


def _dot_bf16x3(a, b):
    a_hi = a.astype(BF16)
    b_hi = b.astype(BF16)
    a_lo = (a - a_hi.astype(F32)).astype(BF16)
    b_lo = (b - b_hi.astype(F32)).astype(BF16)
    dot = functools.partial(jnp.dot, preferred_element_type=F32)
    return dot(a_hi, b_hi) + dot(a_lo, b_hi) + dot(a_hi, b_lo)


def _dft1_kernel(w_ref, x_ref, o_ref, a_ref):
    g = x_ref.shape[2]
    xs = jnp.swapaxes(x_ref[0].astype(BF16), 0, 1)
    w = w_ref[...]
    for j in range(g):
        a_ref[j] = jnp.dot(w, xs[j], preferred_element_type=F32).astype(a_ref.dtype)
    o_ref[0] = jnp.swapaxes(a_ref[...], 0, 1)


def _dft1(w, x4, *, g, ct, name):
    nb, hq, r, c = x4.shape
    mo = w.shape[0]
    return pl.pallas_call(
        _dft1_kernel,
        grid=(nb, r // g, c // ct),
        in_specs=[pl.BlockSpec((mo, hq), lambda b, i, j: (0, 0)),
                  pl.BlockSpec((1, hq, g, ct), lambda b, i, j: (b, 0, i, j))],
        out_specs=pl.BlockSpec((1, mo, g, ct), lambda b, i, j: (b, 0, i, j)),
        out_shape=jax.ShapeDtypeStruct((nb, mo, r, c), BF16),
        scratch_shapes=[pltpu.VMEM((g, mo, ct), BF16)],
        compiler_params=_params("parallel", "parallel", "parallel"),
        name=name,
    )(w, x4)


def _filter_dft1_kernel(w_ref, h3_ref, t_ref, wo_ref, dl_ref, o_ref, l1_ref, a_ref, *, d_b):
    hq, g, e = h3_ref.shape
    ct = wo_ref.shape[1]
    j = pl.program_id(0)
    gi = pl.program_id(1)
    h3 = jnp.swapaxes(h3_ref[...], 0, 1).reshape(g * hq, e)
    hw = _dot_bf16x3(h3, wo_ref[...])
    hw = hw * jnp.exp(-t_ref[...].reshape(g * hq, 1) * dl_ref[...])
    row = lax.broadcasted_iota(jnp.int32, hw.shape, 0)
    hw = jnp.where((row == 0) & (gi == 0) & (j * ct >= d_b), 0.0, hw)

    @pl.when(gi == 0)
    def _():
        l1_ref[...] = jnp.zeros_like(l1_ref)

    l1_ref[...] += jnp.sum(jnp.abs(hw), axis=0, keepdims=True)
    x3 = hw.reshape(g, hq, ct).astype(BF16)
    w = w_ref[...]
    for jj in range(g):
        a_ref[jj] = jnp.dot(w, x3[jj], preferred_element_type=F32).astype(a_ref.dtype)
    o_ref[0] = jnp.swapaxes(a_ref[...], 0, 1)


def _filter_dft1(w, h3, t_sw, wout, deltas2, *, d_b, g, ct):
    hq, r, e = h3.shape
    c2 = wout.shape[1]
    mo = w.shape[0]
    return pl.pallas_call(
        functools.partial(_filter_dft1_kernel, d_b=d_b),
        grid=(c2 // ct, r // g),
        in_specs=[pl.BlockSpec((mo, hq), lambda j, i: (0, 0)),
                  pl.BlockSpec((hq, g, e), lambda j, i: (0, i, 0)),
                  pl.BlockSpec((g, hq, 1), lambda j, i: (i, 0, 0)),
                  pl.BlockSpec((e, ct), lambda j, i: (0, j)),
                  pl.BlockSpec((1, ct), lambda j, i: (0, j))],
        out_specs=[pl.BlockSpec((1, mo, g, ct), lambda j, i: (0, 0, i, j)),
                   pl.BlockSpec((1, ct), lambda j, i: (0, j))],
        out_shape=[jax.ShapeDtypeStruct((1, mo, r, c2), BF16), jax.ShapeDtypeStruct((1, c2), F32)],
        scratch_shapes=[pltpu.VMEM((g, mo, ct), BF16)],
        compiler_params=_params("parallel", "arbitrary"),
        name="filter_dft1",
    )(w, h3, t_sw, wout, deltas2)


def _filter_spec_kernel(af_ref, ab_ref, mf_ref, l1f_ref, l1b_ref, k_ref):
    kb = mf_ref.shape[0]
    r = mf_ref.shape[1] // 2
    ct = k_ref.shape[2]
    scale = 1.0 / (l1f_ref[...] + l1b_ref[...] + EPS)

    for kk in range(kb):
        mfk = mf_ref[kk]
        xf = jnp.dot(mfk, af_ref[0, :, kk].reshape(2 * r, ct), preferred_element_type=F32)
        xb = jnp.dot(mfk, ab_ref[0, :, kk].reshape(2 * r, ct), preferred_element_type=F32)
        k_ref[kk, 0:r, :] = ((xf[:r] + xb[:r]) * scale).astype(k_ref.dtype)
        k_ref[kk, r:, :] = ((xf[r:] - xb[r:]) * scale).astype(k_ref.dtype)


def _filter_spec(ak5, mf, l1, *, kb, ct):
    _, _, hq, r, c2 = ak5.shape
    c = c2 // 2
    nct = c // ct
    return pl.pallas_call(
        _filter_spec_kernel,
        grid=(hq // kb, nct),
        in_specs=[pl.BlockSpec((1, 2, kb, r, ct), lambda i, j: (0, 0, i, 0, j)),
                  pl.BlockSpec((1, 2, kb, r, ct), lambda i, j: (0, 0, i, 0, j + nct)),
                  pl.BlockSpec((kb, 2 * r, 2 * r), lambda i, j: (i, 0, 0)),
                  pl.BlockSpec((1, ct), lambda i, j: (0, j)),
                  pl.BlockSpec((1, ct), lambda i, j: (0, j + nct))],
        out_specs=pl.BlockSpec((kb, 2 * r, ct), lambda i, j: (i, 0, j)),
        out_shape=jax.ShapeDtypeStruct((hq, 2 * r, c), BF16),
        compiler_params=_params("parallel", "parallel"),
        name="filter_spec",
    )(ak5, ak5, mf, l1, l1)


def _spec_conv_kernel(a_ref, mf_ref, mi_ref, k_ref, dr_ref, di_ref, d_scr):
    kb = mf_ref.shape[0]
    r = mf_ref.shape[1] // 2
    ct = k_ref.shape[2]

    for kk in range(kb):
        a = a_ref[0, :, kk].reshape(2 * r, ct)
        x = jnp.dot(mf_ref[kk], a, preferred_element_type=F32)
        xr, xi = x[:r].astype(BF16), x[r:].astype(BF16)
        kr, ki = k_ref[kk, 0:r, :], k_ref[kk, r:, :]
        y = jnp.concatenate([xr * kr - xi * ki, xr * ki + xi * kr], axis=0)
        dd = jnp.dot(mi_ref[kk], y, preferred_element_type=F32)
        d_scr[0, kk] = dd[:r].astype(d_scr.dtype)
        d_scr[1, kk] = dd[r:].astype(d_scr.dtype)
    dr_ref[0] = jnp.swapaxes(d_scr[0], 0, 1)
    di_ref[0] = jnp.swapaxes(d_scr[1], 0, 1)


def _spec_conv(a5, mf, mi, khat, *, kb, ct):
    nb, _, hq, r, c = a5.shape
    mspec = pl.BlockSpec((kb, 2 * r, 2 * r), lambda i, j, b: (i, 0, 0))
    ospec = pl.BlockSpec((1, r, kb, ct), lambda i, j, b: (b, 0, i, j))
    oshape = jax.ShapeDtypeStruct((nb, r, hq, c), BF16)
    return pl.pallas_call(
        _spec_conv_kernel,
        grid=(hq // kb, c // ct, nb),
        in_specs=[pl.BlockSpec((1, 2, kb, r, ct), lambda i, j, b: (b, 0, i, 0, j)), mspec, mspec,
                  pl.BlockSpec((kb, 2 * r, ct), lambda i, j, b: (i, 0, j))],
        out_specs=[ospec, ospec],
        out_shape=[oshape, oshape],
        scratch_shapes=[pltpu.VMEM((2, kb, r, ct), BF16)],
        compiler_params=_params("parallel", "parallel", "arbitrary"),
        name="spec_conv",
    )(a5, mf, mi, khat)


def _conv_out_kernel(wr_ref, wi_ref, dr_ref, di_ref, u_ref, x0_ref, ds_ref, o_ref, y_scr):
    g = dr_ref.shape[1]
    wr, wi = wr_ref[...], wi_ref[...]
    for j in range(g):
        y_scr[j] = (jnp.dot(wr, dr_ref[0, j], preferred_element_type=F32)
                    + jnp.dot(wi, di_ref[0, j], preferred_element_type=F32))
    yconv = jnp.swapaxes(y_scr[...], 0, 1)
    y_b = x0_ref[0].astype(F32) * (yconv + u_ref[0].astype(F32) * ds_ref[...])
    o_ref[0] = y_b.astype(o_ref.dtype)


def _conv_out(wr, wi, dr, di, u4, x04, ds, *, g, ct):
    nb, r, hq, c = dr.shape
    dspec = pl.BlockSpec((1, g, hq, ct), lambda b, i, j: (b, i, 0, j))
    nspec = pl.BlockSpec((1, hq, g, ct), lambda b, i, j: (b, 0, i, j))
    wspec = pl.BlockSpec((hq, hq), lambda b, i, j: (0, 0))
    return pl.pallas_call(
        _conv_out_kernel,
        grid=(nb, r // g, c // ct),
        in_specs=[wspec, wspec, dspec, dspec, nspec, nspec,
                  pl.BlockSpec((1, ct), lambda b, i, j: (0, j))],
        out_specs=nspec,
        out_shape=jax.ShapeDtypeStruct((nb, hq, r, c), BF16),
        scratch_shapes=[pltpu.VMEM((g, hq, ct), F32)],
        compiler_params=_params("parallel", "parallel", "parallel"),
        name="conv_out",
    )(wr, wi, dr, di, u4, x04, ds)


def _out_proj_kernel(ma_ref, yb_ref, x_ref, gb_ref, wa_ref, wb_ref, g2_ref, x1_ref, h2_ref,
                     wabf_ref, wbbf_ref):
    @pl.when(pl.program_id(0) == 0)
    def _():
        wabf_ref[...] = wa_ref[...].astype(BF16)
        wbbf_ref[...] = wb_ref[...].astype(BF16)

    mb = _rms(yb_ref[...].astype(F32), gb_ref[...]).astype(BF16)
    x1 = (x_ref[...]
          + jnp.dot(ma_ref[...], wabf_ref[...], preferred_element_type=F32)
          + jnp.dot(mb, wbbf_ref[...], preferred_element_type=F32))
    x1_ref[...] = x1
    h2_ref[...] = _rms(x1, g2_ref[...]).astype(h2_ref.dtype)


def _out_proj(ma, yb, x2d, gb, w_out, g2, *, tm):
    m, d = x2d.shape
    d_a = ma.shape[1]
    d_b = yb.shape[1]
    row = lambda w: pl.BlockSpec((tm, w), lambda i: (i, 0))
    return pl.pallas_call(
        _out_proj_kernel,
        grid=(m // tm,),
        in_specs=[row(d_a), row(d_b), row(d), pl.BlockSpec((1, d_b), lambda i: (0, 0)),
                  pl.BlockSpec((d_a, d), lambda i: (0, 0), pipeline_mode=pl.Buffered(1)),
                  pl.BlockSpec((d_b, d), lambda i: (d_a // d_b, 0), pipeline_mode=pl.Buffered(1)),
                  pl.BlockSpec((1, d), lambda i: (0, 0))],
        out_specs=[row(d), row(d)],
        out_shape=[jax.ShapeDtypeStruct((m, d), F32), jax.ShapeDtypeStruct((m, d), BF16)],
        scratch_shapes=[pltpu.VMEM((d_a, d), BF16), pltpu.VMEM((d_b, d), BF16)],
        compiler_params=_params("arbitrary"),
        name="out_proj",
    )(ma, yb, x2d, gb, w_out, w_out, g2)


def _ffn_kernel(hp_ref, h_ref, hn_ref, x1_hbm, wg_ref, wv_ref, cw_ref, cb_ref, wd_ref, gf_ref,
                o_ref, hext_ref, x1_sem, *, seq_tiles, final_norm, d_ff):
    tm = h_ref.shape[0]
    tf = wg_ref.shape[1]
    i = pl.program_id(0)
    f = pl.program_id(1)
    x1_copy = pltpu.make_async_copy(x1_hbm.at[pl.ds(i * tm, tm), :], o_ref, x1_sem)

    nf = pl.cdiv(d_ff, tf)
    assert nf >= 2
    dup = nf * tf - d_ff

    def hidden_tile(first, last):
        cols = slice(dup, tf) if last else slice(0, tf)
        gext = jnp.dot(hext_ref[...], wg_ref[:, cols].astype(BF16), preferred_element_type=F32)
        g = _conv3(gext, cw_ref[:, cols], cb_ref[:, cols], tm)
        val = jnp.dot(h_ref[...], wv_ref[:, cols].astype(BF16), preferred_element_type=F32)
        act = (_gelu(g) * val).astype(BF16)
        if first:
            x1_copy.wait()
        o_ref[...] += jnp.dot(act, wd_ref[cols, :].astype(BF16), preferred_element_type=F32)

    @pl.when(f == 0)
    def _():
        x1_copy.start()
        _fill_ext(hext_ref, hp_ref[...], h_ref[...], hn_ref[...], i, seq_tiles)
        hidden_tile(True, False)

    @pl.when((f > 0) & (f < nf - 1))
    def _():
        hidden_tile(False, False)

    @pl.when(f == nf - 1)
    def _():
        hidden_tile(False, True)
        if final_norm:
            o_ref[...] = _rms(o_ref[...], gf_ref[...])


def _ragged_start(j, tf, d_ff, base=0):
    return (base // LANE + jnp.minimum(j * (tf // LANE), (d_ff - tf) // LANE)) * LANE


def _ffn(h2, x1, w_up, cw, cb, wd, gf, *, seq_len, tm, tf, final_norm):
    m, d = x1.shape
    d_ff = wd.shape[0]
    assert d_ff % LANE == 0 and tf % LANE == 0 and d_ff >= tf
    el = pl.Element
    start = functools.partial(_ragged_start, tf=tf, d_ff=d_ff)
    return pl.pallas_call(
        functools.partial(_ffn_kernel, seq_tiles=seq_len // tm, final_norm=final_norm, d_ff=d_ff),
        grid=(m // tm, pl.cdiv(d_ff, tf)),
        in_specs=_halo_specs(tm, d, m) + [
            pl.BlockSpec(memory_space=pl.ANY),
            pl.BlockSpec((el(d), el(tf)), lambda i, j: (0, start(j))),
            pl.BlockSpec((el(d), el(tf)), lambda i, j: (0, start(j, base=d_ff))),
            pl.BlockSpec((el(3), el(tf)), lambda i, j: (0, start(j))),
            pl.BlockSpec((el(1), el(tf)), lambda i, j: (0, start(j))),
            pl.BlockSpec((el(tf), el(d)), lambda i, j: (start(j), 0)),
            pl.BlockSpec((1, d), lambda i, j: (0, 0)),
        ],
        out_specs=pl.BlockSpec((tm, d), lambda i, j: (i, 0)),
        out_shape=jax.ShapeDtypeStruct((m, d), F32),
        scratch_shapes=[pltpu.VMEM((tm + 2 * HALO, d), BF16), pltpu.SemaphoreType.DMA(())],
        compiler_params=_params("parallel", "arbitrary"),
        name="ffn",
    )(h2, h2, h2, x1, w_up, w_up, cw, cb, wd, gf)


@functools.lru_cache(maxsize=None)
def _dft_constants(seq_len):
    n = 2 * seq_len
    r = DFT_R
    q = n // r
    hq = q // 2
    k1 = np.arange(hq, dtype=np.float64)[:, None]
    n1 = np.arange(hq, dtype=np.float64)[None, :]
    th1 = 2.0 * np.pi * n1 * (k1 + 0.5) / q
    f1r, f1i = np.cos(th1), -np.sin(th1)
    f1s = np.concatenate([f1r, f1i], axis=0)
    f1rt, f1it = (2.0 / n) * f1r.T, (2.0 / n) * f1i.T
    k2 = np.arange(r, dtype=np.float64)[None, :, None]
    n2 = np.arange(r, dtype=np.float64)[None, None, :]
    kk = np.arange(hq, dtype=np.float64)[:, None, None]
    th2 = 2.0 * np.pi * (n2 * k2 / r + n2 * (kk + 0.5) / n)
    gr, gi = np.cos(th2), -np.sin(th2)
    mf = np.concatenate([np.concatenate([gr, -gi], axis=2),
                         np.concatenate([gi, gr], axis=2)], axis=1)
    grt, git = gr.transpose(0, 2, 1), gi.transpose(0, 2, 1)
    mi = np.concatenate([np.concatenate([grt, git], axis=2),
                         np.concatenate([-git, grt], axis=2)], axis=1)
    return f1s, f1rt, f1it, mf, mi


def _positional_features(seq_len, e_pad):
    t = jnp.linspace(0.0, 1.0, seq_len, dtype=F32)[:, None]
    w = (2.0 * math.pi / seq_len) * jnp.arange(seq_len, dtype=F32)[:, None]
    f = jnp.linspace(1e-4, N_BANDS - 1, N_BANDS, dtype=F32)[None]
    z = jnp.concatenate([t, jnp.cos(f * w), -jnp.sin(f * w)], axis=-1)
    return jnp.pad(z, ((0, 0), (0, e_pad - z.shape[1])))


def kernel(x, norm1_g, w_in, sgu_norm_g, sgu_w, sgu_b, hy_conv_w, hy_conv_b, hy_f_w1, hy_f_b1,
           hy_f_w2, hy_f_b2, hy_f_w3, hy_f_b3, hy_f_freq, hy_f_wout, hy_d_skip, outnorm_a_g,
           outnorm_b_g, w_out, norm2_g, ffn_w_up, ffn_dw_w, ffn_dw_b, ffn_w_down, final_g):
    nb, seq_len, d = x.shape
    depth = w_in.shape[0]
    d_a = sgu_norm_g.shape[1]
    d_b = hy_d_skip.shape[1]
    d_ff = ffn_dw_b.shape[1]
    m = nb * seq_len
    r = DFT_R
    hq = seq_len // r
    g = DFT_G
    kb = min(DFT_G, hq)
    ct_dft = min(CT_DFT, d_b)

    f1s, f1rt, f1it, mf, mi = (jnp.asarray(a, F32).astype(BF16) for a in _dft_constants(seq_len))
    e_pad = hy_f_w2.shape[1]
    z = _positional_features(seq_len, e_pad)
    max_decay = math.log(DECAY_TARGET) / FAST_DECAY_PCT
    min_decay = math.log(DECAY_TARGET) / SLOW_DECAY_PCT
    deltas = jnp.abs(jnp.linspace(min_decay, max_decay, d_b, dtype=F32))
    deltas2 = jnp.concatenate([deltas, deltas])[None]

    x2d = x.reshape(m, d)
    row = lambda v: v.reshape(1, -1)
    for l in range(depth):
        w_in_l = w_in[l]
        g1 = row(norm1_g[l])

        bias_full = jnp.repeat(sgu_b[l].T, d_a // N_HEADS_A, axis=1)
        ma, h1, w_up_bf = _in_proj_a(x2d, g1, w_in_l, row(sgu_norm_g[l]), sgu_w[l].astype(BF16),
                                     bias_full, row(outnorm_a_g[l]), ffn_w_up[l], tm=TM_IN_A)

        x0, u = _in_proj_b(h1, w_in_l, hy_conv_w[l], row(hy_conv_b[l]),
                           col0=2 * d_a, d_b=d_b, seq_len=seq_len, tm=TM_IN_B, tn=TN_IN_B)
        u4 = u.reshape(nb, hq, r, d_b)
        x04 = x0.reshape(nb, hq, r, d_b)

        w1p = jnp.pad(hy_f_w1[l], ((0, e_pad - hy_f_w1.shape[1]), (0, 0)))
        h3 = _filter_feat(z, w1p, row(hy_f_b1[l]), hy_f_w2[l], row(hy_f_b2[l]), hy_f_w3[l],
                          row(hy_f_b3[l]), row(hy_f_freq[l]), tl=min(TL_FEAT, seq_len // 2))
        t_sw = z[:, 0].reshape(hq, r).T[:, :, None]
        ak, l1 = _filter_dft1(f1s, h3.reshape(hq, r, e_pad), t_sw, hy_f_wout[l], deltas2,
                              d_b=d_b, g=g, ct=ct_dft)
        khat = _filter_spec(ak.reshape(1, 2, hq, r, 2 * d_b), mf, l1, kb=kb, ct=CT_SPEC)

        a = _dft1(f1s, u4, g=g, ct=ct_dft, name="u_dft1")
        dr, di = _spec_conv(a.reshape(nb, 2, hq, r, d_b), mf, mi, khat, kb=kb, ct=CT_SPEC)
        yb = _conv_out(f1rt, f1it, dr, di, u4, x04, row(hy_d_skip[l]), g=g, ct=ct_dft)

        x1, h2 = _out_proj(ma, yb.reshape(m, d_b), x2d, row(outnorm_b_g[l]),
                           w_out[l], row(norm2_g[l]), tm=TM_OUT)

        x2d = _ffn(h2, x1, w_up_bf, ffn_dw_w[l], row(ffn_dw_b[l]),
                   ffn_w_down[l], row(final_g), seq_len=seq_len, tm=TM_FFN, tf=TF_FFN,
                   final_norm=(l == depth - 1))
    return x2d.reshape(nb, seq_len, d)
```

```python
import functools
import math

import numpy as np
import jax
import jax.numpy as jnp
from jax import lax
from jax.experimental import pallas as pl
from jax.experimental.pallas import tpu as pltpu

EPS = 1e-6
CHUNK = 128
N_HEADS_A = 8
DECAY_TARGET = 1e-2
FAST_DECAY_PCT = 0.3
SLOW_DECAY_PCT = 1.5
N_BANDS = 16

LANE = 128
BF16_SUBLANES = 16
DFT_R = 128
DFT_G = BF16_SUBLANES
HALO = BF16_SUBLANES
V7X_VMEM_BYTES = 64 * 1024 * 1024
VMEM_LIMIT = V7X_VMEM_BYTES * 7 // 8

TM_IN_A = 512
TM_IN_B = 1024
TN_IN_B = 512
TM_OUT = 512
TM_FFN = 1024
TF_FFN = 512
CT_DFT = 1024
CT_SPEC = 256
TL_FEAT = 512

F32 = jnp.float32
BF16 = jnp.bfloat16


def _gelu(x):
    return 0.5 * x * (1.0 + lax.erf(x * np.float32(math.sqrt(0.5))))


def _rms(x, g):
    return x * lax.rsqrt(jnp.mean(x * x, axis=-1, keepdims=True) + EPS) * g


def _params(*sem):
    return pltpu.CompilerParams(dimension_semantics=sem, vmem_limit_bytes=VMEM_LIMIT)


def _in_proj_a_kernel(x_ref, g1_ref, w_ref, gs_ref, ws_ref, bias_ref, go_ref, side_ref,
                      o_ref, h_ref, side_bf_ref, ya_ref, wbf_ref):
    tm = x_ref.shape[0]
    d_a = o_ref.shape[1]
    hd = d_a // N_HEADS_A
    side_bf_ref[...] = side_ref[...].astype(BF16)

    @pl.when(pl.program_id(0) == 0)
    def _():
        wbf_ref[...] = w_ref[...].astype(BF16)

    h = _rms(x_ref[...], g1_ref[...]).astype(BF16)
    h_ref[...] = h
    pv = jnp.dot(h, wbf_ref[:, d_a:], preferred_element_type=F32)
    zv = _rms(_gelu(pv), gs_ref[...]).astype(BF16)
    zu = _gelu(jnp.dot(h, wbf_ref[:, :d_a], preferred_element_type=F32))
    for c in range(tm // CHUNK):
        rows = slice(c * CHUNK, (c + 1) * CHUNK)
        for hh in range(N_HEADS_A):
            cols = slice(hh * hd, (hh + 1) * hd)
            s = jnp.dot(ws_ref[hh], zv[rows, cols], preferred_element_type=F32)
            ya_ref[rows, cols] = zu[rows, cols] * (s + bias_ref[:, cols])
    o_ref[...] = _rms(ya_ref[...], go_ref[...]).astype(o_ref.dtype)


def _in_proj_a(x2d, g1, w_a, gs, ws, bias_full, go, side, *, tm):
    m, d = x2d.shape
    d_a = gs.shape[1]
    side_rows = side.shape[0] // (m // tm)
    assert side_rows * (m // tm) == side.shape[0] and side_rows % 16 == 0
    side_spec = pl.BlockSpec((side_rows, side.shape[1]), lambda i: (i, 0))
    return pl.pallas_call(
        _in_proj_a_kernel,
        grid=(m // tm,),
        in_specs=[
            pl.BlockSpec((tm, d), lambda i: (i, 0)),
            pl.BlockSpec((1, d), lambda i: (0, 0)),
            pl.BlockSpec((d, 2 * d_a), lambda i: (0, 0), pipeline_mode=pl.Buffered(1)),
            pl.BlockSpec((1, d_a), lambda i: (0, 0)),
            pl.BlockSpec((N_HEADS_A, CHUNK, CHUNK), lambda i: (0, 0, 0)),
            pl.BlockSpec((CHUNK, d_a), lambda i: (0, 0)),
            pl.BlockSpec((1, d_a), lambda i: (0, 0)),
            side_spec,
        ],
        out_specs=[pl.BlockSpec((tm, d_a), lambda i: (i, 0)), pl.BlockSpec((tm, d), lambda i: (i, 0)),
                   side_spec],
        out_shape=[jax.ShapeDtypeStruct((m, d_a), BF16), jax.ShapeDtypeStruct((m, d), BF16),
                   jax.ShapeDtypeStruct(side.shape, BF16)],
        scratch_shapes=[pltpu.VMEM((tm, d_a), F32), pltpu.VMEM((d, 2 * d_a), BF16)],
        compiler_params=_params("arbitrary"),
        name="in_proj_a",
    )(x2d, g1, w_a, gs, ws, bias_full, go, side)


def _fill_ext(hext_ref, prev, main, nxt, i, seq_tiles):
    tm = main.shape[0]
    has_prev = (i % seq_tiles) != 0
    has_next = ((i + 1) % seq_tiles) != 0
    hext_ref[0:HALO, :] = jnp.where(has_prev, prev, jnp.zeros_like(prev))
    hext_ref[HALO:HALO + tm, :] = main
    hext_ref[HALO + tm:, :] = jnp.where(has_next, nxt, jnp.zeros_like(nxt))


def _conv3(gext, w_ref, b_ref, tm):
    n = gext.shape[0]
    prev = pltpu.roll(gext, 1, 0)[HALO:HALO + tm]
    nxt = pltpu.roll(gext, n - 1, 0)[HALO:HALO + tm]
    cur = gext[HALO:HALO + tm]
    return prev * w_ref[0:1, :] + cur * w_ref[1:2, :] + nxt * w_ref[2:3, :] + b_ref[...]


def _in_proj_b_kernel(hp_ref, h_ref, hn_ref, w0_ref, w1_ref, w2_ref,
                      cw0_ref, cw1_ref, cw2_ref, cb0_ref, cb1_ref, cb2_ref,
                      x0_ref, u_ref, hext_ref, *, seq_tiles):
    tm = h_ref.shape[0]
    i = pl.program_id(0)

    @pl.when(pl.program_id(1) == 0)
    def _():
        _fill_ext(hext_ref, hp_ref[...], h_ref[...], hn_ref[...], i, seq_tiles)

    hext = hext_ref[...]

    def branch(w_ref, cw_ref, cb_ref):
        gext = jnp.dot(hext, w_ref[...].astype(BF16), preferred_element_type=F32)
        return _conv3(gext, cw_ref, cb_ref, tm)

    x0_ref[...] = branch(w0_ref, cw0_ref, cb0_ref).astype(x0_ref.dtype)
    x1 = branch(w1_ref, cw1_ref, cb1_ref)
    v = branch(w2_ref, cw2_ref, cb2_ref)
    u_ref[...] = (v * x1).astype(u_ref.dtype)


def _halo_specs(tm, d, m):
    blocks_per_tile = tm // HALO
    last = m // HALO - 1
    return [
        pl.BlockSpec((HALO, d), lambda i, j: (jnp.maximum(i * blocks_per_tile - 1, 0), 0)),
        pl.BlockSpec((tm, d), lambda i, j: (i, 0)),
        pl.BlockSpec((HALO, d), lambda i, j: (jnp.minimum((i + 1) * blocks_per_tile, last), 0)),
    ]


def _in_proj_b(h, w_in, conv_w, conv_b, *, col0, d_b, seq_len, tm, tn):
    m, d = h.shape
    nj = d_b // tn
    w_specs = [pl.BlockSpec((d, tn), functools.partial(
        lambda i, j, off: (0, off + j), off=(col0 + g * d_b) // tn)) for g in range(3)]
    cw_specs = [pl.BlockSpec((3, tn), functools.partial(
        lambda i, j, off: (0, off + j), off=(g * d_b) // tn)) for g in range(3)]
    cb_specs = [pl.BlockSpec((1, tn), functools.partial(
        lambda i, j, off: (0, off + j), off=(g * d_b) // tn)) for g in range(3)]
    out_spec = pl.BlockSpec((tm, tn), lambda i, j: (i, j))
    return pl.pallas_call(
        functools.partial(_in_proj_b_kernel, seq_tiles=seq_len // tm),
        grid=(m // tm, nj),
        in_specs=_halo_specs(tm, d, m) + w_specs + cw_specs + cb_specs,
        out_specs=[out_spec, out_spec],
        out_shape=[jax.ShapeDtypeStruct((m, d_b), BF16)] * 2,
        scratch_shapes=[pltpu.VMEM((tm + 2 * HALO, d), BF16)],
        compiler_params=_params("parallel", "arbitrary"),
        name="in_proj_b",
    )(h, h, h, w_in, w_in, w_in, conv_w, conv_w, conv_w, conv_b, conv_b, conv_b)


def _filter_feat_kernel(z_ref, w1_ref, b1_ref, w2_ref, b2_ref, w3_ref, b3_ref, fr_ref, side_ref,
                        h_ref, side_bf_ref):
    hi = lax.Precision.HIGHEST
    side_bf_ref[...] = side_ref[...].astype(BF16)
    fr = fr_ref[...]
    h = z_ref[...]
    for w_ref, b_ref in ((w1_ref, b1_ref), (w2_ref, b2_ref), (w3_ref, b3_ref)):
        h = jnp.sin(fr * (jnp.dot(h, w_ref[...], precision=hi, preferred_element_type=F32) + b_ref[...]))
    h_ref[...] = h


def _filter_feat(z, w1, b1, w2, b2, w3, b3, fr, side, *, tl):
    seq_len, e = z.shape
    pack = LANE // e
    assert e * pack == LANE and seq_len % (pack * tl) == 0
    eye = jnp.eye(pack, dtype=F32)
    bd = lambda w: jnp.kron(eye, w)
    rep = lambda v: jnp.tile(v, (1, pack))
    full = lambda shape: pl.BlockSpec(shape, lambda i: (0, 0))
    rows = seq_len // pack
    steps = rows // tl
    side_rows = side.shape[0] // steps
    assert side_rows * steps == side.shape[0] and side_rows % BF16_SUBLANES == 0
    side_spec = pl.BlockSpec((side_rows, side.shape[1]), lambda i: (i, 0))
    h, side_bf = pl.pallas_call(
        _filter_feat_kernel,
        grid=(steps,),
        in_specs=[pl.BlockSpec((tl, LANE), lambda i: (i, 0)), full((LANE, LANE)), full((1, LANE)),
                  full((LANE, LANE)), full((1, LANE)), full((LANE, LANE)), full((1, LANE)),
                  full((1, LANE)), side_spec],
        out_specs=[pl.BlockSpec((tl, LANE), lambda i: (i, 0)), side_spec],
        out_shape=[jax.ShapeDtypeStruct((rows, LANE), F32), jax.ShapeDtypeStruct(side.shape, BF16)],
        compiler_params=_params("parallel"),
        name="filter_feat",
    )(z.reshape(rows, LANE), bd(w1), rep(b1), bd(w2), rep(b2), bd(w3), rep(b3), rep(fr), side)
    return h.reshape(seq_len, e), side_bf


def _dot_bf16x3(a, b):
    a_hi = a.astype(BF16)
    b_hi = b.astype(BF16)
    a_lo = (a - a_hi.astype(F32)).astype(BF16)
    b_lo = (b - b_hi.astype(F32)).astype(BF16)
    dot = functools.partial(jnp.dot, preferred_element_type=F32)
    return dot(a_hi, b_hi) + dot(a_lo, b_hi) + dot(a_hi, b_lo)


def _dft1_kernel(w_ref, x_ref, o_ref, a_ref):
    g = x_ref.shape[2]
    xs = jnp.swapaxes(x_ref[0].astype(BF16), 0, 1)
    w = w_ref[...]
    for j in range(g):
        a_ref[j] = jnp.dot(w, xs[j], preferred_element_type=F32).astype(a_ref.dtype)
    o_ref[0] = jnp.swapaxes(a_ref[...], 0, 1)


def _dft1(w, x4, *, g, ct, name):
    nb, hq, r, c = x4.shape
    mo = w.shape[0]
    return pl.pallas_call(
        _dft1_kernel,
        grid=(nb, r // g, c // ct),
        in_specs=[pl.BlockSpec((mo, hq), lambda b, i, j: (0, 0)),
                  pl.BlockSpec((1, hq, g, ct), lambda b, i, j: (b, 0, i, j))],
        out_specs=pl.BlockSpec((1, mo, g, ct), lambda b, i, j: (b, 0, i, j)),
        out_shape=jax.ShapeDtypeStruct((nb, mo, r, c), BF16),
        scratch_shapes=[pltpu.VMEM((g, mo, ct), BF16)],
        compiler_params=_params("parallel", "parallel", "parallel"),
        name=name,
    )(w, x4)


def _filter_dft1_kernel(w_ref, h3_ref, t_ref, wo_ref, dl_ref, o_ref, l1_ref, a_ref, *, d_b):
    hq, g, e = h3_ref.shape
    ct = wo_ref.shape[1]
    j = pl.program_id(0)
    gi = pl.program_id(1)
    h3 = jnp.swapaxes(h3_ref[...], 0, 1).reshape(g * hq, e)
    hw = _dot_bf16x3(h3, wo_ref[...])
    hw = hw * jnp.exp(-t_ref[...].reshape(g * hq, 1) * dl_ref[...])
    row = lax.broadcasted_iota(jnp.int32, hw.shape, 0)
    hw = jnp.where((row == 0) & (gi == 0) & (j * ct >= d_b), 0.0, hw)

    @pl.when(gi == 0)
    def _():
        l1_ref[...] = jnp.zeros_like(l1_ref)

    l1_ref[...] += jnp.sum(jnp.abs(hw), axis=0, keepdims=True)
    x3 = hw.reshape(g, hq, ct).astype(BF16)
    w = w_ref[...]
    for jj in range(g):
        a_ref[jj] = jnp.dot(w, x3[jj], preferred_element_type=F32).astype(a_ref.dtype)
    o_ref[0] = jnp.swapaxes(a_ref[...], 0, 1)


def _filter_dft1(w, h3, t_sw, wout, deltas2, *, d_b, g, ct):
    hq, r, e = h3.shape
    c2 = wout.shape[1]
    mo = w.shape[0]
    return pl.pallas_call(
        functools.partial(_filter_dft1_kernel, d_b=d_b),
        grid=(c2 // ct, r // g),
        in_specs=[pl.BlockSpec((mo, hq), lambda j, i: (0, 0)),
                  pl.BlockSpec((hq, g, e), lambda j, i: (0, i, 0)),
                  pl.BlockSpec((g, hq, 1), lambda j, i: (i, 0, 0)),
                  pl.BlockSpec((e, ct), lambda j, i: (0, j)),
                  pl.BlockSpec((1, ct), lambda j, i: (0, j))],
        out_specs=[pl.BlockSpec((1, mo, g, ct), lambda j, i: (0, 0, i, j)),
                   pl.BlockSpec((1, ct), lambda j, i: (0, j))],
        out_shape=[jax.ShapeDtypeStruct((1, mo, r, c2), BF16), jax.ShapeDtypeStruct((1, c2), F32)],
        scratch_shapes=[pltpu.VMEM((g, mo, ct), BF16)],
        compiler_params=_params("parallel", "arbitrary"),
        name="filter_dft1",
    )(w, h3, t_sw, wout, deltas2)


def _filter_spec_kernel(af_ref, ab_ref, mf_ref, l1f_ref, l1b_ref, k_ref):
    kb = mf_ref.shape[0]
    r = mf_ref.shape[1] // 2
    ct = k_ref.shape[2]
    scale = 1.0 / (l1f_ref[...] + l1b_ref[...] + EPS)

    for kk in range(kb):
        mfk = mf_ref[kk]
        xf = jnp.dot(mfk, af_ref[0, :, kk].reshape(2 * r, ct), preferred_element_type=F32)
        xb = jnp.dot(mfk, ab_ref[0, :, kk].reshape(2 * r, ct), preferred_element_type=F32)
        k_ref[kk, 0:r, :] = ((xf[:r] + xb[:r]) * scale).astype(k_ref.dtype)
        k_ref[kk, r:, :] = ((xf[r:] - xb[r:]) * scale).astype(k_ref.dtype)


def _filter_spec(ak5, mf, l1, *, kb, ct):
    _, _, hq, r, c2 = ak5.shape
    c = c2 // 2
    nct = c // ct
    return pl.pallas_call(
        _filter_spec_kernel,
        grid=(hq // kb, nct),
        in_specs=[pl.BlockSpec((1, 2, kb, r, ct), lambda i, j: (0, 0, i, 0, j)),
                  pl.BlockSpec((1, 2, kb, r, ct), lambda i, j: (0, 0, i, 0, j + nct)),
                  pl.BlockSpec((kb, 2 * r, 2 * r), lambda i, j: (i, 0, 0)),
                  pl.BlockSpec((1, ct), lambda i, j: (0, j)),
                  pl.BlockSpec((1, ct), lambda i, j: (0, j + nct))],
        out_specs=pl.BlockSpec((kb, 2 * r, ct), lambda i, j: (i, 0, j)),
        out_shape=jax.ShapeDtypeStruct((hq, 2 * r, c), BF16),
        compiler_params=_params("parallel", "parallel"),
        name="filter_spec",
    )(ak5, ak5, mf, l1, l1)


def _spec_conv_kernel(a_ref, mf_ref, mi_ref, k_ref, dr_ref, di_ref, d_scr):
    kb = mf_ref.shape[0]
    r = mf_ref.shape[1] // 2
    ct = k_ref.shape[2]

    for kk in range(kb):
        a = a_ref[0, :, kk].reshape(2 * r, ct)
        x = jnp.dot(mf_ref[kk], a, preferred_element_type=F32)
        xr, xi = x[:r].astype(BF16), x[r:].astype(BF16)
        kr, ki = k_ref[kk, 0:r, :], k_ref[kk, r:, :]
        y = jnp.concatenate([xr * kr - xi * ki, xr * ki + xi * kr], axis=0)
        dd = jnp.dot(mi_ref[kk], y, preferred_element_type=F32)
        d_scr[0, kk] = dd[:r].astype(d_scr.dtype)
        d_scr[1, kk] = dd[r:].astype(d_scr.dtype)
    dr_ref[0] = jnp.swapaxes(d_scr[0], 0, 1)
    di_ref[0] = jnp.swapaxes(d_scr[1], 0, 1)


def _spec_conv(a5, mf, mi, khat, *, kb, ct):
    nb, _, hq, r, c = a5.shape
    mspec = pl.BlockSpec((kb, 2 * r, 2 * r), lambda i, j, b: (i, 0, 0))
    ospec = pl.BlockSpec((1, r, kb, ct), lambda i, j, b: (b, 0, i, j))
    oshape = jax.ShapeDtypeStruct((nb, r, hq, c), BF16)
    return pl.pallas_call(
        _spec_conv_kernel,
        grid=(hq // kb, c // ct, nb),
        in_specs=[pl.BlockSpec((1, 2, kb, r, ct), lambda i, j, b: (b, 0, i, 0, j)), mspec, mspec,
                  pl.BlockSpec((kb, 2 * r, ct), lambda i, j, b: (i, 0, j))],
        out_specs=[ospec, ospec],
        out_shape=[oshape, oshape],
        scratch_shapes=[pltpu.VMEM((2, kb, r, ct), BF16)],
        compiler_params=_params("parallel", "parallel", "arbitrary"),
        name="spec_conv",
    )(a5, mf, mi, khat)


def _conv_out_kernel(wr_ref, wi_ref, dr_ref, di_ref, u_ref, x0_ref, ds_ref, o_ref, y_scr):
    g = dr_ref.shape[1]
    wr, wi = wr_ref[...], wi_ref[...]
    for j in range(g):
        y_scr[j] = (jnp.dot(wr, dr_ref[0, j], preferred_element_type=F32)
                    + jnp.dot(wi, di_ref[0, j], preferred_element_type=F32))
    yconv = jnp.swapaxes(y_scr[...], 0, 1)
    y_b = x0_ref[0].astype(F32) * (yconv + u_ref[0].astype(F32) * ds_ref[...])
    o_ref[0] = y_b.astype(o_ref.dtype)


def _conv_out(wr, wi, dr, di, u4, x04, ds, *, g, ct):
    nb, r, hq, c = dr.shape
    dspec = pl.BlockSpec((1, g, hq, ct), lambda b, i, j: (b, i, 0, j))
    nspec = pl.BlockSpec((1, hq, g, ct), lambda b, i, j: (b, 0, i, j))
    wspec = pl.BlockSpec((hq, hq), lambda b, i, j: (0, 0))
    return pl.pallas_call(
        _conv_out_kernel,
        grid=(nb, r // g, c // ct),
        in_specs=[wspec, wspec, dspec, dspec, nspec, nspec,
                  pl.BlockSpec((1, ct), lambda b, i, j: (0, j))],
        out_specs=nspec,
        out_shape=jax.ShapeDtypeStruct((nb, hq, r, c), BF16),
        scratch_shapes=[pltpu.VMEM((g, hq, ct), F32)],
        compiler_params=_params("parallel", "parallel", "parallel"),
        name="conv_out",
    )(wr, wi, dr, di, u4, x04, ds)


def _out_proj_kernel(ma_ref, yb_ref, x_ref, gb_ref, wa_ref, wb_ref, g2_ref, x1_ref, h2_ref,
                     wabf_ref, wbbf_ref):
    @pl.when(pl.program_id(0) == 0)
    def _():
        wabf_ref[...] = wa_ref[...].astype(BF16)
        wbbf_ref[...] = wb_ref[...].astype(BF16)

    mb = _rms(yb_ref[...].astype(F32), gb_ref[...]).astype(BF16)
    x1 = (x_ref[...]
          + jnp.dot(ma_ref[...], wabf_ref[...], preferred_element_type=F32)
          + jnp.dot(mb, wbbf_ref[...], preferred_element_type=F32))
    x1_ref[...] = x1
    h2_ref[...] = _rms(x1, g2_ref[...]).astype(h2_ref.dtype)


def _out_proj(ma, yb, x2d, gb, w_out, g2, *, tm):
    m, d = x2d.shape
    d_a = ma.shape[1]
    d_b = yb.shape[1]
    row = lambda w: pl.BlockSpec((tm, w), lambda i: (i, 0))
    return pl.pallas_call(
        _out_proj_kernel,
        grid=(m // tm,),
        in_specs=[row(d_a), row(d_b), row(d), pl.BlockSpec((1, d_b), lambda i: (0, 0)),
                  pl.BlockSpec((d_a, d), lambda i: (0, 0), pipeline_mode=pl.Buffered(1)),
                  pl.BlockSpec((d_b, d), lambda i: (d_a // d_b, 0), pipeline_mode=pl.Buffered(1)),
                  pl.BlockSpec((1, d), lambda i: (0, 0))],
        out_specs=[row(d), row(d)],
        out_shape=[jax.ShapeDtypeStruct((m, d), F32), jax.ShapeDtypeStruct((m, d), BF16)],
        scratch_shapes=[pltpu.VMEM((d_a, d), BF16), pltpu.VMEM((d_b, d), BF16)],
        compiler_params=_params("arbitrary"),
        name="out_proj",
    )(ma, yb, x2d, gb, w_out, w_out, g2)


def _ffn_kernel(hp_ref, h_ref, hn_ref, x1_hbm, wg_ref, wv_ref, cw_ref, cb_ref, wd_ref, gf_ref,
                o_ref, hext_ref, x1_sem, *, seq_tiles, final_norm, d_ff):
    tm = h_ref.shape[0]
    tf = wg_ref.shape[1]
    i = pl.program_id(0)
    f = pl.program_id(1)
    x1_copy = pltpu.make_async_copy(x1_hbm.at[pl.ds(i * tm, tm), :], o_ref, x1_sem)

    nf = pl.cdiv(d_ff, tf)
    assert nf >= 2
    dup = nf * tf - d_ff

    def hidden_tile(first, last):
        cols = slice(dup, tf) if last else slice(0, tf)
        gext = jnp.dot(hext_ref[...], wg_ref[:, cols].astype(BF16), preferred_element_type=F32)
        g = _conv3(gext, cw_ref[:, cols], cb_ref[:, cols], tm)
        val = jnp.dot(h_ref[...], wv_ref[:, cols].astype(BF16), preferred_element_type=F32)
        act = (_gelu(g) * val).astype(BF16)
        if first:
            x1_copy.wait()
        o_ref[...] += jnp.dot(act, wd_ref[cols, :].astype(BF16), preferred_element_type=F32)

    @pl.when(f == 0)
    def _():
        x1_copy.start()
        _fill_ext(hext_ref, hp_ref[...], h_ref[...], hn_ref[...], i, seq_tiles)
        hidden_tile(True, False)

    @pl.when((f > 0) & (f < nf - 1))
    def _():
        hidden_tile(False, False)

    @pl.when(f == nf - 1)
    def _():
        hidden_tile(False, True)
        if final_norm:
            o_ref[...] = _rms(o_ref[...], gf_ref[...])


def _ragged_start(j, tf, d_ff, base=0):
    return (base // LANE + jnp.minimum(j * (tf // LANE), (d_ff - tf) // LANE)) * LANE


def _ffn(h2, x1, w_up, cw, cb, wd, gf, *, seq_len, tm, tf, final_norm):
    m, d = x1.shape
    d_ff = wd.shape[0]
    assert d_ff % LANE == 0 and tf % LANE == 0 and d_ff >= tf
    el = pl.Element
    start = functools.partial(_ragged_start, tf=tf, d_ff=d_ff)
    return pl.pallas_call(
        functools.partial(_ffn_kernel, seq_tiles=seq_len // tm, final_norm=final_norm, d_ff=d_ff),
        grid=(m // tm, pl.cdiv(d_ff, tf)),
        in_specs=_halo_specs(tm, d, m) + [
            pl.BlockSpec(memory_space=pl.ANY),
            pl.BlockSpec((el(d), el(tf)), lambda i, j: (0, start(j))),
            pl.BlockSpec((el(d), el(tf)), lambda i, j: (0, start(j, base=d_ff))),
            pl.BlockSpec((el(3), el(tf)), lambda i, j: (0, start(j))),
            pl.BlockSpec((el(1), el(tf)), lambda i, j: (0, start(j))),
            pl.BlockSpec((el(tf), el(d)), lambda i, j: (start(j), 0)),
            pl.BlockSpec((1, d), lambda i, j: (0, 0)),
        ],
        out_specs=pl.BlockSpec((tm, d), lambda i, j: (i, 0)),
        out_shape=jax.ShapeDtypeStruct((m, d), F32),
        scratch_shapes=[pltpu.VMEM((tm + 2 * HALO, d), BF16), pltpu.SemaphoreType.DMA(())],
        compiler_params=_params("parallel", "arbitrary"),
        name="ffn",
    )(h2, h2, h2, x1, w_up, w_up, cw, cb, wd, gf)


@functools.lru_cache(maxsize=None)
def _dft_constants(seq_len):
    n = 2 * seq_len
    r = DFT_R
    q = n // r
    hq = q // 2
    k1 = np.arange(hq, dtype=np.float64)[:, None]
    n1 = np.arange(hq, dtype=np.float64)[None, :]
    th1 = 2.0 * np.pi * n1 * (k1 + 0.5) / q
    f1r, f1i = np.cos(th1), -np.sin(th1)
    f1s = np.concatenate([f1r, f1i], axis=0)
    f1rt, f1it = (2.0 / n) * f1r.T, (2.0 / n) * f1i.T
    k2 = np.arange(r, dtype=np.float64)[None, :, None]
    n2 = np.arange(r, dtype=np.float64)[None, None, :]
    kk = np.arange(hq, dtype=np.float64)[:, None, None]
    th2 = 2.0 * np.pi * (n2 * k2 / r + n2 * (kk + 0.5) / n)
    gr, gi = np.cos(th2), -np.sin(th2)
    mf = np.concatenate([np.concatenate([gr, -gi], axis=2),
                         np.concatenate([gi, gr], axis=2)], axis=1)
    grt, git = gr.transpose(0, 2, 1), gi.transpose(0, 2, 1)
    mi = np.concatenate([np.concatenate([grt, git], axis=2),
                         np.concatenate([-git, grt], axis=2)], axis=1)
    return f1s, f1rt, f1it, mf, mi


def _positional_features(seq_len, e_pad):
    t = jnp.linspace(0.0, 1.0, seq_len, dtype=F32)[:, None]
    w = (2.0 * math.pi / seq_len) * jnp.arange(seq_len, dtype=F32)[:, None]
    f = jnp.linspace(1e-4, N_BANDS - 1, N_BANDS, dtype=F32)[None]
    z = jnp.concatenate([t, jnp.cos(f * w), -jnp.sin(f * w)], axis=-1)
    return jnp.pad(z, ((0, 0), (0, e_pad - z.shape[1])))


def kernel(x, norm1_g, w_in, sgu_norm_g, sgu_w, sgu_b, hy_conv_w, hy_conv_b, hy_f_w1, hy_f_b1,
           hy_f_w2, hy_f_b2, hy_f_w3, hy_f_b3, hy_f_freq, hy_f_wout, hy_d_skip, outnorm_a_g,
           outnorm_b_g, w_out, norm2_g, ffn_w_up, ffn_dw_w, ffn_dw_b, ffn_w_down, final_g):
    nb, seq_len, d = x.shape
    depth = w_in.shape[0]
    d_a = sgu_norm_g.shape[1]
    d_b = hy_d_skip.shape[1]
    d_ff = ffn_dw_b.shape[1]
    m = nb * seq_len
    r = DFT_R
    hq = seq_len // r
    g = DFT_G
    kb = min(DFT_G, hq)
    ct_dft = min(CT_DFT, d_b)

    f1s, f1rt, f1it, mf, mi = (jnp.asarray(a, F32).astype(BF16) for a in _dft_constants(seq_len))
    e_pad = hy_f_w2.shape[1]
    z = _positional_features(seq_len, e_pad)
    max_decay = math.log(DECAY_TARGET) / FAST_DECAY_PCT
    min_decay = math.log(DECAY_TARGET) / SLOW_DECAY_PCT
    deltas = jnp.abs(jnp.linspace(min_decay, max_decay, d_b, dtype=F32))
    deltas2 = jnp.concatenate([deltas, deltas])[None]

    x2d = x.reshape(m, d)
    row = lambda v: v.reshape(1, -1)
    for l in range(depth):
        w_in_l = w_in[l]
        g1 = row(norm1_g[l])

        bias_full = jnp.repeat(sgu_b[l].T, d_a // N_HEADS_A, axis=1)
        ma, h1, w_up_bf = _in_proj_a(x2d, g1, w_in_l, row(sgu_norm_g[l]), sgu_w[l].astype(BF16),
                                     bias_full, row(outnorm_a_g[l]), ffn_w_up[l], tm=TM_IN_A)

        x0, u = _in_proj_b(h1, w_in_l, hy_conv_w[l], row(hy_conv_b[l]),
                           col0=2 * d_a, d_b=d_b, seq_len=seq_len, tm=TM_IN_B, tn=TN_IN_B)
        u4 = u.reshape(nb, hq, r, d_b)
        x04 = x0.reshape(nb, hq, r, d_b)

        w1p = jnp.pad(hy_f_w1[l], ((0, e_pad - hy_f_w1.shape[1]), (0, 0)))
        h3, w_down_bf = _filter_feat(z, w1p, row(hy_f_b1[l]), hy_f_w2[l], row(hy_f_b2[l]),
                                     hy_f_w3[l], row(hy_f_b3[l]), row(hy_f_freq[l]), ffn_w_down[l],
                                     tl=min(TL_FEAT, seq_len // 2))
        t_sw = z[:, 0].reshape(hq, r).T[:, :, None]
        ak, l1 = _filter_dft1(f1s, h3.reshape(hq, r, e_pad), t_sw, hy_f_wout[l], deltas2,
                              d_b=d_b, g=g, ct=ct_dft)
        khat = _filter_spec(ak.reshape(1, 2, hq, r, 2 * d_b), mf, l1, kb=kb, ct=CT_SPEC)

        a = _dft1(f1s, u4, g=g, ct=ct_dft, name="u_dft1")
        dr, di = _spec_conv(a.reshape(nb, 2, hq, r, d_b), mf, mi, khat, kb=kb, ct=CT_SPEC)
        yb = _conv_out(f1rt, f1it, dr, di, u4, x04, row(hy_d_skip[l]), g=g, ct=ct_dft)

        x1, h2 = _out_proj(ma, yb.reshape(m, d_b), x2d, row(outnorm_b_g[l]),
                           w_out[l], row(norm2_g[l]), tm=TM_OUT)

        x2d = _ffn(h2, x1, w_up_bf, ffn_dw_w[l], row(ffn_dw_b[l]),
                   w_down_bf, row(final_g), seq_len=seq_len, tm=TM_FFN, tf=TF_FFN,
                   final_norm=(l == depth - 1))
    return x2d.reshape(nb, seq_len, d)
```

```python
import functools
import math

import numpy as np
import jax
import jax.numpy as jnp
from jax import lax
from jax.experimental import pallas as pl
from jax.experimental.pallas import tpu as pltpu

EPS = 1e-6
CHUNK = 128
N_HEADS_A = 8
DECAY_TARGET = 1e-2
FAST_DECAY_PCT = 0.3
SLOW_DECAY_PCT = 1.5
N_BANDS = 16

LANE = 128
BF16_SUBLANES = 16
DFT_R = 128
DFT_G = BF16_SUBLANES
HALO = BF16_SUBLANES
V7X_VMEM_BYTES = 64 * 1024 * 1024
VMEM_LIMIT = V7X_VMEM_BYTES * 7 // 8

TM_IN_A = 512
TM_IN_B = 1024
TN_IN_B = 512
TM_OUT = 512
TM_FFN = 1024
TF_FFN = 768
CT_DFT = 1024
CT_SPEC = 256
TL_FEAT = 512

F32 = jnp.float32
BF16 = jnp.bfloat16


def _gelu(x):
    return 0.5 * x * (1.0 + lax.erf(x * np.float32(math.sqrt(0.5))))


def _rms(x, g):
    return x * lax.rsqrt(jnp.mean(x * x, axis=-1, keepdims=True) + EPS) * g


def _params(*sem):
    return pltpu.CompilerParams(dimension_semantics=sem, vmem_limit_bytes=VMEM_LIMIT)


def _in_proj_a_kernel(x_ref, g1_ref, w_ref, gs_ref, ws_ref, bias_ref, go_ref, side_ref,
                      o_ref, h_ref, side_bf_ref, ya_ref, wbf_ref):
    tm = x_ref.shape[0]
    d_a = o_ref.shape[1]
    hd = d_a // N_HEADS_A
    side_bf_ref[...] = side_ref[...].astype(BF16)

    @pl.when(pl.program_id(0) == 0)
    def _():
        wbf_ref[...] = w_ref[...].astype(BF16)

    h = _rms(x_ref[...], g1_ref[...]).astype(BF16)
    h_ref[...] = h
    pv = jnp.dot(h, wbf_ref[:, d_a:], preferred_element_type=F32)
    zv = _rms(_gelu(pv), gs_ref[...]).astype(BF16)
    zu = _gelu(jnp.dot(h, wbf_ref[:, :d_a], preferred_element_type=F32))
    for c in range(tm // CHUNK):
        rows = slice(c * CHUNK, (c + 1) * CHUNK)
        for hh in range(N_HEADS_A):
            cols = slice(hh * hd, (hh + 1) * hd)
            s = jnp.dot(ws_ref[hh], zv[rows, cols], preferred_element_type=F32)
            ya_ref[rows, cols] = zu[rows, cols] * (s + bias_ref[:, cols])
    o_ref[...] = _rms(ya_ref[...], go_ref[...]).astype(o_ref.dtype)


def _in_proj_a(x2d, g1, w_a, gs, ws, bias_full, go, side, *, tm):
    m, d = x2d.shape
    d_a = gs.shape[1]
    side_rows = side.shape[0] // (m // tm)
    assert side_rows * (m // tm) == side.shape[0] and side_rows % 16 == 0
    side_spec = pl.BlockSpec((side_rows, side.shape[1]), lambda i: (i, 0))
    return pl.pallas_call(
        _in_proj_a_kernel,
        grid=(m // tm,),
        in_specs=[
            pl.BlockSpec((tm, d), lambda i: (i, 0)),
            pl.BlockSpec((1, d), lambda i: (0, 0)),
            pl.BlockSpec((d, 2 * d_a), lambda i: (0, 0), pipeline_mode=pl.Buffered(1)),
            pl.BlockSpec((1, d_a), lambda i: (0, 0)),
            pl.BlockSpec((N_HEADS_A, CHUNK, CHUNK), lambda i: (0, 0, 0)),
            pl.BlockSpec((CHUNK, d_a), lambda i: (0, 0)),
            pl.BlockSpec((1, d_a), lambda i: (0, 0)),
            side_spec,
        ],
        out_specs=[pl.BlockSpec((tm, d_a), lambda i: (i, 0)), pl.BlockSpec((tm, d), lambda i: (i, 0)),
                   side_spec],
        out_shape=[jax.ShapeDtypeStruct((m, d_a), BF16), jax.ShapeDtypeStruct((m, d), BF16),
                   jax.ShapeDtypeStruct(side.shape, BF16)],
        scratch_shapes=[pltpu.VMEM((tm, d_a), F32), pltpu.VMEM((d, 2 * d_a), BF16)],
        compiler_params=_params("arbitrary"),
        name="in_proj_a",
    )(x2d, g1, w_a, gs, ws, bias_full, go, side)


def _fill_ext(hext_ref, prev, main, nxt, i, seq_tiles):
    tm = main.shape[0]
    has_prev = (i % seq_tiles) != 0
    has_next = ((i + 1) % seq_tiles) != 0
    hext_ref[0:HALO, :] = jnp.where(has_prev, prev, jnp.zeros_like(prev))
    hext_ref[HALO:HALO + tm, :] = main
    hext_ref[HALO + tm:, :] = jnp.where(has_next, nxt, jnp.zeros_like(nxt))


def _conv3(gext, w_ref, b_ref, tm):
    n = gext.shape[0]
    prev = pltpu.roll(gext, 1, 0)[HALO:HALO + tm]
    nxt = pltpu.roll(gext, n - 1, 0)[HALO:HALO + tm]
    cur = gext[HALO:HALO + tm]
    return prev * w_ref[0:1, :] + cur * w_ref[1:2, :] + nxt * w_ref[2:3, :] + b_ref[...]


def _in_proj_b_kernel(hp_ref, h_ref, hn_ref, w0_ref, w1_ref, w2_ref,
                      cw0_ref, cw1_ref, cw2_ref, cb0_ref, cb1_ref, cb2_ref,
                      x0_ref, u_ref, hext_ref, *, seq_tiles):
    tm = h_ref.shape[0]
    i = pl.program_id(0)

    @pl.when(pl.program_id(1) == 0)
    def _():
        _fill_ext(hext_ref, hp_ref[...], h_ref[...], hn_ref[...], i, seq_tiles)

    hext = hext_ref[...]

    def branch(w_ref, cw_ref, cb_ref):
        gext = jnp.dot(hext, w_ref[...].astype(BF16), preferred_element_type=F32)
        return _conv3(gext, cw_ref, cb_ref, tm)

    x0_ref[...] = branch(w0_ref, cw0_ref, cb0_ref).astype(x0_ref.dtype)
    x1 = branch(w1_ref, cw1_ref, cb1_ref)
    v = branch(w2_ref, cw2_ref, cb2_ref)
    u_ref[...] = (v * x1).astype(u_ref.dtype)


def _halo_specs(tm, d, m):
    blocks_per_tile = tm // HALO
    last = m // HALO - 1
    return [
        pl.BlockSpec((HALO, d), lambda i, j: (jnp.maximum(i * blocks_per_tile - 1, 0), 0)),
        pl.BlockSpec((tm, d), lambda i, j: (i, 0)),
        pl.BlockSpec((HALO, d), lambda i, j: (jnp.minimum((i + 1) * blocks_per_tile, last), 0)),
    ]


def _in_proj_b(h, w_in, conv_w, conv_b, *, col0, d_b, seq_len, tm, tn):
    m, d = h.shape
    nj = d_b // tn
    w_specs = [pl.BlockSpec((d, tn), functools.partial(
        lambda i, j, off: (0, off + j), off=(col0 + g * d_b) // tn)) for g in range(3)]
    cw_specs = [pl.BlockSpec((3, tn), functools.partial(
        lambda i, j, off: (0, off + j), off=(g * d_b) // tn)) for g in range(3)]
    cb_specs = [pl.BlockSpec((1, tn), functools.partial(
        lambda i, j, off: (0, off + j), off=(g * d_b) // tn)) for g in range(3)]
    out_spec = pl.BlockSpec((tm, tn), lambda i, j: (i, j))
    return pl.pallas_call(
        functools.partial(_in_proj_b_kernel, seq_tiles=seq_len // tm),
        grid=(m // tm, nj),
        in_specs=_halo_specs(tm, d, m) + w_specs + cw_specs + cb_specs,
        out_specs=[out_spec, out_spec],
        out_shape=[jax.ShapeDtypeStruct((m, d_b), BF16)] * 2,
        scratch_shapes=[pltpu.VMEM((tm + 2 * HALO, d), BF16)],
        compiler_params=_params("parallel", "arbitrary"),
        name="in_proj_b",
    )(h, h, h, w_in, w_in, w_in, conv_w, conv_w, conv_w, conv_b, conv_b, conv_b)


def _filter_feat_kernel(z_ref, w1_ref, b1_ref, w2_ref, b2_ref, w3_ref, b3_ref, fr_ref, side_ref,
                        h_ref, side_bf_ref):
    hi = lax.Precision.HIGHEST
    side_bf_ref[...] = side_ref[...].astype(BF16)
    fr = fr_ref[...]
    h = z_ref[...]
    for w_ref, b_ref in ((w1_ref, b1_ref), (w2_ref, b2_ref), (w3_ref, b3_ref)):
        h = jnp.sin(fr * (jnp.dot(h, w_ref[...], precision=hi, preferred_element_type=F32) + b_ref[...]))
    h_ref[...] = h


def _filter_feat(z, w1, b1, w2, b2, w3, b3, fr, side, *, tl):
    seq_len, e = z.shape
    pack = LANE // e
    assert e * pack == LANE and seq_len % (pack * tl) == 0
    eye = jnp.eye(pack, dtype=F32)
    bd = lambda w: jnp.kron(eye, w)
    rep = lambda v: jnp.tile(v, (1, pack))
    full = lambda shape: pl.BlockSpec(shape, lambda i: (0, 0))
    rows = seq_len // pack
    steps = rows // tl
    side_rows = side.shape[0] // steps
    assert side_rows * steps == side.shape[0] and side_rows % BF16_SUBLANES == 0
    side_spec = pl.BlockSpec((side_rows, side.shape[1]), lambda i: (i, 0))
    h, side_bf = pl.pallas_call(
        _filter_feat_kernel,
        grid=(steps,),
        in_specs=[pl.BlockSpec((tl, LANE), lambda i: (i, 0)), full((LANE, LANE)), full((1, LANE)),
                  full((LANE, LANE)), full((1, LANE)), full((LANE, LANE)), full((1, LANE)),
                  full((1, LANE)), side_spec],
        out_specs=[pl.BlockSpec((tl, LANE), lambda i: (i, 0)), side_spec],
        out_shape=[jax.ShapeDtypeStruct((rows, LANE), F32), jax.ShapeDtypeStruct(side.shape, BF16)],
        compiler_params=_params("parallel"),
        name="filter_feat",
    )(z.reshape(rows, LANE), bd(w1), rep(b1), bd(w2), rep(b2), bd(w3), rep(b3), rep(fr), side)
    return h.reshape(seq_len, e), side_bf


def _dot_bf16x3(a, b):
    a_hi = a.astype(BF16)
    b_hi = b.astype(BF16)
    a_lo = (a - a_hi.astype(F32)).astype(BF16)
    b_lo = (b - b_hi.astype(F32)).astype(BF16)
    dot = functools.partial(jnp.dot, preferred_element_type=F32)
    return dot(a_hi, b_hi) + dot(a_lo, b_hi) + dot(a_hi, b_lo)


def _dft1_kernel(w_ref, x_ref, o_ref, a_ref):
    g = x_ref.shape[2]
    xs = jnp.swapaxes(x_ref[0].astype(BF16), 0, 1)
    w = w_ref[...]
    for j in range(g):
        a_ref[j] = jnp.dot(w, xs[j], preferred_element_type=F32).astype(a_ref.dtype)
    o_ref[0] = jnp.swapaxes(a_ref[...], 0, 1)


def _dft1(w, x4, *, g, ct, name):
    nb, hq, r, c = x4.shape
    mo = w.shape[0]
    return pl.pallas_call(
        _dft1_kernel,
        grid=(nb, r // g, c // ct),
        in_specs=[pl.BlockSpec((mo, hq), lambda b, i, j: (0, 0)),
                  pl.BlockSpec((1, hq, g, ct), lambda b, i, j: (b, 0, i, j))],
        out_specs=pl.BlockSpec((1, mo, g, ct), lambda b, i, j: (b, 0, i, j)),
        out_shape=jax.ShapeDtypeStruct((nb, mo, r, c), BF16),
        scratch_shapes=[pltpu.VMEM((g, mo, ct), BF16)],
        compiler_params=_params("parallel", "parallel", "parallel"),
        name=name,
    )(w, x4)


def _filter_dft1_kernel(w_ref, h3_ref, t_ref, wo_ref, dl_ref, o_ref, l1_ref, a_ref, *, d_b):
    hq, g, e = h3_ref.shape
    ct = wo_ref.shape[1]
    j = pl.program_id(0)
    gi = pl.program_id(1)
    h3 = jnp.swapaxes(h3_ref[...], 0, 1).reshape(g * hq, e)
    hw = _dot_bf16x3(h3, wo_ref[...])
    hw = hw * jnp.exp(-t_ref[...].reshape(g * hq, 1) * dl_ref[...])
    row = lax.broadcasted_iota(jnp.int32, hw.shape, 0)
    hw = jnp.where((row == 0) & (gi == 0) & (j * ct >= d_b), 0.0, hw)

    @pl.when(gi == 0)
    def _():
        l1_ref[...] = jnp.zeros_like(l1_ref)

    l1_ref[...] += jnp.sum(jnp.abs(hw), axis=0, keepdims=True)
    x3 = hw.reshape(g, hq, ct).astype(BF16)
    w = w_ref[...]
    for jj in range(g):
        a_ref[jj] = jnp.dot(w, x3[jj], preferred_element_type=F32).astype(a_ref.dtype)
    o_ref[0] = jnp.swapaxes(a_ref[...], 0, 1)


def _filter_dft1(w, h3, t_sw, wout, deltas2, *, d_b, g, ct):
    hq, r, e = h3.shape
    c2 = wout.shape[1]
    mo = w.shape[0]
    return pl.pallas_call(
        functools.partial(_filter_dft1_kernel, d_b=d_b),
        grid=(c2 // ct, r // g),
        in_specs=[pl.BlockSpec((mo, hq), lambda j, i: (0, 0)),
                  pl.BlockSpec((hq, g, e), lambda j, i: (0, i, 0)),
                  pl.BlockSpec((g, hq, 1), lambda j, i: (i, 0, 0)),
                  pl.BlockSpec((e, ct), lambda j, i: (0, j)),
                  pl.BlockSpec((1, ct), lambda j, i: (0, j))],
        out_specs=[pl.BlockSpec((1, mo, g, ct), lambda j, i: (0, 0, i, j)),
                   pl.BlockSpec((1, ct), lambda j, i: (0, j))],
        out_shape=[jax.ShapeDtypeStruct((1, mo, r, c2), BF16), jax.ShapeDtypeStruct((1, c2), F32)],
        scratch_shapes=[pltpu.VMEM((g, mo, ct), BF16)],
        compiler_params=_params("parallel", "arbitrary"),
        name="filter_dft1",
    )(w, h3, t_sw, wout, deltas2)


def _filter_spec_kernel(af_ref, ab_ref, mf_ref, l1f_ref, l1b_ref, k_ref):
    kb = mf_ref.shape[0]
    r = mf_ref.shape[1] // 2
    ct = k_ref.shape[2]
    scale = 1.0 / (l1f_ref[...] + l1b_ref[...] + EPS)

    for kk in range(kb):
        mfk = mf_ref[kk]
        xf = jnp.dot(mfk, af_ref[0, :, kk].reshape(2 * r, ct), preferred_element_type=F32)
        xb = jnp.dot(mfk, ab_ref[0, :, kk].reshape(2 * r, ct), preferred_element_type=F32)
        k_ref[kk, 0:r, :] = ((xf[:r] + xb[:r]) * scale).astype(k_ref.dtype)
        k_ref[kk, r:, :] = ((xf[r:] - xb[r:]) * scale).astype(k_ref.dtype)


def _filter_spec(ak5, mf, l1, *, kb, ct):
    _, _, hq, r, c2 = ak5.shape
    c = c2 // 2
    nct = c // ct
    return pl.pallas_call(
        _filter_spec_kernel,
        grid=(hq // kb, nct),
        in_specs=[pl.BlockSpec((1, 2, kb, r, ct), lambda i, j: (0, 0, i, 0, j)),
                  pl.BlockSpec((1, 2, kb, r, ct), lambda i, j: (0, 0, i, 0, j + nct)),
                  pl.BlockSpec((kb, 2 * r, 2 * r), lambda i, j: (i, 0, 0)),
                  pl.BlockSpec((1, ct), lambda i, j: (0, j)),
                  pl.BlockSpec((1, ct), lambda i, j: (0, j + nct))],
        out_specs=pl.BlockSpec((kb, 2 * r, ct), lambda i, j: (i, 0, j)),
        out_shape=jax.ShapeDtypeStruct((hq, 2 * r, c), BF16),
        compiler_params=_params("parallel", "parallel"),
        name="filter_spec",
    )(ak5, ak5, mf, l1, l1)


def _spec_conv_kernel(a_ref, mf_ref, mi_ref, k_ref, dr_ref, di_ref, d_scr):
    kb = mf_ref.shape[0]
    r = mf_ref.shape[1] // 2
    ct = k_ref.shape[2]

    for kk in range(kb):
        a = a_ref[0, :, kk].reshape(2 * r, ct)
        x = jnp.dot(mf_ref[kk], a, preferred_element_type=F32)
        xr, xi = x[:r].astype(BF16), x[r:].astype(BF16)
        kr, ki = k_ref[kk, 0:r, :], k_ref[kk, r:, :]
        y = jnp.concatenate([xr * kr - xi * ki, xr * ki + xi * kr], axis=0)
        dd = jnp.dot(mi_ref[kk], y, preferred_element_type=F32)
        d_scr[0, kk] = dd[:r].astype(d_scr.dtype)
        d_scr[1, kk] = dd[r:].astype(d_scr.dtype)
    dr_ref[0] = jnp.swapaxes(d_scr[0], 0, 1)
    di_ref[0] = jnp.swapaxes(d_scr[1], 0, 1)


def _spec_conv(a5, mf, mi, khat, *, kb, ct):
    nb, _, hq, r, c = a5.shape
    mspec = pl.BlockSpec((kb, 2 * r, 2 * r), lambda i, j, b: (i, 0, 0))
    ospec = pl.BlockSpec((1, r, kb, ct), lambda i, j, b: (b, 0, i, j))
    oshape = jax.ShapeDtypeStruct((nb, r, hq, c), BF16)
    return pl.pallas_call(
        _spec_conv_kernel,
        grid=(hq // kb, c // ct, nb),
        in_specs=[pl.BlockSpec((1, 2, kb, r, ct), lambda i, j, b: (b, 0, i, 0, j)), mspec, mspec,
                  pl.BlockSpec((kb, 2 * r, ct), lambda i, j, b: (i, 0, j))],
        out_specs=[ospec, ospec],
        out_shape=[oshape, oshape],
        scratch_shapes=[pltpu.VMEM((2, kb, r, ct), BF16)],
        compiler_params=_params("parallel", "parallel", "arbitrary"),
        name="spec_conv",
    )(a5, mf, mi, khat)


def _conv_out_kernel(wr_ref, wi_ref, dr_ref, di_ref, u_ref, x0_ref, ds_ref, o_ref, y_scr):
    g = dr_ref.shape[1]
    wr, wi = wr_ref[...], wi_ref[...]
    for j in range(g):
        y_scr[j] = (jnp.dot(wr, dr_ref[0, j], preferred_element_type=F32)
                    + jnp.dot(wi, di_ref[0, j], preferred_element_type=F32))
    yconv = jnp.swapaxes(y_scr[...], 0, 1)
    y_b = x0_ref[0].astype(F32) * (yconv + u_ref[0].astype(F32) * ds_ref[...])
    o_ref[0] = y_b.astype(o_ref.dtype)


def _conv_out(wr, wi, dr, di, u4, x04, ds, *, g, ct):
    nb, r, hq, c = dr.shape
    dspec = pl.BlockSpec((1, g, hq, ct), lambda b, i, j: (b, i, 0, j))
    nspec = pl.BlockSpec((1, hq, g, ct), lambda b, i, j: (b, 0, i, j))
    wspec = pl.BlockSpec((hq, hq), lambda b, i, j: (0, 0))
    return pl.pallas_call(
        _conv_out_kernel,
        grid=(nb, r // g, c // ct),
        in_specs=[wspec, wspec, dspec, dspec, nspec, nspec,
                  pl.BlockSpec((1, ct), lambda b, i, j: (0, j))],
        out_specs=nspec,
        out_shape=jax.ShapeDtypeStruct((nb, hq, r, c), BF16),
        scratch_shapes=[pltpu.VMEM((g, hq, ct), F32)],
        compiler_params=_params("parallel", "parallel", "parallel"),
        name="conv_out",
    )(wr, wi, dr, di, u4, x04, ds)


def _out_proj_kernel(ma_ref, yb_ref, x_ref, gb_ref, wa_ref, wb_ref, g2_ref, x1_ref, h2_ref,
                     wabf_ref, wbbf_ref):
    @pl.when(pl.program_id(0) == 0)
    def _():
        wabf_ref[...] = wa_ref[...].astype(BF16)
        wbbf_ref[...] = wb_ref[...].astype(BF16)

    mb = _rms(yb_ref[...].astype(F32), gb_ref[...]).astype(BF16)
    x1 = (x_ref[...]
          + jnp.dot(ma_ref[...], wabf_ref[...], preferred_element_type=F32)
          + jnp.dot(mb, wbbf_ref[...], preferred_element_type=F32))
    x1_ref[...] = x1
    h2_ref[...] = _rms(x1, g2_ref[...]).astype(h2_ref.dtype)


def _out_proj(ma, yb, x2d, gb, w_out, g2, *, tm):
    m, d = x2d.shape
    d_a = ma.shape[1]
    d_b = yb.shape[1]
    row = lambda w: pl.BlockSpec((tm, w), lambda i: (i, 0))
    return pl.pallas_call(
        _out_proj_kernel,
        grid=(m // tm,),
        in_specs=[row(d_a), row(d_b), row(d), pl.BlockSpec((1, d_b), lambda i: (0, 0)),
                  pl.BlockSpec((d_a, d), lambda i: (0, 0), pipeline_mode=pl.Buffered(1)),
                  pl.BlockSpec((d_b, d), lambda i: (d_a // d_b, 0), pipeline_mode=pl.Buffered(1)),
                  pl.BlockSpec((1, d), lambda i: (0, 0))],
        out_specs=[row(d), row(d)],
        out_shape=[jax.ShapeDtypeStruct((m, d), F32), jax.ShapeDtypeStruct((m, d), BF16)],
        scratch_shapes=[pltpu.VMEM((d_a, d), BF16), pltpu.VMEM((d_b, d), BF16)],
        compiler_params=_params("arbitrary"),
        name="out_proj",
    )(ma, yb, x2d, gb, w_out, w_out, g2)


def _ffn_kernel(hp_ref, h_ref, hn_ref, x1_hbm, wg_ref, wv_ref, cw_ref, cb_ref, wd_ref, gf_ref,
                o_ref, hext_ref, x1_sem, *, seq_tiles, final_norm, d_ff):
    tm = h_ref.shape[0]
    tf = wg_ref.shape[1]
    i = pl.program_id(0)
    f = pl.program_id(1)
    x1_copy = pltpu.make_async_copy(x1_hbm.at[pl.ds(i * tm, tm), :], o_ref, x1_sem)

    nf = pl.cdiv(d_ff, tf)
    assert nf >= 2
    dup = nf * tf - d_ff

    def hidden_tile(first, last):
        cols = slice(dup, tf) if last else slice(0, tf)
        gext = jnp.dot(hext_ref[...], wg_ref[:, cols].astype(BF16), preferred_element_type=F32)
        g = _conv3(gext, cw_ref[:, cols], cb_ref[:, cols], tm)
        val = jnp.dot(h_ref[...], wv_ref[:, cols].astype(BF16), preferred_element_type=F32)
        act = (_gelu(g) * val).astype(BF16)
        if first:
            x1_copy.wait()
        o_ref[...] += jnp.dot(act, wd_ref[cols, :].astype(BF16), preferred_element_type=F32)

    @pl.when(f == 0)
    def _():
        x1_copy.start()
        _fill_ext(hext_ref, hp_ref[...], h_ref[...], hn_ref[...], i, seq_tiles)
        hidden_tile(True, False)

    @pl.when((f > 0) & (f < nf - 1))
    def _():
        hidden_tile(False, False)

    @pl.when(f == nf - 1)
    def _():
        hidden_tile(False, True)
        if final_norm:
            o_ref[...] = _rms(o_ref[...], gf_ref[...])


def _ragged_start(j, tf, d_ff, base=0):
    return (base // LANE + jnp.minimum(j * (tf // LANE), (d_ff - tf) // LANE)) * LANE


def _ffn(h2, x1, w_up, cw, cb, wd, gf, *, seq_len, tm, tf, final_norm):
    m, d = x1.shape
    d_ff = wd.shape[0]
    assert d_ff % LANE == 0 and tf % LANE == 0 and d_ff >= tf
    el = pl.Element
    start = functools.partial(_ragged_start, tf=tf, d_ff=d_ff)
    return pl.pallas_call(
        functools.partial(_ffn_kernel, seq_tiles=seq_len // tm, final_norm=final_norm, d_ff=d_ff),
        grid=(m // tm, pl.cdiv(d_ff, tf)),
        in_specs=_halo_specs(tm, d, m) + [
            pl.BlockSpec(memory_space=pl.ANY),
            pl.BlockSpec((el(d), el(tf)), lambda i, j: (0, start(j))),
            pl.BlockSpec((el(d), el(tf)), lambda i, j: (0, start(j, base=d_ff))),
            pl.BlockSpec((el(3), el(tf)), lambda i, j: (0, start(j))),
            pl.BlockSpec((el(1), el(tf)), lambda i, j: (0, start(j))),
            pl.BlockSpec((el(tf), el(d)), lambda i, j: (start(j), 0)),
            pl.BlockSpec((1, d), lambda i, j: (0, 0)),
        ],
        out_specs=pl.BlockSpec((tm, d), lambda i, j: (i, 0)),
        out_shape=jax.ShapeDtypeStruct((m, d), F32),
        scratch_shapes=[pltpu.VMEM((tm + 2 * HALO, d), BF16), pltpu.SemaphoreType.DMA(())],
        compiler_params=_params("parallel", "arbitrary"),
        name="ffn",
    )(h2, h2, h2, x1, w_up, w_up, cw, cb, wd, gf)


@functools.lru_cache(maxsize=None)
def _dft_constants(seq_len):
    n = 2 * seq_len
    r = DFT_R
    q = n // r
    hq = q // 2
    k1 = np.arange(hq, dtype=np.float64)[:, None]
    n1 = np.arange(hq, dtype=np.float64)[None, :]
    th1 = 2.0 * np.pi * n1 * (k1 + 0.5) / q
    f1r, f1i = np.cos(th1), -np.sin(th1)
    f1s = np.concatenate([f1r, f1i], axis=0)
    f1rt, f1it = (2.0 / n) * f1r.T, (2.0 / n) * f1i.T
    k2 = np.arange(r, dtype=np.float64)[None, :, None]
    n2 = np.arange(r, dtype=np.float64)[None, None, :]
    kk = np.arange(hq, dtype=np.float64)[:, None, None]
    th2 = 2.0 * np.pi * (n2 * k2 / r + n2 * (kk + 0.5) / n)
    gr, gi = np.cos(th2), -np.sin(th2)
    mf = np.concatenate([np.concatenate([gr, -gi], axis=2),
                         np.concatenate([gi, gr], axis=2)], axis=1)
    grt, git = gr.transpose(0, 2, 1), gi.transpose(0, 2, 1)
    mi = np.concatenate([np.concatenate([grt, git], axis=2),
                         np.concatenate([-git, grt], axis=2)], axis=1)
    return f1s, f1rt, f1it, mf, mi


def _positional_features(seq_len, e_pad):
    t = jnp.linspace(0.0, 1.0, seq_len, dtype=F32)[:, None]
    w = (2.0 * math.pi / seq_len) * jnp.arange(seq_len, dtype=F32)[:, None]
    f = jnp.linspace(1e-4, N_BANDS - 1, N_BANDS, dtype=F32)[None]
    z = jnp.concatenate([t, jnp.cos(f * w), -jnp.sin(f * w)], axis=-1)
    return jnp.pad(z, ((0, 0), (0, e_pad - z.shape[1])))


def kernel(x, norm1_g, w_in, sgu_norm_g, sgu_w, sgu_b, hy_conv_w, hy_conv_b, hy_f_w1, hy_f_b1,
           hy_f_w2, hy_f_b2, hy_f_w3, hy_f_b3, hy_f_freq, hy_f_wout, hy_d_skip, outnorm_a_g,
           outnorm_b_g, w_out, norm2_g, ffn_w_up, ffn_dw_w, ffn_dw_b, ffn_w_down, final_g):
    nb, seq_len, d = x.shape
    depth = w_in.shape[0]
    d_a = sgu_norm_g.shape[1]
    d_b = hy_d_skip.shape[1]
    d_ff = ffn_dw_b.shape[1]
    m = nb * seq_len
    r = DFT_R
    hq = seq_len // r
    g = DFT_G
    kb = min(DFT_G, hq)
    ct_dft = min(CT_DFT, d_b)

    f1s, f1rt, f1it, mf, mi = (jnp.asarray(a, F32).astype(BF16) for a in _dft_constants(seq_len))
    e_pad = hy_f_w2.shape[1]
    z = _positional_features(seq_len, e_pad)
    max_decay = math.log(DECAY_TARGET) / FAST_DECAY_PCT
    min_decay = math.log(DECAY_TARGET) / SLOW_DECAY_PCT
    deltas = jnp.abs(jnp.linspace(min_decay, max_decay, d_b, dtype=F32))
    deltas2 = jnp.concatenate([deltas, deltas])[None]

    x2d = x.reshape(m, d)
    row = lambda v: v.reshape(1, -1)
    for l in range(depth):
        w_in_l = w_in[l]
        g1 = row(norm1_g[l])

        bias_full = jnp.repeat(sgu_b[l].T, d_a // N_HEADS_A, axis=1)
        ma, h1, w_up_bf = _in_proj_a(x2d, g1, w_in_l, row(sgu_norm_g[l]), sgu_w[l].astype(BF16),
                                     bias_full, row(outnorm_a_g[l]), ffn_w_up[l], tm=TM_IN_A)

        x0, u = _in_proj_b(h1, w_in_l, hy_conv_w[l], row(hy_conv_b[l]),
                           col0=2 * d_a, d_b=d_b, seq_len=seq_len, tm=TM_IN_B, tn=TN_IN_B)
        u4 = u.reshape(nb, hq, r, d_b)
        x04 = x0.reshape(nb, hq, r, d_b)

        w1p = jnp.pad(hy_f_w1[l], ((0, e_pad - hy_f_w1.shape[1]), (0, 0)))
        h3, w_down_bf = _filter_feat(z, w1p, row(hy_f_b1[l]), hy_f_w2[l], row(hy_f_b2[l]),
                                     hy_f_w3[l], row(hy_f_b3[l]), row(hy_f_freq[l]), ffn_w_down[l],
                                     tl=min(TL_FEAT, seq_len // 2))
        t_sw = z[:, 0].reshape(hq, r).T[:, :, None]
        ak, l1 = _filter_dft1(f1s, h3.reshape(hq, r, e_pad), t_sw, hy_f_wout[l], deltas2,
                              d_b=d_b, g=g, ct=ct_dft)
        khat = _filter_spec(ak.reshape(1, 2, hq, r, 2 * d_b), mf, l1, kb=kb, ct=CT_SPEC)

        a = _dft1(f1s, u4, g=g, ct=ct_dft, name="u_dft1")
        dr, di = _spec_conv(a.reshape(nb, 2, hq, r, d_b), mf, mi, khat, kb=kb, ct=CT_SPEC)
        yb = _conv_out(f1rt, f1it, dr, di, u4, x04, row(hy_d_skip[l]), g=g, ct=ct_dft)

        x1, h2 = _out_proj(ma, yb.reshape(m, d_b), x2d, row(outnorm_b_g[l]),
                           w_out[l], row(norm2_g[l]), tm=TM_OUT)

        x2d = _ffn(h2, x1, w_up_bf, ffn_dw_w[l], row(ffn_dw_b[l]),
                   w_down_bf, row(final_g), seq_len=seq_len, tm=TM_FFN, tf=TF_FFN,
                   final_norm=(l == depth - 1))
    return x2d.reshape(nb, seq_len, d)
```

```python
import functools
import math

import numpy as np
import jax
import jax.numpy as jnp
from jax import lax
from jax.experimental import pallas as pl
from jax.experimental.pallas import tpu as pltpu

EPS = 1e-6
CHUNK = 128
N_HEADS_A = 8
DECAY_TARGET = 1e-2
FAST_DECAY_PCT = 0.3
SLOW_DECAY_PCT = 1.5
N_BANDS = 16

LANE = 128
BF16_SUBLANES = 16
DFT_R = 128
DFT_G = BF16_SUBLANES
HALO = BF16_SUBLANES
V7X_VMEM_BYTES = 64 * 1024 * 1024
VMEM_LIMIT = V7X_VMEM_BYTES * 7 // 8

TM_IN_A = 512
TM_IN_B = 1024
TN_IN_B = 512
TM_OUT = 512
TM_FFN = 1024
TF_FFN = 768
CT_DFT = 1024
CT_SPEC = 256
KB_SPEC = 32
TL_FEAT = 512

F32 = jnp.float32
BF16 = jnp.bfloat16


def _gelu(x):
    return 0.5 * x * (1.0 + lax.erf(x * np.float32(math.sqrt(0.5))))


def _rms(x, g):
    return x * lax.rsqrt(jnp.mean(x * x, axis=-1, keepdims=True) + EPS) * g


def _params(*sem):
    return pltpu.CompilerParams(dimension_semantics=sem, vmem_limit_bytes=VMEM_LIMIT)


def _in_proj_a_kernel(x_ref, g1_ref, w_ref, gs_ref, ws_ref, bias_ref, go_ref, side_ref,
                      o_ref, h_ref, side_bf_ref, ya_ref, wbf_ref):
    tm = x_ref.shape[0]
    d_a = o_ref.shape[1]
    hd = d_a // N_HEADS_A
    side_bf_ref[...] = side_ref[...].astype(BF16)

    @pl.when(pl.program_id(0) == 0)
    def _():
        wbf_ref[...] = w_ref[...].astype(BF16)

    h = _rms(x_ref[...], g1_ref[...]).astype(BF16)
    h_ref[...] = h
    pv = jnp.dot(h, wbf_ref[:, d_a:], preferred_element_type=F32)
    zv = _rms(_gelu(pv), gs_ref[...]).astype(BF16)
    zu = _gelu(jnp.dot(h, wbf_ref[:, :d_a], preferred_element_type=F32))
    for c in range(tm // CHUNK):
        rows = slice(c * CHUNK, (c + 1) * CHUNK)
        for hh in range(N_HEADS_A):
            cols = slice(hh * hd, (hh + 1) * hd)
            s = jnp.dot(ws_ref[hh], zv[rows, cols], preferred_element_type=F32)
            ya_ref[rows, cols] = zu[rows, cols] * (s + bias_ref[:, cols])
    o_ref[...] = _rms(ya_ref[...], go_ref[...]).astype(o_ref.dtype)


def _in_proj_a(x2d, g1, w_a, gs, ws, bias_full, go, side, *, tm):
    m, d = x2d.shape
    d_a = gs.shape[1]
    side_rows = side.shape[0] // (m // tm)
    assert side_rows * (m // tm) == side.shape[0] and side_rows % 16 == 0
    side_spec = pl.BlockSpec((side_rows, side.shape[1]), lambda i: (i, 0))
    return pl.pallas_call(
        _in_proj_a_kernel,
        grid=(m // tm,),
        in_specs=[
            pl.BlockSpec((tm, d), lambda i: (i, 0)),
            pl.BlockSpec((1, d), lambda i: (0, 0)),
            pl.BlockSpec((d, 2 * d_a), lambda i: (0, 0), pipeline_mode=pl.Buffered(1)),
            pl.BlockSpec((1, d_a), lambda i: (0, 0)),
            pl.BlockSpec((N_HEADS_A, CHUNK, CHUNK), lambda i: (0, 0, 0)),
            pl.BlockSpec((CHUNK, d_a), lambda i: (0, 0)),
            pl.BlockSpec((1, d_a), lambda i: (0, 0)),
            side_spec,
        ],
        out_specs=[pl.BlockSpec((tm, d_a), lambda i: (i, 0)), pl.BlockSpec((tm, d), lambda i: (i, 0)),
                   side_spec],
        out_shape=[jax.ShapeDtypeStruct((m, d_a), BF16), jax.ShapeDtypeStruct((m, d), BF16),
                   jax.ShapeDtypeStruct(side.shape, BF16)],
        scratch_shapes=[pltpu.VMEM((tm, d_a), F32), pltpu.VMEM((d, 2 * d_a), BF16)],
        compiler_params=_params("arbitrary"),
        name="in_proj_a",
    )(x2d, g1, w_a, gs, ws, bias_full, go, side)


def _fill_ext(hext_ref, prev, main, nxt, i, seq_tiles):
    tm = main.shape[0]
    has_prev = (i % seq_tiles) != 0
    has_next = ((i + 1) % seq_tiles) != 0
    hext_ref[0:HALO, :] = jnp.where(has_prev, prev, jnp.zeros_like(prev))
    hext_ref[HALO:HALO + tm, :] = main
    hext_ref[HALO + tm:, :] = jnp.where(has_next, nxt, jnp.zeros_like(nxt))


def _conv3(gext, w_ref, b_ref, tm):
    n = gext.shape[0]
    prev = pltpu.roll(gext, 1, 0)[HALO:HALO + tm]
    nxt = pltpu.roll(gext, n - 1, 0)[HALO:HALO + tm]
    cur = gext[HALO:HALO + tm]
    return prev * w_ref[0:1, :] + cur * w_ref[1:2, :] + nxt * w_ref[2:3, :] + b_ref[...]


def _in_proj_b_kernel(hp_ref, h_ref, hn_ref, w0_ref, w1_ref, w2_ref,
                      cw0_ref, cw1_ref, cw2_ref, cb0_ref, cb1_ref, cb2_ref,
                      x0_ref, u_ref, hext_ref, *, seq_tiles):
    tm = h_ref.shape[0]
    i = pl.program_id(0)

    @pl.when(pl.program_id(1) == 0)
    def _():
        _fill_ext(hext_ref, hp_ref[...], h_ref[...], hn_ref[...], i, seq_tiles)

    hext = hext_ref[...]

    def branch(w_ref, cw_ref, cb_ref):
        gext = jnp.dot(hext, w_ref[...].astype(BF16), preferred_element_type=F32)
        return _conv3(gext, cw_ref, cb_ref, tm)

    x0_ref[...] = branch(w0_ref, cw0_ref, cb0_ref).astype(x0_ref.dtype)
    x1 = branch(w1_ref, cw1_ref, cb1_ref)
    v = branch(w2_ref, cw2_ref, cb2_ref)
    u_ref[...] = (v * x1).astype(u_ref.dtype)


def _halo_specs(tm, d, m):
    blocks_per_tile = tm // HALO
    last = m // HALO - 1
    return [
        pl.BlockSpec((HALO, d), lambda i, j: (jnp.maximum(i * blocks_per_tile - 1, 0), 0)),
        pl.BlockSpec((tm, d), lambda i, j: (i, 0)),
        pl.BlockSpec((HALO, d), lambda i, j: (jnp.minimum((i + 1) * blocks_per_tile, last), 0)),
    ]


def _in_proj_b(h, w_in, conv_w, conv_b, *, col0, d_b, seq_len, tm, tn):
    m, d = h.shape
    nj = d_b // tn
    w_specs = [pl.BlockSpec((d, tn), functools.partial(
        lambda i, j, off: (0, off + j), off=(col0 + g * d_b) // tn)) for g in range(3)]
    cw_specs = [pl.BlockSpec((3, tn), functools.partial(
        lambda i, j, off: (0, off + j), off=(g * d_b) // tn)) for g in range(3)]
    cb_specs = [pl.BlockSpec((1, tn), functools.partial(
        lambda i, j, off: (0, off + j), off=(g * d_b) // tn)) for g in range(3)]
    out_spec = pl.BlockSpec((tm, tn), lambda i, j: (i, j))
    return pl.pallas_call(
        functools.partial(_in_proj_b_kernel, seq_tiles=seq_len // tm),
        grid=(m // tm, nj),
        in_specs=_halo_specs(tm, d, m) + w_specs + cw_specs + cb_specs,
        out_specs=[out_spec, out_spec],
        out_shape=[jax.ShapeDtypeStruct((m, d_b), BF16)] * 2,
        scratch_shapes=[pltpu.VMEM((tm + 2 * HALO, d), BF16)],
        compiler_params=_params("parallel", "arbitrary"),
        name="in_proj_b",
    )(h, h, h, w_in, w_in, w_in, conv_w, conv_w, conv_w, conv_b, conv_b, conv_b)


def _filter_feat_kernel(z_ref, w1_ref, b1_ref, w2_ref, b2_ref, w3_ref, b3_ref, fr_ref, side_ref,
                        h_ref, side_bf_ref):
    hi = lax.Precision.HIGHEST
    side_bf_ref[...] = side_ref[...].astype(BF16)
    fr = fr_ref[...]
    h = z_ref[...]
    for w_ref, b_ref in ((w1_ref, b1_ref), (w2_ref, b2_ref), (w3_ref, b3_ref)):
        h = jnp.sin(fr * (jnp.dot(h, w_ref[...], precision=hi, preferred_element_type=F32) + b_ref[...]))
    h_ref[...] = h


def _filter_feat(z, w1, b1, w2, b2, w3, b3, fr, side, *, tl):
    seq_len, e = z.shape
    pack = LANE // e
    assert e * pack == LANE and seq_len % (pack * tl) == 0
    eye = jnp.eye(pack, dtype=F32)
    bd = lambda w: jnp.kron(eye, w)
    rep = lambda v: jnp.tile(v, (1, pack))
    full = lambda shape: pl.BlockSpec(shape, lambda i: (0, 0))
    rows = seq_len // pack
    steps = rows // tl
    side_rows = side.shape[0] // steps
    assert side_rows * steps == side.shape[0] and side_rows % BF16_SUBLANES == 0
    side_spec = pl.BlockSpec((side_rows, side.shape[1]), lambda i: (i, 0))
    h, side_bf = pl.pallas_call(
        _filter_feat_kernel,
        grid=(steps,),
        in_specs=[pl.BlockSpec((tl, LANE), lambda i: (i, 0)), full((LANE, LANE)), full((1, LANE)),
                  full((LANE, LANE)), full((1, LANE)), full((LANE, LANE)), full((1, LANE)),
                  full((1, LANE)), side_spec],
        out_specs=[pl.BlockSpec((tl, LANE), lambda i: (i, 0)), side_spec],
        out_shape=[jax.ShapeDtypeStruct((rows, LANE), F32), jax.ShapeDtypeStruct(side.shape, BF16)],
        compiler_params=_params("parallel"),
        name="filter_feat",
    )(z.reshape(rows, LANE), bd(w1), rep(b1), bd(w2), rep(b2), bd(w3), rep(b3), rep(fr), side)
    return h.reshape(seq_len, e), side_bf


def _dot_bf16x3(a, b):
    a_hi = a.astype(BF16)
    b_hi = b.astype(BF16)
    a_lo = (a - a_hi.astype(F32)).astype(BF16)
    b_lo = (b - b_hi.astype(F32)).astype(BF16)
    dot = functools.partial(jnp.dot, preferred_element_type=F32)
    return dot(a_hi, b_hi) + dot(a_lo, b_hi) + dot(a_hi, b_lo)


def _dft1_kernel(w_ref, x_ref, o_ref, a_ref):
    g = x_ref.shape[2]
    xs = jnp.swapaxes(x_ref[0].astype(BF16), 0, 1)
    w = w_ref[...]
    for j in range(g):
        a_ref[j] = jnp.dot(w, xs[j], preferred_element_type=F32).astype(a_ref.dtype)
    o_ref[0] = jnp.swapaxes(a_ref[...], 0, 1)


def _dft1(w, x4, *, g, ct, name):
    nb, hq, r, c = x4.shape
    mo = w.shape[0]
    return pl.pallas_call(
        _dft1_kernel,
        grid=(nb, r // g, c // ct),
        in_specs=[pl.BlockSpec((mo, hq), lambda b, i, j: (0, 0)),
                  pl.BlockSpec((1, hq, g, ct), lambda b, i, j: (b, 0, i, j))],
        out_specs=pl.BlockSpec((1, mo, g, ct), lambda b, i, j: (b, 0, i, j)),
        out_shape=jax.ShapeDtypeStruct((nb, mo, r, c), BF16),
        scratch_shapes=[pltpu.VMEM((g, mo, ct), BF16)],
        compiler_params=_params("parallel", "parallel", "parallel"),
        name=name,
    )(w, x4)


def _filter_dft1_kernel(w_ref, h3_ref, t_ref, wo_ref, dl_ref, o_ref, l1_ref, a_ref, *, d_b):
    hq, g, e = h3_ref.shape
    ct = wo_ref.shape[1]
    j = pl.program_id(0)
    gi = pl.program_id(1)
    h3 = jnp.swapaxes(h3_ref[...], 0, 1).reshape(g * hq, e)
    hw = _dot_bf16x3(h3, wo_ref[...])
    hw = hw * jnp.exp(-t_ref[...].reshape(g * hq, 1) * dl_ref[...])
    row = lax.broadcasted_iota(jnp.int32, hw.shape, 0)
    hw = jnp.where((row == 0) & (gi == 0) & (j * ct >= d_b), 0.0, hw)

    @pl.when(gi == 0)
    def _():
        l1_ref[...] = jnp.zeros_like(l1_ref)

    l1_ref[...] += jnp.sum(jnp.abs(hw), axis=0, keepdims=True)
    x3 = hw.reshape(g, hq, ct).astype(BF16)
    w = w_ref[...]
    for jj in range(g):
        a_ref[jj] = jnp.dot(w, x3[jj], preferred_element_type=F32).astype(a_ref.dtype)
    o_ref[0] = jnp.swapaxes(a_ref[...], 0, 1)


def _filter_dft1(w, h3, t_sw, wout, deltas2, *, d_b, g, ct):
    hq, r, e = h3.shape
    c2 = wout.shape[1]
    mo = w.shape[0]
    return pl.pallas_call(
        functools.partial(_filter_dft1_kernel, d_b=d_b),
        grid=(c2 // ct, r // g),
        in_specs=[pl.BlockSpec((mo, hq), lambda j, i: (0, 0)),
                  pl.BlockSpec((hq, g, e), lambda j, i: (0, i, 0)),
                  pl.BlockSpec((g, hq, 1), lambda j, i: (i, 0, 0)),
                  pl.BlockSpec((e, ct), lambda j, i: (0, j)),
                  pl.BlockSpec((1, ct), lambda j, i: (0, j))],
        out_specs=[pl.BlockSpec((1, mo, g, ct), lambda j, i: (0, 0, i, j)),
                   pl.BlockSpec((1, ct), lambda j, i: (0, j))],
        out_shape=[jax.ShapeDtypeStruct((1, mo, r, c2), BF16), jax.ShapeDtypeStruct((1, c2), F32)],
        scratch_shapes=[pltpu.VMEM((g, mo, ct), BF16)],
        compiler_params=_params("parallel", "arbitrary"),
        name="filter_dft1",
    )(w, h3, t_sw, wout, deltas2)


def _filter_spec_kernel(af_ref, ab_ref, mf_ref, l1f_ref, l1b_ref, k_ref):
    kb = mf_ref.shape[0]
    r = mf_ref.shape[1] // 2
    ct = k_ref.shape[2]
    scale = 1.0 / (l1f_ref[...] + l1b_ref[...] + EPS)

    for kk in range(kb):
        mfk = mf_ref[kk]
        xf = jnp.dot(mfk, af_ref[0, :, kk].reshape(2 * r, ct), preferred_element_type=F32)
        xb = jnp.dot(mfk, ab_ref[0, :, kk].reshape(2 * r, ct), preferred_element_type=F32)
        k_ref[kk, 0:r, :] = ((xf[:r] + xb[:r]) * scale).astype(k_ref.dtype)
        k_ref[kk, r:, :] = ((xf[r:] - xb[r:]) * scale).astype(k_ref.dtype)


def _filter_spec(ak5, mf, l1, *, kb, ct):
    _, _, hq, r, c2 = ak5.shape
    c = c2 // 2
    nct = c // ct
    return pl.pallas_call(
        _filter_spec_kernel,
        grid=(hq // kb, nct),
        in_specs=[pl.BlockSpec((1, 2, kb, r, ct), lambda i, j: (0, 0, i, 0, j)),
                  pl.BlockSpec((1, 2, kb, r, ct), lambda i, j: (0, 0, i, 0, j + nct)),
                  pl.BlockSpec((kb, 2 * r, 2 * r), lambda i, j: (i, 0, 0)),
                  pl.BlockSpec((1, ct), lambda i, j: (0, j)),
                  pl.BlockSpec((1, ct), lambda i, j: (0, j + nct))],
        out_specs=pl.BlockSpec((kb, 2 * r, ct), lambda i, j: (i, 0, j)),
        out_shape=jax.ShapeDtypeStruct((hq, 2 * r, c), BF16),
        compiler_params=_params("parallel", "parallel"),
        name="filter_spec",
    )(ak5, ak5, mf, l1, l1)


def _spec_conv_kernel(a_ref, mf_ref, mi_ref, k_ref, dr_ref, di_ref, d_scr):
    kb = mf_ref.shape[0]
    r = mf_ref.shape[1] // 2
    ct = k_ref.shape[2]

    for kk in range(kb):
        a = a_ref[0, :, kk].reshape(2 * r, ct)
        x = jnp.dot(mf_ref[kk], a, preferred_element_type=F32)
        xr, xi = x[:r].astype(BF16), x[r:].astype(BF16)
        kr, ki = k_ref[kk, 0:r, :], k_ref[kk, r:, :]
        y = jnp.concatenate([xr * kr - xi * ki, xr * ki + xi * kr], axis=0)
        dd = jnp.dot(mi_ref[kk], y, preferred_element_type=F32)
        d_scr[0, kk] = dd[:r].astype(d_scr.dtype)
        d_scr[1, kk] = dd[r:].astype(d_scr.dtype)
    dr_ref[0] = jnp.swapaxes(d_scr[0], 0, 1)
    di_ref[0] = jnp.swapaxes(d_scr[1], 0, 1)


def _spec_conv(a5, mf, mi, khat, *, kb, ct):
    nb, _, hq, r, c = a5.shape
    mspec = pl.BlockSpec((kb, 2 * r, 2 * r), lambda i, j, b: (i, 0, 0))
    ospec = pl.BlockSpec((1, r, kb, ct), lambda i, j, b: (b, 0, i, j))
    oshape = jax.ShapeDtypeStruct((nb, r, hq, c), BF16)
    return pl.pallas_call(
        _spec_conv_kernel,
        grid=(hq // kb, c // ct, nb),
        in_specs=[pl.BlockSpec((1, 2, kb, r, ct), lambda i, j, b: (b, 0, i, 0, j)), mspec, mspec,
                  pl.BlockSpec((kb, 2 * r, ct), lambda i, j, b: (i, 0, j))],
        out_specs=[ospec, ospec],
        out_shape=[oshape, oshape],
        scratch_shapes=[pltpu.VMEM((2, kb, r, ct), BF16)],
        compiler_params=_params("parallel", "parallel", "arbitrary"),
        name="spec_conv",
    )(a5, mf, mi, khat)


def _conv_out_kernel(wr_ref, wi_ref, dr_ref, di_ref, u_ref, x0_ref, ds_ref, o_ref, y_scr):
    g = dr_ref.shape[1]
    wr, wi = wr_ref[...], wi_ref[...]
    for j in range(g):
        y_scr[j] = (jnp.dot(wr, dr_ref[0, j], preferred_element_type=F32)
                    + jnp.dot(wi, di_ref[0, j], preferred_element_type=F32))
    yconv = jnp.swapaxes(y_scr[...], 0, 1)
    y_b = x0_ref[0].astype(F32) * (yconv + u_ref[0].astype(F32) * ds_ref[...])
    o_ref[0] = y_b.astype(o_ref.dtype)


def _conv_out(wr, wi, dr, di, u4, x04, ds, *, g, ct):
    nb, r, hq, c = dr.shape
    dspec = pl.BlockSpec((1, g, hq, ct), lambda b, i, j: (b, i, 0, j))
    nspec = pl.BlockSpec((1, hq, g, ct), lambda b, i, j: (b, 0, i, j))
    wspec = pl.BlockSpec((hq, hq), lambda b, i, j: (0, 0))
    return pl.pallas_call(
        _conv_out_kernel,
        grid=(nb, r // g, c // ct),
        in_specs=[wspec, wspec, dspec, dspec, nspec, nspec,
                  pl.BlockSpec((1, ct), lambda b, i, j: (0, j))],
        out_specs=nspec,
        out_shape=jax.ShapeDtypeStruct((nb, hq, r, c), BF16),
        scratch_shapes=[pltpu.VMEM((g, hq, ct), F32)],
        compiler_params=_params("parallel", "parallel", "parallel"),
        name="conv_out",
    )(wr, wi, dr, di, u4, x04, ds)


def _out_proj_kernel(ma_ref, yb_ref, x_ref, gb_ref, wa_ref, wb_ref, g2_ref, x1_ref, h2_ref,
                     wabf_ref, wbbf_ref):
    @pl.when(pl.program_id(0) == 0)
    def _():
        wabf_ref[...] = wa_ref[...].astype(BF16)
        wbbf_ref[...] = wb_ref[...].astype(BF16)

    mb = _rms(yb_ref[...].astype(F32), gb_ref[...]).astype(BF16)
    x1 = (x_ref[...]
          + jnp.dot(ma_ref[...], wabf_ref[...], preferred_element_type=F32)
          + jnp.dot(mb, wbbf_ref[...], preferred_element_type=F32))
    x1_ref[...] = x1
    h2_ref[...] = _rms(x1, g2_ref[...]).astype(h2_ref.dtype)


def _out_proj(ma, yb, x2d, gb, w_out, g2, *, tm):
    m, d = x2d.shape
    d_a = ma.shape[1]
    d_b = yb.shape[1]
    row = lambda w: pl.BlockSpec((tm, w), lambda i: (i, 0))
    return pl.pallas_call(
        _out_proj_kernel,
        grid=(m // tm,),
        in_specs=[row(d_a), row(d_b), row(d), pl.BlockSpec((1, d_b), lambda i: (0, 0)),
                  pl.BlockSpec((d_a, d), lambda i: (0, 0), pipeline_mode=pl.Buffered(1)),
                  pl.BlockSpec((d_b, d), lambda i: (d_a // d_b, 0), pipeline_mode=pl.Buffered(1)),
                  pl.BlockSpec((1, d), lambda i: (0, 0))],
        out_specs=[row(d), row(d)],
        out_shape=[jax.ShapeDtypeStruct((m, d), F32), jax.ShapeDtypeStruct((m, d), BF16)],
        scratch_shapes=[pltpu.VMEM((d_a, d), BF16), pltpu.VMEM((d_b, d), BF16)],
        compiler_params=_params("arbitrary"),
        name="out_proj",
    )(ma, yb, x2d, gb, w_out, w_out, g2)


def _ffn_kernel(hp_ref, h_ref, hn_ref, x1_hbm, wg_ref, wv_ref, cw_ref, cb_ref, wd_ref, gf_ref,
                o_ref, hext_ref, x1_sem, *, seq_tiles, final_norm, d_ff):
    tm = h_ref.shape[0]
    tf = wg_ref.shape[1]
    i = pl.program_id(0)
    f = pl.program_id(1)
    x1_copy = pltpu.make_async_copy(x1_hbm.at[pl.ds(i * tm, tm), :], o_ref, x1_sem)

    nf = pl.cdiv(d_ff, tf)
    assert nf >= 2
    dup = nf * tf - d_ff

    def hidden_tile(first, last):
        cols = slice(dup, tf) if last else slice(0, tf)
        gext = jnp.dot(hext_ref[...], wg_ref[:, cols].astype(BF16), preferred_element_type=F32)
        g = _conv3(gext, cw_ref[:, cols], cb_ref[:, cols], tm)
        val = jnp.dot(h_ref[...], wv_ref[:, cols].astype(BF16), preferred_element_type=F32)
        act = (_gelu(g) * val).astype(BF16)
        if first:
            x1_copy.wait()
        o_ref[...] += jnp.dot(act, wd_ref[cols, :].astype(BF16), preferred_element_type=F32)

    @pl.when(f == 0)
    def _():
        x1_copy.start()
        _fill_ext(hext_ref, hp_ref[...], h_ref[...], hn_ref[...], i, seq_tiles)
        hidden_tile(True, False)

    @pl.when((f > 0) & (f < nf - 1))
    def _():
        hidden_tile(False, False)

    @pl.when(f == nf - 1)
    def _():
        hidden_tile(False, True)
        if final_norm:
            o_ref[...] = _rms(o_ref[...], gf_ref[...])


def _ragged_start(j, tf, d_ff, base=0):
    return (base // LANE + jnp.minimum(j * (tf // LANE), (d_ff - tf) // LANE)) * LANE


def _ffn(h2, x1, w_up, cw, cb, wd, gf, *, seq_len, tm, tf, final_norm):
    m, d = x1.shape
    d_ff = wd.shape[0]
    assert d_ff % LANE == 0 and tf % LANE == 0 and d_ff >= tf
    el = pl.Element
    start = functools.partial(_ragged_start, tf=tf, d_ff=d_ff)
    return pl.pallas_call(
        functools.partial(_ffn_kernel, seq_tiles=seq_len // tm, final_norm=final_norm, d_ff=d_ff),
        grid=(m // tm, pl.cdiv(d_ff, tf)),
        in_specs=_halo_specs(tm, d, m) + [
            pl.BlockSpec(memory_space=pl.ANY),
            pl.BlockSpec((el(d), el(tf)), lambda i, j: (0, start(j))),
            pl.BlockSpec((el(d), el(tf)), lambda i, j: (0, start(j, base=d_ff))),
            pl.BlockSpec((el(3), el(tf)), lambda i, j: (0, start(j))),
            pl.BlockSpec((el(1), el(tf)), lambda i, j: (0, start(j))),
            pl.BlockSpec((el(tf), el(d)), lambda i, j: (start(j), 0)),
            pl.BlockSpec((1, d), lambda i, j: (0, 0)),
        ],
        out_specs=pl.BlockSpec((tm, d), lambda i, j: (i, 0)),
        out_shape=jax.ShapeDtypeStruct((m, d), F32),
        scratch_shapes=[pltpu.VMEM((tm + 2 * HALO, d), BF16), pltpu.SemaphoreType.DMA(())],
        compiler_params=_params("parallel", "arbitrary"),
        name="ffn",
    )(h2, h2, h2, x1, w_up, w_up, cw, cb, wd, gf)


@functools.lru_cache(maxsize=None)
def _dft_constants(seq_len):
    n = 2 * seq_len
    r = DFT_R
    q = n // r
    hq = q // 2
    k1 = np.arange(hq, dtype=np.float64)[:, None]
    n1 = np.arange(hq, dtype=np.float64)[None, :]
    th1 = 2.0 * np.pi * n1 * (k1 + 0.5) / q
    f1r, f1i = np.cos(th1), -np.sin(th1)
    f1s = np.concatenate([f1r, f1i], axis=0)
    f1rt, f1it = (2.0 / n) * f1r.T, (2.0 / n) * f1i.T
    k2 = np.arange(r, dtype=np.float64)[None, :, None]
    n2 = np.arange(r, dtype=np.float64)[None, None, :]
    kk = np.arange(hq, dtype=np.float64)[:, None, None]
    th2 = 2.0 * np.pi * (n2 * k2 / r + n2 * (kk + 0.5) / n)
    gr, gi = np.cos(th2), -np.sin(th2)
    mf = np.concatenate([np.concatenate([gr, -gi], axis=2),
                         np.concatenate([gi, gr], axis=2)], axis=1)
    grt, git = gr.transpose(0, 2, 1), gi.transpose(0, 2, 1)
    mi = np.concatenate([np.concatenate([grt, git], axis=2),
                         np.concatenate([-git, grt], axis=2)], axis=1)
    return f1s, f1rt, f1it, mf, mi


def _positional_features(seq_len, e_pad):
    t = jnp.linspace(0.0, 1.0, seq_len, dtype=F32)[:, None]
    w = (2.0 * math.pi / seq_len) * jnp.arange(seq_len, dtype=F32)[:, None]
    f = jnp.linspace(1e-4, N_BANDS - 1, N_BANDS, dtype=F32)[None]
    z = jnp.concatenate([t, jnp.cos(f * w), -jnp.sin(f * w)], axis=-1)
    return jnp.pad(z, ((0, 0), (0, e_pad - z.shape[1])))


def kernel(x, norm1_g, w_in, sgu_norm_g, sgu_w, sgu_b, hy_conv_w, hy_conv_b, hy_f_w1, hy_f_b1,
           hy_f_w2, hy_f_b2, hy_f_w3, hy_f_b3, hy_f_freq, hy_f_wout, hy_d_skip, outnorm_a_g,
           outnorm_b_g, w_out, norm2_g, ffn_w_up, ffn_dw_w, ffn_dw_b, ffn_w_down, final_g):
    nb, seq_len, d = x.shape
    depth = w_in.shape[0]
    d_a = sgu_norm_g.shape[1]
    d_b = hy_d_skip.shape[1]
    d_ff = ffn_dw_b.shape[1]
    m = nb * seq_len
    r = DFT_R
    hq = seq_len // r
    g = DFT_G
    kb = min(KB_SPEC, hq)
    ct_dft = min(CT_DFT, d_b)

    f1s, f1rt, f1it, mf, mi = (jnp.asarray(a, F32).astype(BF16) for a in _dft_constants(seq_len))
    e_pad = hy_f_w2.shape[1]
    z = _positional_features(seq_len, e_pad)
    max_decay = math.log(DECAY_TARGET) / FAST_DECAY_PCT
    min_decay = math.log(DECAY_TARGET) / SLOW_DECAY_PCT
    deltas = jnp.abs(jnp.linspace(min_decay, max_decay, d_b, dtype=F32))
    deltas2 = jnp.concatenate([deltas, deltas])[None]

    x2d = x.reshape(m, d)
    row = lambda v: v.reshape(1, -1)
    for l in range(depth):
        w_in_l = w_in[l]
        g1 = row(norm1_g[l])

        bias_full = jnp.repeat(sgu_b[l].T, d_a // N_HEADS_A, axis=1)
        ma, h1, w_up_bf = _in_proj_a(x2d, g1, w_in_l, row(sgu_norm_g[l]), sgu_w[l].astype(BF16),
                                     bias_full, row(outnorm_a_g[l]), ffn_w_up[l], tm=TM_IN_A)

        x0, u = _in_proj_b(h1, w_in_l, hy_conv_w[l], row(hy_conv_b[l]),
                           col0=2 * d_a, d_b=d_b, seq_len=seq_len, tm=TM_IN_B, tn=TN_IN_B)
        u4 = u.reshape(nb, hq, r, d_b)
        x04 = x0.reshape(nb, hq, r, d_b)

        w1p = jnp.pad(hy_f_w1[l], ((0, e_pad - hy_f_w1.shape[1]), (0, 0)))
        h3, w_down_bf = _filter_feat(z, w1p, row(hy_f_b1[l]), hy_f_w2[l], row(hy_f_b2[l]),
                                     hy_f_w3[l], row(hy_f_b3[l]), row(hy_f_freq[l]), ffn_w_down[l],
                                     tl=min(TL_FEAT, seq_len // 2))
        t_sw = z[:, 0].reshape(hq, r).T[:, :, None]
        ak, l1 = _filter_dft1(f1s, h3.reshape(hq, r, e_pad), t_sw, hy_f_wout[l], deltas2,
                              d_b=d_b, g=g, ct=ct_dft)
        khat = _filter_spec(ak.reshape(1, 2, hq, r, 2 * d_b), mf, l1, kb=kb, ct=CT_SPEC)

        a = _dft1(f1s, u4, g=g, ct=ct_dft, name="u_dft1")
        dr, di = _spec_conv(a.reshape(nb, 2, hq, r, d_b), mf, mi, khat, kb=kb, ct=CT_SPEC)
        yb = _conv_out(f1rt, f1it, dr, di, u4, x04, row(hy_d_skip[l]), g=g, ct=ct_dft)

        x1, h2 = _out_proj(ma, yb.reshape(m, d_b), x2d, row(outnorm_b_g[l]),
                           w_out[l], row(norm2_g[l]), tm=TM_OUT)

        x2d = _ffn(h2, x1, w_up_bf, ffn_dw_w[l], row(ffn_dw_b[l]),
                   w_down_bf, row(final_g), seq_len=seq_len, tm=TM_FFN, tf=TF_FFN,
                   final_norm=(l == depth - 1))
    return x2d.reshape(nb, seq_len, d)
```
